```python
import jax, jax.numpy as jnp
from jax import lax
import numpy as np

D_MODEL = 1024
BATCH = 8
SEQ = 2048
DEPTH = 4

GRID_W = 64
CTX_LEN = 256
EPS = 1e-6
N_Q_HEADS = 8
N_KV_HEADS = 2
HEAD_DIM = 64
WINDOW = 128
BLOCK = 128
ROPE_BASE = 10000.0
SCONV_DIM = 512
SCONV_WIDTH = 3
POOL_DIM = 512
N_POOL_GROUPS = 4
POOL_GROUP = POOL_DIM // N_POOL_GROUPS
POOL_WINDOWS = (2, 4, 8, 16)
FFT_DIM = 512
N_FFT_GROUPS = 4
FFT_GROUP = FFT_DIM // N_FFT_GROUPS
D_FF = 2816
FFN_CONV_WIDTH = 3

Q_DIM = N_Q_HEADS * HEAD_DIM
KV_DIM = N_KV_HEADS * HEAD_DIM
ATT_IN = Q_DIM + 2 * KV_DIM + 3 * SCONV_DIM
EVEN_SPLITS = (Q_DIM, Q_DIM + KV_DIM, Q_DIM + 2 * KV_DIM, Q_DIM + 2 * KV_DIM + SCONV_DIM, Q_DIM + 2 * KV_DIM + 2 * SCONV_DIM)
MIX_OUT_EVEN = Q_DIM + SCONV_DIM
MIX_IN_ODD = POOL_DIM + FFT_DIM
N_EVEN = (DEPTH + 1) // 2
N_ODD = DEPTH // 2

kernel_name = 'hybrid_swa_sconv_pool_fourier_dit'


def rmsnorm(x, g):
    xf = x.astype(jnp.float32)
    y = xf * lax.rsqrt(jnp.mean(xf * xf, axis=-1, keepdims=True) + EPS)
    return (y * g.astype(jnp.float32)).astype(x.dtype)


def dwconv_centred(u, w):
    k = w.shape[0]
    half = k // 2
    t = u.shape[1]
    up = jnp.pad(u, ((0, 0), (half, k - 1 - half), (0, 0)))
    return sum(up[:, j:j + t] * w[j] for j in range(k))


def _rotate(x, ang):
    f = ang.shape[-1]
    cos = jnp.cos(ang)[:, None, :].astype(x.dtype)
    sin = jnp.sin(ang)[:, None, :].astype(x.dtype)
    x1, x2 = x[..., :f], x[..., f:]
    return jnp.concatenate([x1 * cos - x2 * sin, x2 * cos + x1 * sin], axis=-1)


def rope_2d(x, row, col):
    half = HEAD_DIM // 2
    nf = half // 2
    inv = ROPE_BASE ** (-jnp.arange(nf, dtype=jnp.float32) / nf)
    ang_r = row.astype(jnp.float32)[:, None] * inv
    ang_c = col.astype(jnp.float32)[:, None] * inv
    return jnp.concatenate([_rotate(x[..., :half], ang_r), _rotate(x[..., half:], ang_c)], axis=-1)


def sink_softmax(logits, sink_b):
    m = jnp.maximum(jnp.max(logits, axis=-1, keepdims=True), sink_b)
    e = jnp.exp(logits - m)
    return e / (jnp.sum(e, axis=-1, keepdims=True) + jnp.exp(sink_b - m))


def window_attention(q, k, v, kc, vc, sink):
    b, s = q.shape[:2]
    nb = s // BLOCK
    g = N_Q_HEADS // N_KV_HEADS
    scale = HEAD_DIM ** -0.5
    qb = q.reshape(b, nb, BLOCK, N_KV_HEADS, g, HEAD_DIM)

    def band(t):
        tp = jnp.pad(t, ((0, 0), (BLOCK, BLOCK), (0, 0), (0, 0))).reshape(b, nb + 2, BLOCK, N_KV_HEADS, HEAD_DIM)
        return jnp.concatenate([tp[:, :-2], tp[:, 1:-1], tp[:, 2:]], axis=2)

    kb, vb = band(k), band(v)
    s_win = jnp.einsum('bnqhgd,bnkhd->bnhgqk', qb, kb).astype(jnp.float32) * scale
    qpos = jnp.arange(nb)[:, None] * BLOCK + jnp.arange(BLOCK)[None, :]
    kpos = jnp.arange(nb)[:, None] * BLOCK - BLOCK + jnp.arange(3 * BLOCK)[None, :]
    valid = (jnp.abs(qpos[:, :, None] - kpos[:, None, :]) <= WINDOW) & (kpos[:, None, :] >= 0) & (kpos[:, None, :] < s)
    s_win = jnp.where(valid[None, :, None, None, :, :], s_win, -jnp.inf)
    s_ctx = jnp.einsum('bnqhgd,blhd->bnhgql', qb, kc).astype(jnp.float32) * scale
    sink_b = sink.astype(jnp.float32).reshape(1, 1, N_KV_HEADS, g, 1, 1)
    p = sink_softmax(jnp.concatenate([s_win, s_ctx], axis=-1), sink_b).astype(v.dtype)
    o = (jnp.einsum('bnhgqk,bnkhd->bnqhgd', p[..., :3 * BLOCK], vb)
         + jnp.einsum('bnhgql,blhd->bnqhgd', p[..., 3 * BLOCK:], vc))
    return o.reshape(b, s, Q_DIM)


def context_attention(q, k, v, sink):
    b, l = q.shape[:2]
    g = N_Q_HEADS // N_KV_HEADS
    qg = q.reshape(b, l, N_KV_HEADS, g, HEAD_DIM)
    s = jnp.einsum('blhgd,bmhd->bhglm', qg, k).astype(jnp.float32) * (HEAD_DIM ** -0.5)
    sink_b = sink.astype(jnp.float32).reshape(1, N_KV_HEADS, g, 1, 1)
    p = sink_softmax(s, sink_b).astype(v.dtype)
    o = jnp.einsum('bhglm,bmhd->blhgd', p, v)
    return o.reshape(b, l, Q_DIM)


def even_mixer(hx, hc, w_in, sink, conv_w, w_out, row, col, ctx_out):
    b, s = hx.shape[:2]
    l = hc.shape[1]
    qx, kx, vx, bx, cx, zx = jnp.split(hx @ w_in, EVEN_SPLITS, axis=-1)
    qx = rope_2d(qx.reshape(b, s, N_Q_HEADS, HEAD_DIM), row, col)
    kx = rope_2d(kx.reshape(b, s, N_KV_HEADS, HEAD_DIM), row, col)
    vx = vx.reshape(b, s, N_KV_HEADS, HEAD_DIM)
    if ctx_out:
        qc, kc, vc, bc, cc, zc = jnp.split(hc @ w_in, EVEN_SPLITS, axis=-1)
    else:
        kc, vc = jnp.split(hc @ w_in[:, Q_DIM:Q_DIM + 2 * KV_DIM], 2, axis=-1)
    kc = kc.reshape(b, l, N_KV_HEADS, HEAD_DIM)
    vc = vc.reshape(b, l, N_KV_HEADS, HEAD_DIM)
    ax = window_attention(qx, kx, vx, kc, vc, sink)
    sx = bx * dwconv_centred(cx * zx, conv_w)
    yx = jnp.concatenate([ax, sx], axis=-1) @ w_out
    if not ctx_out:
        return yx, None
    ac = context_attention(qc.reshape(b, l, N_Q_HEADS, HEAD_DIM), kc, vc, sink)
    s_c = bc * dwconv_centred(cc * zc, conv_w)
    yc = jnp.concatenate([ac, s_c], axis=-1) @ w_out
    return yx, yc


def pool_mix(u, w_grp, scale):
    b, t = u.shape[:2]
    uf = u.astype(jnp.float32)
    cs = jnp.pad(jnp.cumsum(uf, axis=1), ((0, 0), (1, 0), (0, 0)))
    pos = jnp.arange(t)
    outs = []
    for gi, w in enumerate(POOL_WINDOWS):
        lo = jnp.clip(pos - w // 2, 0, t)
        hi = jnp.clip(pos + w - w // 2, 0, t)
        sl = slice(gi * POOL_GROUP, (gi + 1) * POOL_GROUP)
        seg = cs[:, :, sl]
        mean = (seg[:, hi] - seg[:, lo]) / (hi - lo).astype(jnp.float32)[:, None]
        outs.append(mean - uf[:, :, sl])
    p = jnp.stack(outs, axis=2).astype(u.dtype)
    y = jnp.einsum('btgc,gcd->btgd', p, w_grp).reshape(b, t, POOL_DIM)
    return y * scale


def fourier_mix(u):
    b, t = u.shape[:2]
    ug = u.astype(jnp.float32).reshape(b, t, N_FFT_GROUPS, FFT_GROUP)
    f = jnp.fft.fftn(ug, axes=(1, 3), norm='ortho').real
    return f.reshape(b, t, FFT_DIM).astype(u.dtype)


def odd_mixer(h, w_in, w_grp, scale, w_out):
    u = h @ w_in
    y = jnp.concatenate([pool_mix(u[..., :POOL_DIM], w_grp, scale), fourier_mix(u[..., POOL_DIM:])], axis=-1)
    return y @ w_out


def conv_ffn(h, w_up, conv_w, w_down):
    gate, val = jnp.split(h @ w_up, 2, axis=-1)
    return (jax.nn.silu(dwconv_centred(gate, conv_w)) * val) @ w_down


def setup_inputs(seed: int = 0) -> dict:
    key = jax.random.key(seed)
    ks = jax.random.split(key, 18)

    def nrm(k, shape, s):
        return jax.random.normal(k, shape, jnp.float32) * s

    return {
        'x': nrm(ks[0], (BATCH, SEQ, D_MODEL), 1.0),
        'c': nrm(ks[1], (BATCH, D_MODEL), 1.0),
        'ctx': nrm(ks[2], (BATCH, CTX_LEN, D_MODEL), 1.0),
        'c_ctx': nrm(ks[3], (D_MODEL,), 1.0),
        'mod_w': nrm(ks[4], (DEPTH, D_MODEL, 6 * D_MODEL), 0.5 * D_MODEL ** -0.5),
        'mod_b': nrm(ks[5], (DEPTH, 6 * D_MODEL), 0.01),
        'norm_g': 1.0 + nrm(ks[6], (DEPTH, 4, D_MODEL), 0.05),
        'att_w_in': nrm(ks[7], (N_EVEN, D_MODEL, ATT_IN), D_MODEL ** -0.5),
        'att_sink': nrm(ks[8], (N_EVEN, N_Q_HEADS), 0.5),
        'sconv_w': nrm(ks[9], (N_EVEN, SCONV_WIDTH, SCONV_DIM), SCONV_WIDTH ** -0.5),
        'att_w_out': nrm(ks[10], (N_EVEN, MIX_OUT_EVEN, D_MODEL), MIX_OUT_EVEN ** -0.5),
        'mix_w_in': nrm(ks[11], (N_ODD, D_MODEL, MIX_IN_ODD), D_MODEL ** -0.5),
        'pool_w_grp': nrm(ks[12], (N_ODD, N_POOL_GROUPS, POOL_GROUP, POOL_GROUP), POOL_GROUP ** -0.5),
        'pool_scale': 1.0 + nrm(ks[13], (N_ODD, POOL_DIM), 0.1),
        'mix_w_out': nrm(ks[14], (N_ODD, MIX_IN_ODD, D_MODEL), MIX_IN_ODD ** -0.5),
        'ffn_w_up': nrm(ks[15], (DEPTH, D_MODEL, 2 * D_FF), D_MODEL ** -0.5),
        'ffn_conv': nrm(ks[16], (DEPTH, FFN_CONV_WIDTH, D_FF), FFN_CONV_WIDTH ** -0.5),
        'ffn_w_down': nrm(ks[17], (DEPTH, D_FF, D_MODEL), D_FF ** -0.5),
    }


def reference(x, c, ctx, c_ctx, mod_w, mod_b, norm_g, att_w_in, att_sink, sconv_w, att_w_out,
              mix_w_in, pool_w_grp, pool_scale, mix_w_out, ffn_w_up, ffn_conv, ffn_w_down):
    s = x.shape[1]
    rows = s // GRID_W
    row = jnp.repeat(jnp.arange(rows), GRID_W)
    col = jnp.tile(jnp.arange(GRID_W), rows)
    silu_c = jax.nn.silu(c)
    silu_cc = jax.nn.silu(c_ctx)
    hctx = ctx
    for i in range(DEPTH):
        even = i % 2 == 0
        ctx_live = any(j % 2 == 0 for j in range(i + 1, DEPTH))
        gains = norm_g[i]
        mod_x = (silu_c @ mod_w[i] + mod_b[i])[:, None, :]
        sh1, sc1, g1, sh2, sc2, g2 = jnp.split(mod_x, 6, axis=-1)
        hx = rmsnorm(x, gains[0]) * (1.0 + sc1) + sh1
        if even or ctx_live:
            mod_c = silu_cc @ mod_w[i] + mod_b[i]
            csh1, csc1, cg1, csh2, csc2, cg2 = jnp.split(mod_c, 6, axis=-1)
            hc = rmsnorm(hctx, gains[0]) * (1.0 + csc1) + csh1
        if even:
            e = i // 2
            yx, yc = even_mixer(hx, hc, att_w_in[e], att_sink[e], sconv_w[e], att_w_out[e], row, col, ctx_live)
        else:
            o = i // 2
            yx = odd_mixer(hx, mix_w_in[o], pool_w_grp[o], pool_scale[o], mix_w_out[o])
            yc = odd_mixer(hc, mix_w_in[o], pool_w_grp[o], pool_scale[o], mix_w_out[o]) if ctx_live else None
        x = x + g1 * rmsnorm(yx, gains[1])
        hx2 = rmsnorm(x, gains[2]) * (1.0 + sc2) + sh2
        x = x + g2 * rmsnorm(conv_ffn(hx2, ffn_w_up[i], ffn_conv[i], ffn_w_down[i]), gains[3])
        if ctx_live:
            hctx = hctx + cg1 * rmsnorm(yc, gains[1])
            hc2 = rmsnorm(hctx, gains[2]) * (1.0 + csc2) + csh2
            hctx = hctx + cg2 * rmsnorm(conv_ffn(hc2, ffn_w_up[i], ffn_conv[i], ffn_w_down[i]), gains[3])
    return x
```

```python
import functools

import numpy as np
import jax
import jax.numpy as jnp
from jax import lax
from jax.experimental import pallas as pl
from jax.experimental.pallas import tpu as pltpu

F32 = jnp.float32
BF16 = jnp.bfloat16

D_MODEL = 1024
DEPTH = 4
GRID_W = 64
EPS = 1e-6
N_Q_HEADS = 8
N_KV_HEADS = 2
HEAD_DIM = 64
WINDOW = 128
BLOCK = 128
ROPE_BASE = 10000.0
SCONV_DIM = 512
POOL_DIM = 512
N_POOL_GROUPS = 4
POOL_GROUP = POOL_DIM // N_POOL_GROUPS
POOL_WINDOWS = (2, 4, 8, 16)
FFT_DIM = 512
N_FFT_GROUPS = 4
FFT_GROUP = FFT_DIM // N_FFT_GROUPS
D_FF = 2816
Q_DIM = N_Q_HEADS * HEAD_DIM
KV_DIM = N_KV_HEADS * HEAD_DIM

LANES = 128
MASK_VALUE = -1e30
VMEM_LIMIT = 56 * 1024 * 1024
ROW_TILE = 512
FFN_ROWS = 2048
FFN_CHUNK = 256


def _params(*sem):
    return pltpu.CompilerParams(dimension_semantics=sem, vmem_limit_bytes=VMEM_LIMIT)


def _rms(x, g):
    ms = jnp.mean(x * x, axis=-1, keepdims=True)
    return x * lax.rsqrt(ms + EPS) * g


def _dot(a, b):
    return jnp.dot(a, b, preferred_element_type=F32)


def _mod_kernel(c_ref, w_ref, b_ref, o_ref):
    c = c_ref[...]
    s = (c / (1.0 + jnp.exp(-c))).astype(BF16)
    o_ref[0] = _dot(s, w_ref[0].astype(BF16)) + b_ref[0]


def _modulation(cc, mod_w, mod_b):
    depth, d, n = mod_w.shape
    tn = 1024
    return pl.pallas_call(
        _mod_kernel,
        grid=(depth, n // tn),
        in_specs=[
            pl.BlockSpec((16, d), lambda i, j: (0, 0)),
            pl.BlockSpec((1, d, tn), lambda i, j: (i, 0, j)),
            pl.BlockSpec((1, 1, tn), lambda i, j: (i, 0, j)),
        ],
        out_specs=pl.BlockSpec((1, 16, tn), lambda i, j: (i, 0, j)),
        out_shape=jax.ShapeDtypeStruct((depth, 16, n), F32),
        compiler_params=_params("arbitrary", "arbitrary"),
        name="modulation",
    )(cc, mod_w, mod_b.reshape(depth, 1, n))


def _normed(x_ref, gn_ref, sc_ref, sh_ref):
    return (_rms(x_ref[...], gn_ref[...]) * (1.0 + sc_ref[0]) + sh_ref[0]).astype(BF16)


def _rope128(x, cos, sin_a, sin_b):
    return x * cos + pltpu.roll(x, LANES - 16, 1) * sin_a + pltpu.roll(x, 16, 1) * sin_b


def _in_even_kernel(x_ref, gn_ref, sc_ref, sh_ref, w_ref, *rest, rope, full):
    if rope:
        cos_ref, sa_ref, sb_ref = rest[:3]
        rest = rest[3:]
    h = _normed(x_ref, gn_ref, sc_ref, sh_ref)
    if full:
        q_ref, k_ref, v_ref, b_ref, cz_ref = rest
        q = _dot(h, w_ref[:, 0:Q_DIM])
        k = _dot(h, w_ref[:, Q_DIM:Q_DIM + KV_DIM])
        v = _dot(h, w_ref[:, Q_DIM + KV_DIM:Q_DIM + 2 * KV_DIM])
    else:
        k_ref, v_ref = rest
        k = _dot(h, w_ref[:, 0:KV_DIM])
        v = _dot(h, w_ref[:, KV_DIM:2 * KV_DIM])
    if rope:
        cos, sa, sb = cos_ref[...], sa_ref[...], sb_ref[...]
        k = _rope128(k, cos, sa, sb)
    lo = lax.broadcasted_iota(jnp.int32, (1, LANES), 1) < HEAD_DIM
    for ref, t in ((k_ref, k), (v_ref, v)):
        tr = pltpu.roll(t, HEAD_DIM, 1)
        zero = jnp.zeros_like(t)
        ref[...] = jnp.concatenate(
            [jnp.where(lo, t, zero), jnp.where(lo, zero, tr), jnp.where(lo, tr, zero), jnp.where(lo, zero, t)],
            axis=1).astype(BF16)
    if full:
        if rope:
            q = jnp.concatenate(
                [_rope128(q[:, LANES * j:LANES * (j + 1)], cos, sa, sb) for j in range(Q_DIM // LANES)], axis=1)
        q_ref[...] = (q * HEAD_DIM ** -0.5).astype(BF16)
        o = Q_DIM + 2 * KV_DIM
        b_ref[...] = _dot(h, w_ref[:, o:o + SCONV_DIM])
        cz_ref[...] = (_dot(h, w_ref[:, o + SCONV_DIM:o + 2 * SCONV_DIM])
                       * _dot(h, w_ref[:, o + 2 * SCONV_DIM:o + 3 * SCONV_DIM]))


def _in_even(x2, seg, gn, sc, sh, w, tables, full):
    rows, d = x2.shape
    tm = min(ROW_TILE, seg)
    per_seg = seg // tm
    rope = tables is not None
    row = lambda i: (i, 0)
    in_specs = [
        pl.BlockSpec((tm, d), row),
        pl.BlockSpec((1, d), lambda i: (0, 0)),
        pl.BlockSpec((1, 1, d), lambda i: (i // per_seg, 0, 0)),
        pl.BlockSpec((1, 1, d), lambda i: (i // per_seg, 0, 0)),
        pl.BlockSpec(w.shape, lambda i: (0, 0)),
    ]
    args = [x2, gn, sc, sh, w]
    if rope:
        in_specs += [pl.BlockSpec((tm, LANES), lambda i: (i % per_seg, 0))] * 3
        args += list(tables)
    kv_w = 2 * KV_DIM * 2
    out_specs = [pl.BlockSpec((tm, kv_w), row), pl.BlockSpec((tm, kv_w), row)]
    out_shape = [jax.ShapeDtypeStruct((rows, kv_w), BF16)] * 2
    if full:
        out_specs = [pl.BlockSpec((tm, Q_DIM), row)] + out_specs + [pl.BlockSpec((tm, SCONV_DIM), row)] * 2
        out_shape = ([jax.ShapeDtypeStruct((rows, Q_DIM), BF16)] + out_shape
                     + [jax.ShapeDtypeStruct((rows, SCONV_DIM), F32)] * 2)
    return pl.pallas_call(
        functools.partial(_in_even_kernel, rope=rope, full=full),
        grid=(rows // tm,),
        in_specs=in_specs,
        out_specs=out_specs,
        out_shape=out_shape,
        compiler_params=_params("arbitrary"),
        name="in_even",
    )(*args)


def _in_odd_kernel(x_ref, gn_ref, sc_ref, sh_ref, w_ref, cc_ref, cs_ref, p_ref, ab_ref):
    h = _normed(x_ref, gn_ref, sc_ref, sh_ref)
    p_ref[...] = _dot(h, w_ref[:, 0:POOL_DIM])
    uf = _dot(h, w_ref[:, POOL_DIM:POOL_DIM + FFT_DIM]).astype(BF16)
    cc, cs = cc_ref[...], cs_ref[...]
    parts = []
    for m in (cc, cs):
        for g in range(N_FFT_GROUPS):
            parts.append(_dot(uf[:, g * FFT_GROUP:(g + 1) * FFT_GROUP], m))
    ab_ref[...] = jnp.concatenate(parts, axis=1).astype(BF16)


def _in_odd(x2, seg, gn, sc, sh, w, cc, cs):
    rows, d = x2.shape
    tm = min(ROW_TILE, seg)
    per_seg = seg // tm
    row = lambda i: (i, 0)
    return pl.pallas_call(
        _in_odd_kernel,
        grid=(rows // tm,),
        in_specs=[
            pl.BlockSpec((tm, d), row),
            pl.BlockSpec((1, d), lambda i: (0, 0)),
            pl.BlockSpec((1, 1, d), lambda i: (i // per_seg, 0, 0)),
            pl.BlockSpec((1, 1, d), lambda i: (i // per_seg, 0, 0)),
            pl.BlockSpec(w.shape, lambda i: (0, 0)),
            pl.BlockSpec(cc.shape, lambda i: (0, 0)),
            pl.BlockSpec(cs.shape, lambda i: (0, 0)),
        ],
        out_specs=[pl.BlockSpec((tm, POOL_DIM), row), pl.BlockSpec((tm, 2 * FFT_DIM), row)],
        out_shape=[jax.ShapeDtypeStruct((rows, POOL_DIM), F32), jax.ShapeDtypeStruct((rows, 2 * FFT_DIM), BF16)],
        compiler_params=_params("arbitrary"),
        name="in_odd",
    )(x2, gn, sc, sh, w, cc, cs)


def _softmax_sink(s, sink_col):
    m = jnp.maximum(jnp.max(s, axis=-1, keepdims=True), sink_col)
    e = jnp.exp(s - m)
    den = jnp.sum(e, axis=-1, keepdims=True) + jnp.exp(sink_col - m)
    return (e / den).astype(BF16)


def _attn_kernel(sink_ref, q_ref, *rest, window, seq):
    if window:
        kp, kc, kn, vp, vc, vn, kx, vx, o_ref = rest
        k_blocks, v_blocks = (kp, kc, kn, kx), (vp, vc, vn, vx)
    else:
        kx, vx, o_ref = rest
        k_blocks, v_blocks = (kx,), (vx,)
    n = pl.program_id(1)
    nkeys = sum(r.shape[1] for r in k_blocks)
    rows = 2 * BLOCK
    if window:
        r = lax.broadcasted_iota(jnp.int32, (rows, nkeys), 0) % BLOCK
        j = lax.broadcasted_iota(jnp.int32, (rows, nkeys), 1)
        rel = j - BLOCK
        kpos = n * BLOCK + rel
        in_win = (jnp.abs(r - rel) <= WINDOW) & (kpos >= 0) & (kpos < seq)
        valid = in_win | (j >= 3 * BLOCK)
    first_pair = lax.broadcasted_iota(jnp.int32, (rows, 1), 0) < BLOCK
    for h in range(N_KV_HEADS):
        c0 = 2 * LANES * h
        lo_sl, hi_sl = slice(c0, c0 + LANES), slice(c0 + LANES, c0 + 2 * LANES)
        kst = jnp.concatenate([b[0, :, lo_sl] for b in k_blocks] + [b[0, :, hi_sl] for b in k_blocks], axis=0)
        vst = jnp.concatenate([b[0, :, lo_sl] for b in v_blocks] + [b[0, :, hi_sl] for b in v_blocks], axis=0)
        qp = jnp.concatenate([q_ref[0, :, lo_sl], q_ref[0, :, hi_sl]], axis=0)
        s = lax.dot_general(qp, kst, (((1,), (1,)), ((), ())), preferred_element_type=F32)
        probs = []
        for half in range(2):
            sh = s[:, half * nkeys:(half + 1) * nkeys]
            if window:
                sh = jnp.where(valid, sh, MASK_VALUE)
            head = 4 * h + half
            sink_col = jnp.where(first_pair, sink_ref[0, head], sink_ref[0, head + 2])
            probs.append(_softmax_sink(sh, sink_col))
        o = _dot(jnp.concatenate(probs, axis=1), vst)
        o_ref[0, :, lo_sl] = o[:BLOCK].astype(o_ref.dtype)
        o_ref[0, :, hi_sl] = o[BLOCK:].astype(o_ref.dtype)


def _attention(sink, q, k, v, kx, vx, window):
    b, t, _ = q.shape
    l = kx.shape[1]
    nb = t // BLOCK
    w = kx.shape[2]
    cur = lambda i, n: (i, n, 0)
    in_specs = [pl.BlockSpec(memory_space=pltpu.SMEM), pl.BlockSpec((1, BLOCK, Q_DIM), cur)]
    args = [sink, q]
    if window:
        prev = lambda i, n: (i, jnp.maximum(n - 1, 0), 0)
        nxt = lambda i, n: (i, jnp.minimum(n + 1, nb - 1), 0)
        in_specs += [pl.BlockSpec((1, BLOCK, w), m) for m in (prev, cur, nxt)] * 2
        args += [k, k, k, v, v, v]
    in_specs += [pl.BlockSpec((1, l, w), lambda i, n: (i, 0, 0))] * 2
    args += [kx, vx]
    return pl.pallas_call(
        functools.partial(_attn_kernel, window=window, seq=t),
        grid=(b, nb),
        in_specs=in_specs,
        out_specs=pl.BlockSpec((1, BLOCK, Q_DIM), cur),
        out_shape=jax.ShapeDtypeStruct((b, t, Q_DIM), BF16),
        compiler_params=_params("arbitrary", "arbitrary"),
        name="attention",
    )(*args)


def _shifted(x, pos, seg, off):
    rows = x.shape[0]
    sh = pltpu.roll(x, (-off) % rows, 0)
    ok = (pos + off >= 0) & (pos + off < seg)
    return jnp.where(ok, sh, 0.0)


def _sconv_kernel(b_ref, cz_ref, w_ref, o_ref):
    cz = cz_ref[0]
    t = cz.shape[0]
    pos = lax.broadcasted_iota(jnp.int32, (t, 1), 0)
    w = w_ref[...]
    conv = _shifted(cz, pos, t, -1) * w[0:1] + cz * w[1:2] + _shifted(cz, pos, t, 1) * w[2:3]
    o_ref[0] = (b_ref[0] * conv).astype(o_ref.dtype)


def _sconv(b3, cz3, w):
    b, t, c = b3.shape
    blk = pl.BlockSpec((1, t, c), lambda i: (i, 0, 0))
    return pl.pallas_call(
        _sconv_kernel,
        grid=(b,),
        in_specs=[blk, blk, pl.BlockSpec(w.shape, lambda i: (0, 0))],
        out_specs=blk,
        out_shape=jax.ShapeDtypeStruct((b, t, c), BF16),
        compiler_params=_params("arbitrary"),
        name="sconv",
    )(b3, cz3, w)


def _pool_kernel(u_ref, wg_ref, sc_ref, o_ref):
    t = u_ref.shape[1]
    pos = lax.broadcasted_iota(jnp.int32, (t, 1), 0)
    outs = []
    for gi, w in enumerate(POOL_WINDOWS):
        ug = u_ref[0, :, gi * POOL_GROUP:(gi + 1) * POOL_GROUP]
        back, fwd = w // 2, w - w // 2
        total = ug
        for off in range(-back, fwd):
            if off != 0:
                total = total + _shifted(ug, pos, t, off)
        count = (jnp.minimum(pos + fwd, t) - jnp.maximum(pos - back, 0)).astype(F32)
        p = total / count - ug
        outs.append(_dot(p.astype(BF16), wg_ref[gi]))
    o_ref[0] = (jnp.concatenate(outs, axis=1) * sc_ref[...]).astype(o_ref.dtype)


def _pool(u3, wg, scale):
    b, t, c = u3.shape
    blk = pl.BlockSpec((1, t, c), lambda i: (i, 0, 0))
    return pl.pallas_call(
        _pool_kernel,
        grid=(b,),
        in_specs=[blk, pl.BlockSpec(wg.shape, lambda i: (0, 0, 0)), pl.BlockSpec(scale.shape, lambda i: (0, 0))],
        out_specs=blk,
        out_shape=jax.ShapeDtypeStruct((b, t, c), BF16),
        compiler_params=_params("arbitrary"),
        name="pool",
    )(u3, wg, scale)


def _seq_dft_kernel(ct_ref, st_ref, ab_ref, o_ref, *, norm):
    a = ab_ref[0, :, 0:FFT_DIM]
    b = ab_ref[0, :, FFT_DIM:2 * FFT_DIM]
    y = _dot(ct_ref[...], a) - _dot(st_ref[...], b)
    o_ref[0] = (y * norm).astype(o_ref.dtype)


def _seq_dft(ab3, ct, st):
    b, t, _ = ab3.shape
    tk = min(ROW_TILE, t)
    return pl.pallas_call(
        functools.partial(_seq_dft_kernel, norm=float((t * FFT_GROUP) ** -0.5)),
        grid=(b, t // tk),
        in_specs=[
            pl.BlockSpec((tk, t), lambda i, j: (j, 0)),
            pl.BlockSpec((tk, t), lambda i, j: (j, 0)),
            pl.BlockSpec((1, t, 2 * FFT_DIM), lambda i, j: (i, 0, 0)),
        ],
        out_specs=pl.BlockSpec((1, tk, FFT_DIM), lambda i, j: (i, j, 0)),
        out_shape=jax.ShapeDtypeStruct((b, t, FFT_DIM), BF16),
        compiler_params=_params("arbitrary", "arbitrary"),
        name="seq_dft",
    )(ct, st, ab3)


def _out_kernel(a_ref, b_ref, wa_ref, wb_ref, x_ref, gn_ref, gate_ref, o_ref):
    y = _dot(a_ref[...], wa_ref[...]) + _dot(b_ref[...], wb_ref[...])
    o_ref[...] = x_ref[...] + gate_ref[0] * _rms(y, gn_ref[...])


def _out_proj(a2, b2, wa, wb, x2, seg, gn, gate):
    rows, d = x2.shape
    tm = min(ROW_TILE, seg)
    per_seg = seg // tm
    row = lambda i: (i, 0)
    return pl.pallas_call(
        _out_kernel,
        grid=(rows // tm,),
        in_specs=[
            pl.BlockSpec((tm, a2.shape[1]), row),
            pl.BlockSpec((tm, b2.shape[1]), row),
            pl.BlockSpec(wa.shape, lambda i: (0, 0)),
            pl.BlockSpec(wb.shape, lambda i: (0, 0)),
            pl.BlockSpec((tm, d), row),
            pl.BlockSpec((1, d), lambda i: (0, 0)),
            pl.BlockSpec((1, 1, d), lambda i: (i // per_seg, 0, 0)),
        ],
        out_specs=pl.BlockSpec((tm, d), row),
        out_shape=jax.ShapeDtypeStruct((rows, d), F32),
        compiler_params=_params("arbitrary"),
        name="out_proj",
    )(a2, b2, wa, wb, x2, gn, gate)


def _ffn_kernel(x_ref, gn2_ref, sc_ref, sh_ref, gate_ref, gn3_ref, wg_ref, wv_ref, cw_ref, wd_ref,
                o_ref, h_scr, *, seg):
    k = pl.program_id(1)

    @pl.when(k == 0)
    def _():
        h_scr[...] = _normed(x_ref, gn2_ref, sc_ref, sh_ref)
        o_ref[...] = jnp.zeros_like(o_ref)

    h = h_scr[...]
    g = _dot(h, wg_ref[...])
    val = _dot(h, wv_ref[...])
    rows = g.shape[0]
    pos = lax.broadcasted_iota(jnp.int32, (rows, 1), 0) % seg
    cw = cw_ref[...]
    conv = _shifted(g, pos, seg, -1) * cw[0:1] + g * cw[1:2] + _shifted(g, pos, seg, 1) * cw[2:3]
    act = (conv / (1.0 + jnp.exp(-conv)) * val).astype(BF16)
    o_ref[...] += _dot(act, wd_ref[...])

    @pl.when(k == pl.num_programs(1) - 1)
    def _():
        o_ref[...] = x_ref[...] + gate_ref[0] * _rms(o_ref[...], gn3_ref[...])


def _ffn(x2, seg, gn2, sc, sh, gate, gn3, w_up, conv_w, w_down):
    rows, d = x2.shape
    r = FFN_ROWS
    per_tile_mod = sc.shape[0] > 1
    mod = (lambda i, k: (i, 0, 0)) if per_tile_mod else (lambda i, k: (0, 0, 0))
    nch = D_FF // FFN_CHUNK
    vec = pl.BlockSpec((1, d), lambda i, k: (0, 0))
    return pl.pallas_call(
        functools.partial(_ffn_kernel, seg=seg),
        grid=(rows // r, nch),
        in_specs=[
            pl.BlockSpec((r, d), lambda i, k: (i, 0)),
            vec,
            pl.BlockSpec((1, 1, d), mod),
            pl.BlockSpec((1, 1, d), mod),
            pl.BlockSpec((1, 1, d), mod),
            vec,
            pl.BlockSpec((d, FFN_CHUNK), lambda i, k: (0, k)),
            pl.BlockSpec((d, FFN_CHUNK), lambda i, k: (0, k + nch)),
            pl.BlockSpec((3, FFN_CHUNK), lambda i, k: (0, k)),
            pl.BlockSpec((FFN_CHUNK, d), lambda i, k: (k, 0)),
        ],
        out_specs=pl.BlockSpec((r, d), lambda i, k: (i, 0)),
        out_shape=jax.ShapeDtypeStruct((rows, d), F32),
        scratch_shapes=[pltpu.VMEM((r, d), BF16)],
        compiler_params=_params("arbitrary", "arbitrary"),
        name="ffn",
    )(x2, gn2, sc, sh, gate, gn3, w_up, w_up, conv_w, w_down)


def _rope_tables(seq):
    half = HEAD_DIM // 2
    nf = half // 2
    inv = ROPE_BASE ** (-np.arange(nf, dtype=np.float64) / nf)
    t = np.arange(seq)
    lane = np.arange(LANES)
    in_head = lane % HEAD_DIM
    pos = np.where((in_head // half)[None, :] == 0, (t // GRID_W)[:, None], (t % GRID_W)[:, None])
    ang = pos * inv[lane % nf][None, :]
    first = ((lane % half) < nf)[None, :]
    cos = np.cos(ang)
    sin_a = np.where(first, -np.sin(ang), 0.0)
    sin_b = np.where(first, 0.0, np.sin(ang))
    return tuple(jnp.asarray(a, F32) for a in (cos, sin_a, sin_b))


def _dft_tables(n):
    idx = np.arange(n)
    ang = 2.0 * np.pi * ((idx[:, None] * idx[None, :]) % n) / n
    return jnp.asarray(np.cos(ang), F32).astype(BF16), jnp.asarray(np.sin(ang), F32).astype(BF16)


def kernel(x, c, ctx, c_ctx, mod_w, mod_b, norm_g, att_w_in, att_sink, sconv_w, att_w_out, mix_w_in,
           pool_w_grp, pool_scale, mix_w_out, ffn_w_up, ffn_conv, ffn_w_down):
    bsz, seq, d = x.shape
    ctx_len = ctx.shape[1]
    assert d == D_MODEL and seq % FFN_ROWS == 0 and (bsz * ctx_len) % FFN_ROWS == 0 and FFN_ROWS % ctx_len == 0

    cc = jnp.concatenate([c, c_ctx[None, :], jnp.zeros((16 - bsz - 1, d), F32)], axis=0)
    mods = _modulation(cc, mod_w, mod_b)

    rope = _rope_tables(seq)
    chan_c, chan_s = _dft_tables(FFT_GROUP)
    seq_tabs = {seq: _dft_tables(seq), ctx_len: _dft_tables(ctx_len)}

    xs = x.reshape(bsz * seq, d)
    hc = ctx.reshape(bsz * ctx_len, d)

    for i in range(DEPTH):
        even = i % 2 == 0
        ctx_live = any(j % 2 == 0 for j in range(i + 1, DEPTH))
        gains = norm_g[i]
        gn = [gains[j][None, :] for j in range(4)]
        mx = [mods[i, :bsz, j * d:(j + 1) * d].reshape(bsz, 1, d) for j in range(6)]
        mc = [mods[i, bsz:bsz + 1, j * d:(j + 1) * d].reshape(1, 1, d) for j in range(6)]
        mcb = [jnp.broadcast_to(m, (bsz, 1, d)) for m in mc]
        w_up = ffn_w_up[i].astype(BF16)
        w_down = ffn_w_down[i].astype(BF16)
        conv_w = ffn_conv[i]

        if even:
            e = i // 2
            w_in = att_w_in[e].astype(BF16)
            w_out = att_w_out[e].astype(BF16)
            wa, wb = w_out[:Q_DIM], w_out[Q_DIM:]
            sink = att_sink[e][None, :]
            q, k, v, bb, cz = _in_even(xs, seq, gn[0], mx[1], mx[0], w_in, rope, True)
            if ctx_live:
                qc, kc, vc, bc, czc = _in_even(hc, ctx_len, gn[0], mcb[1], mcb[0], w_in, None, True)
            else:
                kc, vc = _in_even(hc, ctx_len, gn[0], mcb[1], mcb[0], w_in[:, Q_DIM:Q_DIM + 2 * KV_DIM], None, False)
            kc3 = kc.reshape(bsz, ctx_len, -1)
            vc3 = vc.reshape(bsz, ctx_len, -1)
            ax = _attention(sink, q.reshape(bsz, seq, -1), k.reshape(bsz, seq, -1), v.reshape(bsz, seq, -1),
                            kc3, vc3, True)
            sx = _sconv(bb.reshape(bsz, seq, -1), cz.reshape(bsz, seq, -1), sconv_w[e])
            xs = _out_proj(ax.reshape(bsz * seq, -1), sx.reshape(bsz * seq, -1), wa, wb, xs, seq, gn[1], mx[2])
            if ctx_live:
                ac = _attention(sink, qc.reshape(bsz, ctx_len, -1), None, None, kc3, vc3, False)
                sc_ = _sconv(bc.reshape(bsz, ctx_len, -1), czc.reshape(bsz, ctx_len, -1), sconv_w[e])
                hc = _out_proj(ac.reshape(bsz * ctx_len, -1), sc_.reshape(bsz * ctx_len, -1), wa, wb, hc, ctx_len,
                               gn[1], mcb[2])
        else:
            o = i // 2
            w_in = mix_w_in[o].astype(BF16)
            w_out = mix_w_out[o].astype(BF16)
            wa, wb = w_out[:POOL_DIM], w_out[POOL_DIM:]
            wg = pool_w_grp[o].astype(BF16)
            ps = pool_scale[o][None, :]
            streams = [(xs, seq, mx, bsz)]
            if ctx_live:
                streams.append((hc, ctx_len, mcb, bsz))
            outs = []
            for h2, t, m, nb in streams:
                pu, ab = _in_odd(h2, t, gn[0], m[1], m[0], w_in, chan_c, chan_s)
                yp = _pool(pu.reshape(nb, t, -1), wg, ps)
                ct, st = seq_tabs[t]
                yf = _seq_dft(ab.reshape(nb, t, -1), ct, st)
                outs.append(_out_proj(yp.reshape(nb * t, -1), yf.reshape(nb * t, -1), wa, wb, h2, t, gn[1], m[2]))
            xs = outs[0]
            if ctx_live:
                hc = outs[1]

        xs = _ffn(xs, seq, gn[2], mx[4], mx[3], mx[5], gn[3], w_up, conv_w, w_down)
        if ctx_live:
            hc = _ffn(hc, ctx_len, gn[2], mc[4], mc[3], mc[5], gn[3], w_up, conv_w, w_down)
    return xs.reshape(bsz, seq, d)
```

```python
import functools

import numpy as np
import jax
import jax.numpy as jnp
from jax import lax
from jax.experimental import pallas as pl
from jax.experimental.pallas import tpu as pltpu

F32 = jnp.float32
BF16 = jnp.bfloat16

D_MODEL = 1024
DEPTH = 4
GRID_W = 64
EPS = 1e-6
N_Q_HEADS = 8
N_KV_HEADS = 2
HEAD_DIM = 64
WINDOW = 128
BLOCK = 128
ROPE_BASE = 10000.0
SCONV_DIM = 512
POOL_DIM = 512
N_POOL_GROUPS = 4
POOL_GROUP = POOL_DIM // N_POOL_GROUPS
POOL_WINDOWS = (2, 4, 8, 16)
FFT_DIM = 512
N_FFT_GROUPS = 4
FFT_GROUP = FFT_DIM // N_FFT_GROUPS
D_FF = 2816
Q_DIM = N_Q_HEADS * HEAD_DIM
KV_DIM = N_KV_HEADS * HEAD_DIM

LANES = 128
MASK_VALUE = -1e30
VMEM_LIMIT = 56 * 1024 * 1024
ROW_TILE = 512
FFN_ROWS = 512
HALO = 8
FFN_CHUNK = 256


def _params(*sem):
    return pltpu.CompilerParams(dimension_semantics=sem, vmem_limit_bytes=VMEM_LIMIT)


def _rms(x, g):
    ms = jnp.mean(x * x, axis=-1, keepdims=True)
    return x * lax.rsqrt(ms + EPS) * g


def _dot(a, b):
    return jnp.dot(a, b, preferred_element_type=F32)


def _mod_kernel(c_ref, w_ref, b_ref, o_ref):
    c = c_ref[...]
    s = (c / (1.0 + jnp.exp(-c))).astype(BF16)
    o_ref[0] = _dot(s, w_ref[0].astype(BF16)) + b_ref[0]


def _modulation(cc, mod_w, mod_b):
    depth, d, n = mod_w.shape
    tn = 1024
    return pl.pallas_call(
        _mod_kernel,
        grid=(depth, n // tn),
        in_specs=[
            pl.BlockSpec((16, d), lambda i, j: (0, 0)),
            pl.BlockSpec((1, d, tn), lambda i, j: (i, 0, j)),
            pl.BlockSpec((1, 1, tn), lambda i, j: (i, 0, j)),
        ],
        out_specs=pl.BlockSpec((1, 16, tn), lambda i, j: (i, 0, j)),
        out_shape=jax.ShapeDtypeStruct((depth, 16, n), F32),
        compiler_params=_params("arbitrary", "arbitrary"),
        name="modulation",
    )(cc, mod_w, mod_b.reshape(depth, 1, n))


def _normed(x_ref, gn_ref, sc_ref, sh_ref):
    return (_rms(x_ref[...], gn_ref[...]) * (1.0 + sc_ref[0]) + sh_ref[0]).astype(BF16)


def _rope128(x, cos, sin_a, sin_b):
    return x * cos + pltpu.roll(x, LANES - 16, 1) * sin_a + pltpu.roll(x, 16, 1) * sin_b


def _in_even_kernel(x_ref, gn_ref, sc_ref, sh_ref, w_ref, *rest, rope, full):
    if rope:
        cos_ref, sa_ref, sb_ref = rest[:3]
        rest = rest[3:]
    h = _normed(x_ref, gn_ref, sc_ref, sh_ref)
    if full:
        q_ref, k_ref, v_ref, b_ref, cz_ref = rest
        q = _dot(h, w_ref[:, 0:Q_DIM])
        k = _dot(h, w_ref[:, Q_DIM:Q_DIM + KV_DIM])
        v = _dot(h, w_ref[:, Q_DIM + KV_DIM:Q_DIM + 2 * KV_DIM])
    else:
        k_ref, v_ref = rest
        k = _dot(h, w_ref[:, 0:KV_DIM])
        v = _dot(h, w_ref[:, KV_DIM:2 * KV_DIM])
    if rope:
        cos, sa, sb = cos_ref[...], sa_ref[...], sb_ref[...]
        k = _rope128(k, cos, sa, sb)
    lo = lax.broadcasted_iota(jnp.int32, (1, LANES), 1) < HEAD_DIM
    for ref, t in ((k_ref, k), (v_ref, v)):
        tr = pltpu.roll(t, HEAD_DIM, 1)
        zero = jnp.zeros_like(t)
        ref[...] = jnp.concatenate(
            [jnp.where(lo, t, zero), jnp.where(lo, zero, tr), jnp.where(lo, tr, zero), jnp.where(lo, zero, t)],
            axis=1).astype(BF16)
    if full:
        if rope:
            q = jnp.concatenate(
                [_rope128(q[:, LANES * j:LANES * (j + 1)], cos, sa, sb) for j in range(Q_DIM // LANES)], axis=1)
        q_ref[...] = (q * HEAD_DIM ** -0.5).astype(BF16)
        o = Q_DIM + 2 * KV_DIM
        b_ref[...] = _dot(h, w_ref[:, o:o + SCONV_DIM])
        cz_ref[...] = (_dot(h, w_ref[:, o + SCONV_DIM:o + 2 * SCONV_DIM])
                       * _dot(h, w_ref[:, o + 2 * SCONV_DIM:o + 3 * SCONV_DIM]))


def _in_even(x2, seg, gn, sc, sh, w, tables, full):
    rows, d = x2.shape
    tm = min(ROW_TILE, seg)
    per_seg = seg // tm
    rope = tables is not None
    row = lambda i: (i, 0)
    in_specs = [
        pl.BlockSpec((tm, d), row),
        pl.BlockSpec((1, d), lambda i: (0, 0)),
        pl.BlockSpec((1, 1, d), lambda i: (i // per_seg, 0, 0)),
        pl.BlockSpec((1, 1, d), lambda i: (i // per_seg, 0, 0)),
        pl.BlockSpec(w.shape, lambda i: (0, 0)),
    ]
    args = [x2, gn, sc, sh, w]
    if rope:
        in_specs += [pl.BlockSpec((tm, LANES), lambda i: (i % per_seg, 0))] * 3
        args += list(tables)
    kv_w = 2 * KV_DIM * 2
    out_specs = [pl.BlockSpec((tm, kv_w), row), pl.BlockSpec((tm, kv_w), row)]
    out_shape = [jax.ShapeDtypeStruct((rows, kv_w), BF16)] * 2
    if full:
        out_specs = [pl.BlockSpec((tm, Q_DIM), row)] + out_specs + [pl.BlockSpec((tm, SCONV_DIM), row)] * 2
        out_shape = ([jax.ShapeDtypeStruct((rows, Q_DIM), BF16)] + out_shape
                     + [jax.ShapeDtypeStruct((rows, SCONV_DIM), F32)] * 2)
    return pl.pallas_call(
        functools.partial(_in_even_kernel, rope=rope, full=full),
        grid=(rows // tm,),
        in_specs=in_specs,
        out_specs=out_specs,
        out_shape=out_shape,
        compiler_params=_params("arbitrary"),
        name="in_even",
    )(*args)


def _in_odd_kernel(x_ref, gn_ref, sc_ref, sh_ref, w_ref, cc_ref, cs_ref, p_ref, ab_ref):
    h = _normed(x_ref, gn_ref, sc_ref, sh_ref)
    p_ref[...] = _dot(h, w_ref[:, 0:POOL_DIM])
    uf = _dot(h, w_ref[:, POOL_DIM:POOL_DIM + FFT_DIM]).astype(BF16)
    cc, cs = cc_ref[...], cs_ref[...]
    parts = []
    for m in (cc, cs):
        for g in range(N_FFT_GROUPS):
            parts.append(_dot(uf[:, g * FFT_GROUP:(g + 1) * FFT_GROUP], m))
    ab_ref[...] = jnp.concatenate(parts, axis=1).astype(BF16)


def _in_odd(x2, seg, gn, sc, sh, w, cc, cs):
    rows, d = x2.shape
    tm = min(ROW_TILE, seg)
    per_seg = seg // tm
    row = lambda i: (i, 0)
    return pl.pallas_call(
        _in_odd_kernel,
        grid=(rows // tm,),
        in_specs=[
            pl.BlockSpec((tm, d), row),
            pl.BlockSpec((1, d), lambda i: (0, 0)),
            pl.BlockSpec((1, 1, d), lambda i: (i // per_seg, 0, 0)),
            pl.BlockSpec((1, 1, d), lambda i: (i // per_seg, 0, 0)),
            pl.BlockSpec(w.shape, lambda i: (0, 0)),
            pl.BlockSpec(cc.shape, lambda i: (0, 0)),
            pl.BlockSpec(cs.shape, lambda i: (0, 0)),
        ],
        out_specs=[pl.BlockSpec((tm, POOL_DIM), row), pl.BlockSpec((tm, 2 * FFT_DIM), row)],
        out_shape=[jax.ShapeDtypeStruct((rows, POOL_DIM), F32), jax.ShapeDtypeStruct((rows, 2 * FFT_DIM), BF16)],
        compiler_params=_params("arbitrary"),
        name="in_odd",
    )(x2, gn, sc, sh, w, cc, cs)


def _softmax_sink(s, sink_col):
    m = jnp.maximum(jnp.max(s, axis=-1, keepdims=True), sink_col)
    e = jnp.exp(s - m)
    den = jnp.sum(e, axis=-1, keepdims=True) + jnp.exp(sink_col - m)
    return (e / den).astype(BF16)


def _attn_kernel(sink_ref, q_ref, *rest, window, seq):
    if window:
        kp, kc, kn, vp, vc, vn, kx, vx, o_ref = rest
        k_blocks, v_blocks = (kp, kc, kn, kx), (vp, vc, vn, vx)
    else:
        kx, vx, o_ref = rest
        k_blocks, v_blocks = (kx,), (vx,)
    n = pl.program_id(1)
    nkeys = sum(r.shape[1] for r in k_blocks)
    rows = 2 * BLOCK
    if window:
        r = lax.broadcasted_iota(jnp.int32, (rows, nkeys), 0) % BLOCK
        j = lax.broadcasted_iota(jnp.int32, (rows, nkeys), 1)
        rel = j - BLOCK
        kpos = n * BLOCK + rel
        in_win = (jnp.abs(r - rel) <= WINDOW) & (kpos >= 0) & (kpos < seq)
        valid = in_win | (j >= 3 * BLOCK)
    first_pair = lax.broadcasted_iota(jnp.int32, (rows, 1), 0) < BLOCK
    for h in range(N_KV_HEADS):
        c0 = 2 * LANES * h
        lo_sl, hi_sl = slice(c0, c0 + LANES), slice(c0 + LANES, c0 + 2 * LANES)
        kst = jnp.concatenate([b[0, :, lo_sl] for b in k_blocks] + [b[0, :, hi_sl] for b in k_blocks], axis=0)
        vst = jnp.concatenate([b[0, :, lo_sl] for b in v_blocks] + [b[0, :, hi_sl] for b in v_blocks], axis=0)
        qp = jnp.concatenate([q_ref[0, :, lo_sl], q_ref[0, :, hi_sl]], axis=0)
        s = lax.dot_general(qp, kst, (((1,), (1,)), ((), ())), preferred_element_type=F32)
        probs = []
        for half in range(2):
            sh = s[:, half * nkeys:(half + 1) * nkeys]
            if window:
                sh = jnp.where(valid, sh, MASK_VALUE)
            head = 4 * h + half
            sink_col = jnp.where(first_pair, sink_ref[0, head], sink_ref[0, head + 2])
            probs.append(_softmax_sink(sh, sink_col))
        o = _dot(jnp.concatenate(probs, axis=1), vst)
        o_ref[0, :, lo_sl] = o[:BLOCK].astype(o_ref.dtype)
        o_ref[0, :, hi_sl] = o[BLOCK:].astype(o_ref.dtype)


def _attention(sink, q, k, v, kx, vx, window):
    b, t, _ = q.shape
    l = kx.shape[1]
    nb = t // BLOCK
    w = kx.shape[2]
    cur = lambda i, n: (i, n, 0)
    in_specs = [pl.BlockSpec(memory_space=pltpu.SMEM), pl.BlockSpec((1, BLOCK, Q_DIM), cur)]
    args = [sink, q]
    if window:
        prev = lambda i, n: (i, jnp.maximum(n - 1, 0), 0)
        nxt = lambda i, n: (i, jnp.minimum(n + 1, nb - 1), 0)
        in_specs += [pl.BlockSpec((1, BLOCK, w), m) for m in (prev, cur, nxt)] * 2
        args += [k, k, k, v, v, v]
    in_specs += [pl.BlockSpec((1, l, w), lambda i, n: (i, 0, 0))] * 2
    args += [kx, vx]
    return pl.pallas_call(
        functools.partial(_attn_kernel, window=window, seq=t),
        grid=(b, nb),
        in_specs=in_specs,
        out_specs=pl.BlockSpec((1, BLOCK, Q_DIM), cur),
        out_shape=jax.ShapeDtypeStruct((b, t, Q_DIM), BF16),
        compiler_params=_params("arbitrary", "arbitrary"),
        name="attention",
    )(*args)


def _shifted(x, pos, seg, off):
    rows = x.shape[0]
    sh = pltpu.roll(x, (-off) % rows, 0)
    ok = (pos + off >= 0) & (pos + off < seg)
    return jnp.where(ok, sh, 0.0)


def _sconv_kernel(b_ref, cz_ref, w_ref, o_ref):
    cz = cz_ref[0]
    t = cz.shape[0]
    pos = lax.broadcasted_iota(jnp.int32, (t, 1), 0)
    w = w_ref[...]
    conv = _shifted(cz, pos, t, -1) * w[0:1] + cz * w[1:2] + _shifted(cz, pos, t, 1) * w[2:3]
    o_ref[0] = (b_ref[0] * conv).astype(o_ref.dtype)


def _sconv(b3, cz3, w):
    b, t, c = b3.shape
    blk = pl.BlockSpec((1, t, c), lambda i: (i, 0, 0))
    return pl.pallas_call(
        _sconv_kernel,
        grid=(b,),
        in_specs=[blk, blk, pl.BlockSpec(w.shape, lambda i: (0, 0))],
        out_specs=blk,
        out_shape=jax.ShapeDtypeStruct((b, t, c), BF16),
        compiler_params=_params("arbitrary"),
        name="sconv",
    )(b3, cz3, w)


def _pool_kernel(u_ref, wg_ref, sc_ref, o_ref):
    t = u_ref.shape[1]
    pos = lax.broadcasted_iota(jnp.int32, (t, 1), 0)
    outs = []
    for gi, w in enumerate(POOL_WINDOWS):
        ug = u_ref[0, :, gi * POOL_GROUP:(gi + 1) * POOL_GROUP]
        back, fwd = w // 2, w - w // 2
        total = ug
        for off in range(-back, fwd):
            if off != 0:
                total = total + _shifted(ug, pos, t, off)
        count = (jnp.minimum(pos + fwd, t) - jnp.maximum(pos - back, 0)).astype(F32)
        p = total / count - ug
        outs.append(_dot(p.astype(BF16), wg_ref[gi]))
    o_ref[0] = (jnp.concatenate(outs, axis=1) * sc_ref[...]).astype(o_ref.dtype)


def _pool(u3, wg, scale):
    b, t, c = u3.shape
    blk = pl.BlockSpec((1, t, c), lambda i: (i, 0, 0))
    return pl.pallas_call(
        _pool_kernel,
        grid=(b,),
        in_specs=[blk, pl.BlockSpec(wg.shape, lambda i: (0, 0, 0)), pl.BlockSpec(scale.shape, lambda i: (0, 0))],
        out_specs=blk,
        out_shape=jax.ShapeDtypeStruct((b, t, c), BF16),
        compiler_params=_params("arbitrary"),
        name="pool",
    )(u3, wg, scale)


def _seq_dft_kernel(ct_ref, st_ref, ab_ref, o_ref, *, norm):
    a = ab_ref[0, :, 0:FFT_DIM]
    b = ab_ref[0, :, FFT_DIM:2 * FFT_DIM]
    y = _dot(ct_ref[...], a) - _dot(st_ref[...], b)
    o_ref[0] = (y * norm).astype(o_ref.dtype)


def _seq_dft(ab3, ct, st):
    b, t, _ = ab3.shape
    tk = min(ROW_TILE, t)
    return pl.pallas_call(
        functools.partial(_seq_dft_kernel, norm=float((t * FFT_GROUP) ** -0.5)),
        grid=(b, t // tk),
        in_specs=[
            pl.BlockSpec((tk, t), lambda i, j: (j, 0)),
            pl.BlockSpec((tk, t), lambda i, j: (j, 0)),
            pl.BlockSpec((1, t, 2 * FFT_DIM), lambda i, j: (i, 0, 0)),
        ],
        out_specs=pl.BlockSpec((1, tk, FFT_DIM), lambda i, j: (i, j, 0)),
        out_shape=jax.ShapeDtypeStruct((b, t, FFT_DIM), BF16),
        compiler_params=_params("arbitrary", "arbitrary"),
        name="seq_dft",
    )(ct, st, ab3)


def _out_kernel(a_ref, b_ref, wa_ref, wb_ref, x_ref, gn_ref, gate_ref, o_ref):
    y = _dot(a_ref[...], wa_ref[...]) + _dot(b_ref[...], wb_ref[...])
    o_ref[...] = x_ref[...] + gate_ref[0] * _rms(y, gn_ref[...])


def _out_proj(a2, b2, wa, wb, x2, seg, gn, gate):
    rows, d = x2.shape
    tm = min(ROW_TILE, seg)
    per_seg = seg // tm
    row = lambda i: (i, 0)
    return pl.pallas_call(
        _out_kernel,
        grid=(rows // tm,),
        in_specs=[
            pl.BlockSpec((tm, a2.shape[1]), row),
            pl.BlockSpec((tm, b2.shape[1]), row),
            pl.BlockSpec(wa.shape, lambda i: (0, 0)),
            pl.BlockSpec(wb.shape, lambda i: (0, 0)),
            pl.BlockSpec((tm, d), row),
            pl.BlockSpec((1, d), lambda i: (0, 0)),
            pl.BlockSpec((1, 1, d), lambda i: (i // per_seg, 0, 0)),
        ],
        out_specs=pl.BlockSpec((tm, d), row),
        out_shape=jax.ShapeDtypeStruct((rows, d), F32),
        compiler_params=_params("arbitrary"),
        name="out_proj",
    )(a2, b2, wa, wb, x2, gn, gate)


def _ffn_kernel(xp_ref, x_ref, xn_ref, gn2_ref, sc_ref, sh_ref, gate_ref, gn3_ref, wu_ref, cw_ref, wd_ref,
                o_ref, act_scr, *, seg):
    tm = x_ref.shape[0]
    x = x_ref[...]
    gn2, scale, shift = gn2_ref[...], 1.0 + sc_ref[0], sh_ref[0]

    def norm(v):
        return (_rms(v, gn2) * scale + shift).astype(BF16)

    h = norm(x)
    hh = norm(jnp.concatenate([xp_ref[...], x, xn_ref[...]], axis=0))
    pos = (pl.program_id(0) * tm + lax.broadcasted_iota(jnp.int32, (tm, 1), 0)) % seg
    first, last = pos == 0, pos == seg - 1
    cw = cw_ref[...]
    ext = tm + 2 * HALO
    for c in range(D_FF // FFN_CHUNK):
        cs = slice(c * FFN_CHUNK, (c + 1) * FFN_CHUNK)
        g = _dot(hh, wu_ref[:, cs])
        prev = jnp.where(first, 0.0, pltpu.roll(g, 1, 0)[HALO:HALO + tm])
        nxt = jnp.where(last, 0.0, pltpu.roll(g, ext - 1, 0)[HALO:HALO + tm])
        conv = prev * cw[0:1, cs] + g[HALO:HALO + tm] * cw[1:2, cs] + nxt * cw[2:3, cs]
        val = _dot(h, wu_ref[:, D_FF + c * FFN_CHUNK:D_FF + (c + 1) * FFN_CHUNK])
        act_scr[:, cs] = (conv / (1.0 + jnp.exp(-conv)) * val).astype(BF16)
    y = _dot(act_scr[...], wd_ref[...])
    o_ref[...] = x + gate_ref[0] * _rms(y, gn3_ref[...])


def _ffn(x2, seg, gn2, sc, sh, gate, gn3, w_up, conv_w, w_down):
    rows, d = x2.shape
    tm = FFN_ROWS
    per_seg = max(seg // tm, 1)
    mod = (lambda i: (i // per_seg, 0, 0)) if sc.shape[0] > 1 else (lambda i: (0, 0, 0))
    hb = tm // HALO
    vec = pl.BlockSpec((1, d), lambda i: (0, 0))
    const = lambda a: pl.BlockSpec(a.shape, lambda i: (0, 0), pipeline_mode=pl.Buffered(1))
    return pl.pallas_call(
        functools.partial(_ffn_kernel, seg=seg),
        grid=(rows // tm,),
        in_specs=[
            pl.BlockSpec((HALO, d), lambda i: (jnp.maximum(i * hb - 1, 0), 0)),
            pl.BlockSpec((tm, d), lambda i: (i, 0)),
            pl.BlockSpec((HALO, d), lambda i: (jnp.minimum((i + 1) * hb, rows // HALO - 1), 0)),
            vec,
            pl.BlockSpec((1, 1, d), mod),
            pl.BlockSpec((1, 1, d), mod),
            pl.BlockSpec((1, 1, d), mod),
            vec,
            const(w_up),
            pl.BlockSpec(conv_w.shape, lambda i: (0, 0)),
            const(w_down),
        ],
        out_specs=pl.BlockSpec((tm, d), lambda i: (i, 0)),
        out_shape=jax.ShapeDtypeStruct((rows, d), F32),
        scratch_shapes=[pltpu.VMEM((tm, D_FF), BF16)],
        compiler_params=_params("arbitrary"),
        name="ffn",
    )(x2, x2, x2, gn2, sc, sh, gate, gn3, w_up, conv_w, w_down)


def _rope_tables(seq):
    half = HEAD_DIM // 2
    nf = half // 2
    inv = ROPE_BASE ** (-np.arange(nf, dtype=np.float64) / nf)
    t = np.arange(seq)
    lane = np.arange(LANES)
    in_head = lane % HEAD_DIM
    pos = np.where((in_head // half)[None, :] == 0, (t // GRID_W)[:, None], (t % GRID_W)[:, None])
    ang = pos * inv[lane % nf][None, :]
    first = ((lane % half) < nf)[None, :]
    cos = np.cos(ang)
    sin_a = np.where(first, -np.sin(ang), 0.0)
    sin_b = np.where(first, 0.0, np.sin(ang))
    return tuple(jnp.asarray(a, F32) for a in (cos, sin_a, sin_b))


def _dft_tables(n):
    idx = np.arange(n)
    ang = 2.0 * np.pi * ((idx[:, None] * idx[None, :]) % n) / n
    return jnp.asarray(np.cos(ang), F32).astype(BF16), jnp.asarray(np.sin(ang), F32).astype(BF16)


def kernel(x, c, ctx, c_ctx, mod_w, mod_b, norm_g, att_w_in, att_sink, sconv_w, att_w_out, mix_w_in,
           pool_w_grp, pool_scale, mix_w_out, ffn_w_up, ffn_conv, ffn_w_down):
    bsz, seq, d = x.shape
    ctx_len = ctx.shape[1]
    assert d == D_MODEL and seq % FFN_ROWS == 0 and (bsz * ctx_len) % FFN_ROWS == 0 and FFN_ROWS % ctx_len == 0

    cc = jnp.concatenate([c, c_ctx[None, :], jnp.zeros((16 - bsz - 1, d), F32)], axis=0)
    mods = _modulation(cc, mod_w, mod_b)

    rope = _rope_tables(seq)
    chan_c, chan_s = _dft_tables(FFT_GROUP)
    seq_tabs = {seq: _dft_tables(seq), ctx_len: _dft_tables(ctx_len)}

    xs = x.reshape(bsz * seq, d)
    hc = ctx.reshape(bsz * ctx_len, d)

    for i in range(DEPTH):
        even = i % 2 == 0
        ctx_live = any(j % 2 == 0 for j in range(i + 1, DEPTH))
        gains = norm_g[i]
        gn = [gains[j][None, :] for j in range(4)]
        mx = [mods[i, :bsz, j * d:(j + 1) * d].reshape(bsz, 1, d) for j in range(6)]
        mc = [mods[i, bsz:bsz + 1, j * d:(j + 1) * d].reshape(1, 1, d) for j in range(6)]
        mcb = [jnp.broadcast_to(m, (bsz, 1, d)) for m in mc]
        w_up = ffn_w_up[i].astype(BF16)
        w_down = ffn_w_down[i].astype(BF16)
        conv_w = ffn_conv[i]

        if even:
            e = i // 2
            w_in = att_w_in[e].astype(BF16)
            w_out = att_w_out[e].astype(BF16)
            wa, wb = w_out[:Q_DIM], w_out[Q_DIM:]
            sink = att_sink[e][None, :]
            q, k, v, bb, cz = _in_even(xs, seq, gn[0], mx[1], mx[0], w_in, rope, True)
            if ctx_live:
                qc, kc, vc, bc, czc = _in_even(hc, ctx_len, gn[0], mcb[1], mcb[0], w_in, None, True)
            else:
                kc, vc = _in_even(hc, ctx_len, gn[0], mcb[1], mcb[0], w_in[:, Q_DIM:Q_DIM + 2 * KV_DIM], None, False)
            kc3 = kc.reshape(bsz, ctx_len, -1)
            vc3 = vc.reshape(bsz, ctx_len, -1)
            ax = _attention(sink, q.reshape(bsz, seq, -1), k.reshape(bsz, seq, -1), v.reshape(bsz, seq, -1),
                            kc3, vc3, True)
            sx = _sconv(bb.reshape(bsz, seq, -1), cz.reshape(bsz, seq, -1), sconv_w[e])
            xs = _out_proj(ax.reshape(bsz * seq, -1), sx.reshape(bsz * seq, -1), wa, wb, xs, seq, gn[1], mx[2])
            if ctx_live:
                ac = _attention(sink, qc.reshape(bsz, ctx_len, -1), None, None, kc3, vc3, False)
                sc_ = _sconv(bc.reshape(bsz, ctx_len, -1), czc.reshape(bsz, ctx_len, -1), sconv_w[e])
                hc = _out_proj(ac.reshape(bsz * ctx_len, -1), sc_.reshape(bsz * ctx_len, -1), wa, wb, hc, ctx_len,
                               gn[1], mcb[2])
        else:
            o = i // 2
            w_in = mix_w_in[o].astype(BF16)
            w_out = mix_w_out[o].astype(BF16)
            wa, wb = w_out[:POOL_DIM], w_out[POOL_DIM:]
            wg = pool_w_grp[o].astype(BF16)
            ps = pool_scale[o][None, :]
            streams = [(xs, seq, mx, bsz)]
            if ctx_live:
                streams.append((hc, ctx_len, mcb, bsz))
            outs = []
            for h2, t, m, nb in streams:
                pu, ab = _in_odd(h2, t, gn[0], m[1], m[0], w_in, chan_c, chan_s)
                yp = _pool(pu.reshape(nb, t, -1), wg, ps)
                ct, st = seq_tabs[t]
                yf = _seq_dft(ab.reshape(nb, t, -1), ct, st)
                outs.append(_out_proj(yp.reshape(nb * t, -1), yf.reshape(nb * t, -1), wa, wb, h2, t, gn[1], m[2]))
            xs = outs[0]
            if ctx_live:
                hc = outs[1]

        xs = _ffn(xs, seq, gn[2], mx[4], mx[3], mx[5], gn[3], w_up, conv_w, w_down)
        if ctx_live:
            hc = _ffn(hc, ctx_len, gn[2], mc[4], mc[3], mc[5], gn[3], w_up, conv_w, w_down)
    return xs.reshape(bsz, seq, d)
```

```python
import functools

import numpy as np
import jax
import jax.numpy as jnp
from jax import lax
from jax.experimental import pallas as pl
from jax.experimental.pallas import tpu as pltpu

F32 = jnp.float32
BF16 = jnp.bfloat16

D_MODEL = 1024
DEPTH = 4
GRID_W = 64
EPS = 1e-6
N_Q_HEADS = 8
N_KV_HEADS = 2
HEAD_DIM = 64
WINDOW = 128
BLOCK = 128
ROPE_BASE = 10000.0
SCONV_DIM = 512
POOL_DIM = 512
N_POOL_GROUPS = 4
POOL_GROUP = POOL_DIM // N_POOL_GROUPS
POOL_WINDOWS = (2, 4, 8, 16)
FFT_DIM = 512
N_FFT_GROUPS = 4
FFT_GROUP = FFT_DIM // N_FFT_GROUPS
D_FF = 2816
Q_DIM = N_Q_HEADS * HEAD_DIM
KV_DIM = N_KV_HEADS * HEAD_DIM

LANES = 128
MASK_VALUE = -1e30
LOG2E = 1.4426950408889634
ATTN_QBLOCKS = 4
VMEM_LIMIT = 56 * 1024 * 1024
ROW_TILE = 512
FFN_ROWS = 512
HALO = 8
FFN_CHUNK = 256


def _params(*sem):
    return pltpu.CompilerParams(dimension_semantics=sem, vmem_limit_bytes=VMEM_LIMIT)


def _rms(x, g):
    ms = jnp.mean(x * x, axis=-1, keepdims=True)
    return x * lax.rsqrt(ms + EPS) * g


def _dot(a, b):
    return jnp.dot(a, b, preferred_element_type=F32)


def _mod_kernel(c_ref, w_ref, b_ref, o_ref):
    c = c_ref[...]
    s = (c / (1.0 + jnp.exp(-c))).astype(BF16)
    o_ref[0] = _dot(s, w_ref[0].astype(BF16)) + b_ref[0]


def _modulation(cc, mod_w, mod_b):
    depth, d, n = mod_w.shape
    tn = 1024
    return pl.pallas_call(
        _mod_kernel,
        grid=(depth, n // tn),
        in_specs=[
            pl.BlockSpec((16, d), lambda i, j: (0, 0)),
            pl.BlockSpec((1, d, tn), lambda i, j: (i, 0, j)),
            pl.BlockSpec((1, 1, tn), lambda i, j: (i, 0, j)),
        ],
        out_specs=pl.BlockSpec((1, 16, tn), lambda i, j: (i, 0, j)),
        out_shape=jax.ShapeDtypeStruct((depth, 16, n), F32),
        compiler_params=_params("arbitrary", "arbitrary"),
        name="modulation",
    )(cc, mod_w, mod_b.reshape(depth, 1, n))


def _normed(x_ref, gn_ref, sc_ref, sh_ref):
    return (_rms(x_ref[...], gn_ref[...]) * (1.0 + sc_ref[0]) + sh_ref[0]).astype(BF16)


def _rope128(x, cos, sin_a, sin_b):
    return x * cos + pltpu.roll(x, LANES - 16, 1) * sin_a + pltpu.roll(x, 16, 1) * sin_b


def _in_even_kernel(x_ref, gn_ref, sc_ref, sh_ref, w_ref, *rest, rope, full):
    if rope:
        cos_ref, sa_ref, sb_ref = rest[:3]
        rest = rest[3:]
    h = _normed(x_ref, gn_ref, sc_ref, sh_ref)
    if full:
        q_ref, k_ref, v_ref, b_ref, cz_ref = rest
        q = _dot(h, w_ref[:, 0:Q_DIM])
        k = _dot(h, w_ref[:, Q_DIM:Q_DIM + KV_DIM])
        v = _dot(h, w_ref[:, Q_DIM + KV_DIM:Q_DIM + 2 * KV_DIM])
    else:
        k_ref, v_ref = rest
        k = _dot(h, w_ref[:, 0:KV_DIM])
        v = _dot(h, w_ref[:, KV_DIM:2 * KV_DIM])
    if rope:
        cos, sa, sb = cos_ref[...], sa_ref[...], sb_ref[...]
        k = _rope128(k, cos, sa, sb)
    lo = lax.broadcasted_iota(jnp.int32, (1, LANES), 1) < HEAD_DIM
    vr = pltpu.roll(v, HEAD_DIM, 1)
    zero = jnp.zeros_like(v)
    v_ref[...] = jnp.concatenate(
        [jnp.where(lo, v, zero), jnp.where(lo, zero, vr), jnp.where(lo, vr, zero), jnp.where(lo, zero, v)],
        axis=1).astype(BF16)
    kt = k.T
    top = lax.broadcasted_iota(jnp.int32, (LANES, 1), 0) < HEAD_DIM
    ktr = pltpu.roll(kt, HEAD_DIM, 0)
    zero = jnp.zeros_like(kt)
    k_ref[...] = jnp.concatenate(
        [jnp.where(top, kt, zero), jnp.where(top, zero, ktr), jnp.where(top, ktr, zero), jnp.where(top, zero, kt)],
        axis=0).astype(BF16)
    if full:
        if rope:
            q = jnp.concatenate(
                [_rope128(q[:, LANES * j:LANES * (j + 1)], cos, sa, sb) for j in range(Q_DIM // LANES)], axis=1)
        q_ref[...] = (q * (HEAD_DIM ** -0.5 * LOG2E)).astype(BF16)
        o = Q_DIM + 2 * KV_DIM
        b_ref[...] = _dot(h, w_ref[:, o:o + SCONV_DIM])
        cz_ref[...] = (_dot(h, w_ref[:, o + SCONV_DIM:o + 2 * SCONV_DIM])
                       * _dot(h, w_ref[:, o + 2 * SCONV_DIM:o + 3 * SCONV_DIM]))


def _in_even(x2, seg, gn, sc, sh, w, tables, full):
    rows, d = x2.shape
    tm = min(ROW_TILE, seg)
    per_seg = seg // tm
    rope = tables is not None
    row = lambda i: (i, 0)
    in_specs = [
        pl.BlockSpec((tm, d), row),
        pl.BlockSpec((1, d), lambda i: (0, 0)),
        pl.BlockSpec((1, 1, d), lambda i: (i // per_seg, 0, 0)),
        pl.BlockSpec((1, 1, d), lambda i: (i // per_seg, 0, 0)),
        pl.BlockSpec(w.shape, lambda i: (0, 0)),
    ]
    args = [x2, gn, sc, sh, w]
    if rope:
        in_specs += [pl.BlockSpec((tm, LANES), lambda i: (i % per_seg, 0))] * 3
        args += list(tables)
    kv_w = 2 * KV_DIM * 2
    out_specs = [pl.BlockSpec((kv_w, tm), lambda i: (0, i)), pl.BlockSpec((tm, kv_w), row)]
    out_shape = [jax.ShapeDtypeStruct((kv_w, rows), BF16), jax.ShapeDtypeStruct((rows, kv_w), BF16)]
    if full:
        out_specs = [pl.BlockSpec((tm, Q_DIM), row)] + out_specs + [pl.BlockSpec((tm, SCONV_DIM), row)] * 2
        out_shape = ([jax.ShapeDtypeStruct((rows, Q_DIM), BF16)] + out_shape
                     + [jax.ShapeDtypeStruct((rows, SCONV_DIM), F32)] * 2)
    return pl.pallas_call(
        functools.partial(_in_even_kernel, rope=rope, full=full),
        grid=(rows // tm,),
        in_specs=in_specs,
        out_specs=out_specs,
        out_shape=out_shape,
        compiler_params=_params("arbitrary"),
        name="in_even",
    )(*args)


def _in_odd_kernel(x_ref, gn_ref, sc_ref, sh_ref, w_ref, cc_ref, cs_ref, p_ref, ab_ref):
    h = _normed(x_ref, gn_ref, sc_ref, sh_ref)
    p_ref[...] = _dot(h, w_ref[:, 0:POOL_DIM])
    uf = _dot(h, w_ref[:, POOL_DIM:POOL_DIM + FFT_DIM]).astype(BF16)
    cc, cs = cc_ref[...], cs_ref[...]
    parts = []
    for m in (cc, cs):
        for g in range(N_FFT_GROUPS):
            parts.append(_dot(uf[:, g * FFT_GROUP:(g + 1) * FFT_GROUP], m))
    ab_ref[...] = jnp.concatenate(parts, axis=1).astype(BF16)


def _in_odd(x2, seg, gn, sc, sh, w, cc, cs):
    rows, d = x2.shape
    tm = min(ROW_TILE, seg)
    per_seg = seg // tm
    row = lambda i: (i, 0)
    return pl.pallas_call(
        _in_odd_kernel,
        grid=(rows // tm,),
        in_specs=[
            pl.BlockSpec((tm, d), row),
            pl.BlockSpec((1, d), lambda i: (0, 0)),
            pl.BlockSpec((1, 1, d), lambda i: (i // per_seg, 0, 0)),
            pl.BlockSpec((1, 1, d), lambda i: (i // per_seg, 0, 0)),
            pl.BlockSpec(w.shape, lambda i: (0, 0)),
            pl.BlockSpec(cc.shape, lambda i: (0, 0)),
            pl.BlockSpec(cs.shape, lambda i: (0, 0)),
        ],
        out_specs=[pl.BlockSpec((tm, POOL_DIM), row), pl.BlockSpec((tm, 2 * FFT_DIM), row)],
        out_shape=[jax.ShapeDtypeStruct((rows, POOL_DIM), F32), jax.ShapeDtypeStruct((rows, 2 * FFT_DIM), BF16)],
        compiler_params=_params("arbitrary"),
        name="in_odd",
    )(x2, gn, sc, sh, w, cc, cs)


def _attn_kernel(sink_ref, q_ref, *rest, window, qb, nblocks):
    if window:
        ktp, ktc, ktn, vp, vc, vn, kxt, vx, o_ref = rest
    else:
        kxt, vx, o_ref = rest
    n = pl.program_id(1)
    rows = 2 * BLOCK
    first_pair = lax.broadcasted_iota(jnp.int32, (rows, 1), 0) < BLOCK
    lo = lax.broadcasted_iota(jnp.int32, (1, LANES), 1) < HEAD_DIM
    if window:
        qi = lax.broadcasted_iota(jnp.int32, (rows, BLOCK), 0) % BLOCK
        j = lax.broadcasted_iota(jnp.int32, (rows, BLOCK), 1)
        tri_prev = jnp.where(j >= qi, 0.0, MASK_VALUE)
        tri_next = jnp.where(j <= qi, 0.0, MASK_VALUE)
        bias = []
        for t in range(qb):
            blk = n * qb + t
            bias.append((tri_prev + jnp.where(blk >= 1, 0.0, MASK_VALUE),
                         tri_next + jnp.where(blk <= nblocks - 2, 0.0, MASK_VALUE)))
    for h in range(N_KV_HEADS):
        c0 = 2 * LANES * h
        halves = (slice(c0, c0 + LANES), slice(c0 + LANES, c0 + 2 * LANES))
        if window:
            kall = [jnp.concatenate([ktp[sl, :], ktc[sl, :], ktn[sl, :]], axis=1) for sl in halves]
            vall = [jnp.concatenate([vp[0, :, sl], vc[0, :, sl], vn[0, :, sl]], axis=0) for sl in halves]
        for t in range(qb):
            rs = slice(t * BLOCK, (t + 1) * BLOCK)
            kparts, vparts = [], []
            for e, sl in enumerate(halves):
                if window:
                    kparts.append(kall[e][:, t * BLOCK:(t + 3) * BLOCK])
                    vparts.append(vall[e][t * BLOCK:(t + 3) * BLOCK])
                kparts.append(kxt[sl, :])
                vparts.append(vx[0, :, sl])
            kst = jnp.concatenate(kparts, axis=1)
            vst = jnp.concatenate(vparts, axis=0)
            nkeys = vst.shape[0] // 2
            qp = jnp.concatenate([q_ref[0, rs, halves[0]], q_ref[0, rs, halves[1]]], axis=0)
            s = _dot(qp, kst)
            probs, inv = [], []
            for half in range(2):
                tiles = [s[:, half * nkeys + c * LANES:half * nkeys + (c + 1) * LANES] for c in range(nkeys // LANES)]
                if window:
                    tiles[0] = tiles[0] + bias[t][0]
                    tiles[2] = tiles[2] + bias[t][1]
                head = 4 * h + half
                sink_col = jnp.where(first_pair, sink_ref[0, head], sink_ref[0, head + 2]) * LOG2E
                m = jnp.maximum(jnp.max(functools.reduce(jnp.maximum, tiles), axis=-1, keepdims=True), sink_col)
                es = [jnp.exp2(tl - m) for tl in tiles]
                den = jnp.sum(functools.reduce(jnp.add, es), axis=-1, keepdims=True) + jnp.exp2(sink_col - m)
                probs += [e.astype(BF16) for e in es]
                inv.append(1.0 / den)
            o = _dot(jnp.concatenate(probs, axis=1), vst) * jnp.where(lo, inv[0], inv[1])
            o_ref[0, rs, halves[0]] = o[:BLOCK].astype(o_ref.dtype)
            o_ref[0, rs, halves[1]] = o[BLOCK:].astype(o_ref.dtype)


def _attention(sink, q, kt, v, kxt, vx, window):
    b, t, _ = q.shape
    l = vx.shape[1]
    nb = t // BLOCK
    qb = ATTN_QBLOCKS if window else nb
    w = vx.shape[2]
    steps = nb // qb
    cur = lambda i, n: (i, n, 0)
    in_specs = [pl.BlockSpec(memory_space=pltpu.SMEM), pl.BlockSpec((1, qb * BLOCK, Q_DIM), cur)]
    args = [sink, q]
    if window:
        pb = lambda n: jnp.maximum(n * qb - 1, 0)
        nx = lambda n: jnp.minimum(n * qb + qb, nb - 1)
        in_specs += [
            pl.BlockSpec((w, BLOCK), lambda i, n: (0, i * nb + pb(n))),
            pl.BlockSpec((w, qb * BLOCK), lambda i, n: (0, i * steps + n)),
            pl.BlockSpec((w, BLOCK), lambda i, n: (0, i * nb + nx(n))),
            pl.BlockSpec((1, BLOCK, w), lambda i, n: (i, pb(n), 0)),
            pl.BlockSpec((1, qb * BLOCK, w), cur),
            pl.BlockSpec((1, BLOCK, w), lambda i, n: (i, nx(n), 0)),
        ]
        args += [kt, kt, kt, v, v, v]
    in_specs += [pl.BlockSpec((w, l), lambda i, n: (0, i)), pl.BlockSpec((1, l, w), lambda i, n: (i, 0, 0))]
    args += [kxt, vx]
    return pl.pallas_call(
        functools.partial(_attn_kernel, window=window, qb=qb, nblocks=nb),
        grid=(b, steps),
        in_specs=in_specs,
        out_specs=pl.BlockSpec((1, qb * BLOCK, Q_DIM), cur),
        out_shape=jax.ShapeDtypeStruct((b, t, Q_DIM), BF16),
        compiler_params=_params("arbitrary", "arbitrary"),
        name="attention",
    )(*args)


def _shifted(x, pos, seg, off):
    rows = x.shape[0]
    sh = pltpu.roll(x, (-off) % rows, 0)
    ok = (pos + off >= 0) & (pos + off < seg)
    return jnp.where(ok, sh, 0.0)


def _sconv_kernel(b_ref, cz_ref, w_ref, o_ref):
    cz = cz_ref[0]
    t = cz.shape[0]
    pos = lax.broadcasted_iota(jnp.int32, (t, 1), 0)
    w = w_ref[...]
    conv = _shifted(cz, pos, t, -1) * w[0:1] + cz * w[1:2] + _shifted(cz, pos, t, 1) * w[2:3]
    o_ref[0] = (b_ref[0] * conv).astype(o_ref.dtype)


def _sconv(b3, cz3, w):
    b, t, c = b3.shape
    blk = pl.BlockSpec((1, t, c), lambda i: (i, 0, 0))
    return pl.pallas_call(
        _sconv_kernel,
        grid=(b,),
        in_specs=[blk, blk, pl.BlockSpec(w.shape, lambda i: (0, 0))],
        out_specs=blk,
        out_shape=jax.ShapeDtypeStruct((b, t, c), BF16),
        compiler_params=_params("arbitrary"),
        name="sconv",
    )(b3, cz3, w)


def _pool_kernel(u_ref, wg_ref, sc_ref, o_ref):
    t = u_ref.shape[1]
    pos = lax.broadcasted_iota(jnp.int32, (t, 1), 0)
    outs = []
    for gi, w in enumerate(POOL_WINDOWS):
        ug = u_ref[0, :, gi * POOL_GROUP:(gi + 1) * POOL_GROUP]
        back, fwd = w // 2, w - w // 2
        total = ug
        for off in range(-back, fwd):
            if off != 0:
                total = total + _shifted(ug, pos, t, off)
        count = (jnp.minimum(pos + fwd, t) - jnp.maximum(pos - back, 0)).astype(F32)
        p = total / count - ug
        outs.append(_dot(p.astype(BF16), wg_ref[gi]))
    o_ref[0] = (jnp.concatenate(outs, axis=1) * sc_ref[...]).astype(o_ref.dtype)


def _pool(u3, wg, scale):
    b, t, c = u3.shape
    blk = pl.BlockSpec((1, t, c), lambda i: (i, 0, 0))
    return pl.pallas_call(
        _pool_kernel,
        grid=(b,),
        in_specs=[blk, pl.BlockSpec(wg.shape, lambda i: (0, 0, 0)), pl.BlockSpec(scale.shape, lambda i: (0, 0))],
        out_specs=blk,
        out_shape=jax.ShapeDtypeStruct((b, t, c), BF16),
        compiler_params=_params("arbitrary"),
        name="pool",
    )(u3, wg, scale)


def _seq_dft_kernel(ct_ref, st_ref, ab_ref, o_ref, *, norm):
    a = ab_ref[0, :, 0:FFT_DIM]
    b = ab_ref[0, :, FFT_DIM:2 * FFT_DIM]
    y = _dot(ct_ref[...], a) - _dot(st_ref[...], b)
    o_ref[0] = (y * norm).astype(o_ref.dtype)


def _seq_dft(ab3, ct, st):
    b, t, _ = ab3.shape
    tk = min(ROW_TILE, t)
    return pl.pallas_call(
        functools.partial(_seq_dft_kernel, norm=float((t * FFT_GROUP) ** -0.5)),
        grid=(b, t // tk),
        in_specs=[
            pl.BlockSpec((tk, t), lambda i, j: (j, 0)),
            pl.BlockSpec((tk, t), lambda i, j: (j, 0)),
            pl.BlockSpec((1, t, 2 * FFT_DIM), lambda i, j: (i, 0, 0)),
        ],
        out_specs=pl.BlockSpec((1, tk, FFT_DIM), lambda i, j: (i, j, 0)),
        out_shape=jax.ShapeDtypeStruct((b, t, FFT_DIM), BF16),
        compiler_params=_params("arbitrary", "arbitrary"),
        name="seq_dft",
    )(ct, st, ab3)


def _out_kernel(a_ref, b_ref, wa_ref, wb_ref, x_ref, gn_ref, gate_ref, o_ref):
    y = _dot(a_ref[...], wa_ref[...]) + _dot(b_ref[...], wb_ref[...])
    o_ref[...] = x_ref[...] + gate_ref[0] * _rms(y, gn_ref[...])


def _out_proj(a2, b2, wa, wb, x2, seg, gn, gate):
    rows, d = x2.shape
    tm = min(ROW_TILE, seg)
    per_seg = seg // tm
    row = lambda i: (i, 0)
    return pl.pallas_call(
        _out_kernel,
        grid=(rows // tm,),
        in_specs=[
            pl.BlockSpec((tm, a2.shape[1]), row),
            pl.BlockSpec((tm, b2.shape[1]), row),
            pl.BlockSpec(wa.shape, lambda i: (0, 0)),
            pl.BlockSpec(wb.shape, lambda i: (0, 0)),
            pl.BlockSpec((tm, d), row),
            pl.BlockSpec((1, d), lambda i: (0, 0)),
            pl.BlockSpec((1, 1, d), lambda i: (i // per_seg, 0, 0)),
        ],
        out_specs=pl.BlockSpec((tm, d), row),
        out_shape=jax.ShapeDtypeStruct((rows, d), F32),
        compiler_params=_params("arbitrary"),
        name="out_proj",
    )(a2, b2, wa, wb, x2, gn, gate)


def _ffn_kernel(xp_ref, x_ref, xn_ref, gn2_ref, sc_ref, sh_ref, gate_ref, gn3_ref, wu_ref, cw_ref, wd_ref,
                o_ref, act_scr, *, seg):
    tm = x_ref.shape[0]
    x = x_ref[...]
    gn2, scale, shift = gn2_ref[...], 1.0 + sc_ref[0], sh_ref[0]

    def norm(v):
        return (_rms(v, gn2) * scale + shift).astype(BF16)

    h = norm(x)
    hh = norm(jnp.concatenate([xp_ref[...], x, xn_ref[...]], axis=0))
    pos = (pl.program_id(0) * tm + lax.broadcasted_iota(jnp.int32, (tm, 1), 0)) % seg
    first, last = pos == 0, pos == seg - 1
    cw = cw_ref[...]
    ext = tm + 2 * HALO
    for c in range(D_FF // FFN_CHUNK):
        cs = slice(c * FFN_CHUNK, (c + 1) * FFN_CHUNK)
        g = _dot(hh, wu_ref[:, cs])
        prev = jnp.where(first, 0.0, pltpu.roll(g, 1, 0)[HALO:HALO + tm])
        nxt = jnp.where(last, 0.0, pltpu.roll(g, ext - 1, 0)[HALO:HALO + tm])
        conv = prev * cw[0:1, cs] + g[HALO:HALO + tm] * cw[1:2, cs] + nxt * cw[2:3, cs]
        val = _dot(h, wu_ref[:, D_FF + c * FFN_CHUNK:D_FF + (c + 1) * FFN_CHUNK])
        act_scr[:, cs] = (conv / (1.0 + jnp.exp(-conv)) * val).astype(BF16)
    y = _dot(act_scr[...], wd_ref[...])
    o_ref[...] = x + gate_ref[0] * _rms(y, gn3_ref[...])


def _ffn(x2, seg, gn2, sc, sh, gate, gn3, w_up, conv_w, w_down):
    rows, d = x2.shape
    tm = FFN_ROWS
    per_seg = max(seg // tm, 1)
    mod = (lambda i: (i // per_seg, 0, 0)) if sc.shape[0] > 1 else (lambda i: (0, 0, 0))
    hb = tm // HALO
    vec = pl.BlockSpec((1, d), lambda i: (0, 0))
    const = lambda a: pl.BlockSpec(a.shape, lambda i: (0, 0), pipeline_mode=pl.Buffered(1))
    return pl.pallas_call(
        functools.partial(_ffn_kernel, seg=seg),
        grid=(rows // tm,),
        in_specs=[
            pl.BlockSpec((HALO, d), lambda i: (jnp.maximum(i * hb - 1, 0), 0)),
            pl.BlockSpec((tm, d), lambda i: (i, 0)),
            pl.BlockSpec((HALO, d), lambda i: (jnp.minimum((i + 1) * hb, rows // HALO - 1), 0)),
            vec,
            pl.BlockSpec((1, 1, d), mod),
            pl.BlockSpec((1, 1, d), mod),
            pl.BlockSpec((1, 1, d), mod),
            vec,
            const(w_up),
            pl.BlockSpec(conv_w.shape, lambda i: (0, 0)),
            const(w_down),
        ],
        out_specs=pl.BlockSpec((tm, d), lambda i: (i, 0)),
        out_shape=jax.ShapeDtypeStruct((rows, d), F32),
        scratch_shapes=[pltpu.VMEM((tm, D_FF), BF16)],
        compiler_params=_params("arbitrary"),
        name="ffn",
    )(x2, x2, x2, gn2, sc, sh, gate, gn3, w_up, conv_w, w_down)


def _rope_tables(seq):
    half = HEAD_DIM // 2
    nf = half // 2
    inv = ROPE_BASE ** (-np.arange(nf, dtype=np.float64) / nf)
    t = np.arange(seq)
    lane = np.arange(LANES)
    in_head = lane % HEAD_DIM
    pos = np.where((in_head // half)[None, :] == 0, (t // GRID_W)[:, None], (t % GRID_W)[:, None])
    ang = pos * inv[lane % nf][None, :]
    first = ((lane % half) < nf)[None, :]
    cos = np.cos(ang)
    sin_a = np.where(first, -np.sin(ang), 0.0)
    sin_b = np.where(first, 0.0, np.sin(ang))
    return tuple(jnp.asarray(a, F32) for a in (cos, sin_a, sin_b))


def _dft_tables(n):
    idx = np.arange(n)
    ang = 2.0 * np.pi * ((idx[:, None] * idx[None, :]) % n) / n
    return jnp.asarray(np.cos(ang), F32).astype(BF16), jnp.asarray(np.sin(ang), F32).astype(BF16)


def kernel(x, c, ctx, c_ctx, mod_w, mod_b, norm_g, att_w_in, att_sink, sconv_w, att_w_out, mix_w_in,
           pool_w_grp, pool_scale, mix_w_out, ffn_w_up, ffn_conv, ffn_w_down):
    bsz, seq, d = x.shape
    ctx_len = ctx.shape[1]
    assert d == D_MODEL and seq % FFN_ROWS == 0 and (bsz * ctx_len) % FFN_ROWS == 0 and FFN_ROWS % ctx_len == 0

    cc = jnp.concatenate([c, c_ctx[None, :], jnp.zeros((16 - bsz - 1, d), F32)], axis=0)
    mods = _modulation(cc, mod_w, mod_b)

    rope = _rope_tables(seq)
    chan_c, chan_s = _dft_tables(FFT_GROUP)
    seq_tabs = {seq: _dft_tables(seq), ctx_len: _dft_tables(ctx_len)}

    xs = x.reshape(bsz * seq, d)
    hc = ctx.reshape(bsz * ctx_len, d)

    for i in range(DEPTH):
        even = i % 2 == 0
        ctx_live = any(j % 2 == 0 for j in range(i + 1, DEPTH))
        gains = norm_g[i]
        gn = [gains[j][None, :] for j in range(4)]
        mx = [mods[i, :bsz, j * d:(j + 1) * d].reshape(bsz, 1, d) for j in range(6)]
        mc = [mods[i, bsz:bsz + 1, j * d:(j + 1) * d].reshape(1, 1, d) for j in range(6)]
        mcb = [jnp.broadcast_to(m, (bsz, 1, d)) for m in mc]
        w_up = ffn_w_up[i].astype(BF16)
        w_down = ffn_w_down[i].astype(BF16)
        conv_w = ffn_conv[i]

        if even:
            e = i // 2
            w_in = att_w_in[e].astype(BF16)
            w_out = att_w_out[e].astype(BF16)
            wa, wb = w_out[:Q_DIM], w_out[Q_DIM:]
            sink = att_sink[e][None, :]
            q, k, v, bb, cz = _in_even(xs, seq, gn[0], mx[1], mx[0], w_in, rope, True)
            if ctx_live:
                qc, kc, vc, bc, czc = _in_even(hc, ctx_len, gn[0], mcb[1], mcb[0], w_in, None, True)
            else:
                kc, vc = _in_even(hc, ctx_len, gn[0], mcb[1], mcb[0], w_in[:, Q_DIM:Q_DIM + 2 * KV_DIM], None, False)
            vc3 = vc.reshape(bsz, ctx_len, -1)
            ax = _attention(sink, q.reshape(bsz, seq, -1), k, v.reshape(bsz, seq, -1), kc, vc3, True)
            sx = _sconv(bb.reshape(bsz, seq, -1), cz.reshape(bsz, seq, -1), sconv_w[e])
            xs = _out_proj(ax.reshape(bsz * seq, -1), sx.reshape(bsz * seq, -1), wa, wb, xs, seq, gn[1], mx[2])
            if ctx_live:
                ac = _attention(sink, qc.reshape(bsz, ctx_len, -1), None, None, kc, vc3, False)
                sc_ = _sconv(bc.reshape(bsz, ctx_len, -1), czc.reshape(bsz, ctx_len, -1), sconv_w[e])
                hc = _out_proj(ac.reshape(bsz * ctx_len, -1), sc_.reshape(bsz * ctx_len, -1), wa, wb, hc, ctx_len,
                               gn[1], mcb[2])
        else:
            o = i // 2
            w_in = mix_w_in[o].astype(BF16)
            w_out = mix_w_out[o].astype(BF16)
            wa, wb = w_out[:POOL_DIM], w_out[POOL_DIM:]
            wg = pool_w_grp[o].astype(BF16)
            ps = pool_scale[o][None, :]
            streams = [(xs, seq, mx, bsz)]
            if ctx_live:
                streams.append((hc, ctx_len, mcb, bsz))
            outs = []
            for h2, t, m, nb in streams:
                pu, ab = _in_odd(h2, t, gn[0], m[1], m[0], w_in, chan_c, chan_s)
                yp = _pool(pu.reshape(nb, t, -1), wg, ps)
                ct, st = seq_tabs[t]
                yf = _seq_dft(ab.reshape(nb, t, -1), ct, st)
                outs.append(_out_proj(yp.reshape(nb * t, -1), yf.reshape(nb * t, -1), wa, wb, h2, t, gn[1], m[2]))
            xs = outs[0]
            if ctx_live:
                hc = outs[1]

        xs = _ffn(xs, seq, gn[2], mx[4], mx[3], mx[5], gn[3], w_up, conv_w, w_down)
        if ctx_live:
            hc = _ffn(hc, ctx_len, gn[2], mc[4], mc[3], mc[5], gn[3], w_up, conv_w, w_down)
    return xs.reshape(bsz, seq, d)
```

```python
import functools

import numpy as np
import jax
import jax.numpy as jnp
from jax import lax
from jax.experimental import pallas as pl
from jax.experimental.pallas import tpu as pltpu

F32 = jnp.float32
BF16 = jnp.bfloat16

D_MODEL = 1024
DEPTH = 4
GRID_W = 64
EPS = 1e-6
N_Q_HEADS = 8
N_KV_HEADS = 2
HEAD_DIM = 64
WINDOW = 128
BLOCK = 128
ROPE_BASE = 10000.0
SCONV_DIM = 512
POOL_DIM = 512
N_POOL_GROUPS = 4
POOL_GROUP = POOL_DIM // N_POOL_GROUPS
POOL_WINDOWS = (2, 4, 8, 16)
FFT_DIM = 512
N_FFT_GROUPS = 4
FFT_GROUP = FFT_DIM // N_FFT_GROUPS
D_FF = 2816
Q_DIM = N_Q_HEADS * HEAD_DIM
KV_DIM = N_KV_HEADS * HEAD_DIM

LANES = 128
MASK_VALUE = -1e30
LOG2E = 1.4426950408889634
ATTN_QBLOCKS = 4
VMEM_LIMIT = 56 * 1024 * 1024
ROW_TILE = 512
HALO = 16
FFN_CHUNK = 256

assert HALO >= max(POOL_WINDOWS) and all(w & (w - 1) == 0 for w in POOL_WINDOWS)


def _params(*sem):
    return pltpu.CompilerParams(dimension_semantics=sem, vmem_limit_bytes=VMEM_LIMIT)


def _rms(x, g):
    ms = jnp.mean(x * x, axis=-1, keepdims=True)
    return x * lax.rsqrt(ms + EPS) * g


def _dot(a, b):
    return jnp.dot(a, b, preferred_element_type=F32)


def _halo_specs(tm, rows, width):
    hb = tm // HALO
    return [
        pl.BlockSpec((HALO, width), lambda i: (jnp.maximum(i * hb - 1, 0), 0)),
        pl.BlockSpec((tm, width), lambda i: (i, 0)),
        pl.BlockSpec((HALO, width), lambda i: (jnp.minimum((i + 1) * hb, rows // HALO - 1), 0)),
    ]


def _ext(prev_ref, main_ref, next_ref):
    return jnp.concatenate([prev_ref[...], main_ref[...], next_ref[...]], axis=0)


def _tile_pos(tm, seg):
    return (pl.program_id(0) * tm + lax.broadcasted_iota(jnp.int32, (tm, 1), 0)) % seg


def _seg_ext(x_ext, seg):
    tm = x_ext.shape[0] - 2 * HALO
    i = pl.program_id(0)
    top_ok = (i * tm) % seg != 0
    bot_ok = ((i + 1) * tm) % seg != 0
    top, bot = x_ext[:HALO], x_ext[HALO + tm:]
    return jnp.concatenate([jnp.where(top_ok, top, jnp.zeros_like(top)), x_ext[HALO:HALO + tm],
                            jnp.where(bot_ok, bot, jnp.zeros_like(bot))], axis=0)


def _roll_rows(x_ext, off):
    rows = x_ext.shape[0]
    return pltpu.roll(x_ext, (-off) % rows, 0)


def _conv3(x_ext, seg, w):
    tm = x_ext.shape[0] - 2 * HALO
    main = slice(HALO, HALO + tm)
    xm = _seg_ext(x_ext, seg)
    return _roll_rows(xm, -1)[main] * w[0:1] + x_ext[main] * w[1:2] + _roll_rows(xm, 1)[main] * w[2:3]


def _const_spec(a):
    return pl.BlockSpec(a.shape, lambda *_: (0,) * a.ndim, pipeline_mode=pl.Buffered(1))


def _mod_kernel(c_ref, w_ref, b_ref, o_ref):
    c = c_ref[...]
    s = (c / (1.0 + jnp.exp(-c))).astype(BF16)
    o_ref[0] = _dot(s, w_ref[0].astype(BF16)) + b_ref[0]


def _modulation(cc, mod_w, mod_b):
    depth, d, n = mod_w.shape
    tn = 1024
    return pl.pallas_call(
        _mod_kernel,
        grid=(depth, n // tn),
        in_specs=[
            pl.BlockSpec((16, d), lambda i, j: (0, 0)),
            pl.BlockSpec((1, d, tn), lambda i, j: (i, 0, j)),
            pl.BlockSpec((1, 1, tn), lambda i, j: (i, 0, j)),
        ],
        out_specs=pl.BlockSpec((1, 16, tn), lambda i, j: (i, 0, j)),
        out_shape=jax.ShapeDtypeStruct((depth, 16, n), F32),
        compiler_params=_params("arbitrary", "arbitrary"),
        name="modulation",
    )(cc, mod_w, mod_b.reshape(depth, 1, n))


def _rope128(x, cos, sin_a, sin_b):
    return x * cos + pltpu.roll(x, LANES - 16, 1) * sin_a + pltpu.roll(x, 16, 1) * sin_b


def _write_kv(k, v, k_ref, v_ref):
    lo = lax.broadcasted_iota(jnp.int32, (1, LANES), 1) < HEAD_DIM
    vr = pltpu.roll(v, HEAD_DIM, 1)
    zero = jnp.zeros_like(v)
    v_ref[...] = jnp.concatenate(
        [jnp.where(lo, v, zero), jnp.where(lo, zero, vr), jnp.where(lo, vr, zero), jnp.where(lo, zero, v)],
        axis=1).astype(BF16)
    kt = k.T
    top = lax.broadcasted_iota(jnp.int32, (LANES, 1), 0) < HEAD_DIM
    ktr = pltpu.roll(kt, HEAD_DIM, 0)
    zero = jnp.zeros_like(kt)
    k_ref[...] = jnp.concatenate(
        [jnp.where(top, kt, zero), jnp.where(top, zero, ktr), jnp.where(top, ktr, zero), jnp.where(top, zero, kt)],
        axis=0).astype(BF16)


def _in_even_kernel(xp_ref, x_ref, xn_ref, gn_ref, sc_ref, sh_ref, w_ref, cw_ref, *rest, rope, seg):
    if rope:
        cos_ref, sa_ref, sb_ref = rest[:3]
        rest = rest[3:]
    q_ref, k_ref, v_ref, s_ref = rest
    tm = x_ref.shape[0]
    hh = (_rms(_ext(xp_ref, x_ref, xn_ref), gn_ref[...]) * (1.0 + sc_ref[0]) + sh_ref[0]).astype(BF16)
    h = hh[HALO:HALO + tm]
    q = _dot(h, w_ref[:, 0:Q_DIM])
    k = _dot(h, w_ref[:, Q_DIM:Q_DIM + KV_DIM])
    v = _dot(h, w_ref[:, Q_DIM + KV_DIM:Q_DIM + 2 * KV_DIM])
    if rope:
        cos, sa, sb = cos_ref[...], sa_ref[...], sb_ref[...]
        k = _rope128(k, cos, sa, sb)
        q = jnp.concatenate(
            [_rope128(q[:, LANES * j:LANES * (j + 1)], cos, sa, sb) for j in range(Q_DIM // LANES)], axis=1)
    _write_kv(k, v, k_ref, v_ref)
    q_ref[...] = (q * (HEAD_DIM ** -0.5 * LOG2E)).astype(BF16)
    o = Q_DIM + 2 * KV_DIM
    b = _dot(h, w_ref[:, o:o + SCONV_DIM])
    cz = (_dot(hh, w_ref[:, o + SCONV_DIM:o + 2 * SCONV_DIM])
          * _dot(hh, w_ref[:, o + 2 * SCONV_DIM:o + 3 * SCONV_DIM]))
    s_ref[...] = (b * _conv3(cz, seg, cw_ref[...])).astype(BF16)


def _in_even(x2, seg, gn, sc, sh, w, conv_w, tables):
    rows, d = x2.shape
    tm = min(ROW_TILE, seg)
    per_seg = seg // tm
    rope = tables is not None
    row = lambda i: (i, 0)
    mod = pl.BlockSpec((1, 1, d), lambda i: (i // per_seg, 0, 0))
    in_specs = _halo_specs(tm, rows, d) + [pl.BlockSpec((1, d), lambda i: (0, 0)), mod, mod,
                                           _const_spec(w), _const_spec(conv_w)]
    args = [x2, x2, x2, gn, sc, sh, w, conv_w]
    if rope:
        in_specs += [pl.BlockSpec((tm, LANES), lambda i: (i % per_seg, 0))] * 3
        args += list(tables)
    kv_w = 2 * KV_DIM * 2
    return pl.pallas_call(
        functools.partial(_in_even_kernel, rope=rope, seg=seg),
        grid=(rows // tm,),
        in_specs=in_specs,
        out_specs=[pl.BlockSpec((tm, Q_DIM), row), pl.BlockSpec((kv_w, tm), lambda i: (0, i)),
                   pl.BlockSpec((tm, kv_w), row), pl.BlockSpec((tm, SCONV_DIM), row)],
        out_shape=[jax.ShapeDtypeStruct((rows, Q_DIM), BF16), jax.ShapeDtypeStruct((kv_w, rows), BF16),
                   jax.ShapeDtypeStruct((rows, kv_w), BF16), jax.ShapeDtypeStruct((rows, SCONV_DIM), BF16)],
        compiler_params=_params("arbitrary"),
        name="in_even",
    )(*args)


def _in_kv_kernel(x_ref, gn_ref, sc_ref, sh_ref, w_ref, k_ref, v_ref):
    h = (_rms(x_ref[...], gn_ref[...]) * (1.0 + sc_ref[0]) + sh_ref[0]).astype(BF16)
    _write_kv(_dot(h, w_ref[:, 0:KV_DIM]), _dot(h, w_ref[:, KV_DIM:2 * KV_DIM]), k_ref, v_ref)


def _in_kv(x2, seg, gn, sc, sh, w):
    rows, d = x2.shape
    tm = min(ROW_TILE, seg)
    per_seg = seg // tm
    row = lambda i: (i, 0)
    mod = pl.BlockSpec((1, 1, d), lambda i: (i // per_seg, 0, 0))
    kv_w = 2 * KV_DIM * 2
    return pl.pallas_call(
        _in_kv_kernel,
        grid=(rows // tm,),
        in_specs=[pl.BlockSpec((tm, d), row), pl.BlockSpec((1, d), lambda i: (0, 0)), mod, mod, _const_spec(w)],
        out_specs=[pl.BlockSpec((kv_w, tm), lambda i: (0, i)), pl.BlockSpec((tm, kv_w), row)],
        out_shape=[jax.ShapeDtypeStruct((kv_w, rows), BF16), jax.ShapeDtypeStruct((rows, kv_w), BF16)],
        compiler_params=_params("arbitrary"),
        name="in_kv",
    )(x2, gn, sc, sh, w)


def _in_odd_kernel(xp_ref, x_ref, xn_ref, gn_ref, sc_ref, sh_ref, w_ref, cc_ref, cs_ref, wg_ref, ps_ref,
                   p_ref, ab_ref, *, seg):
    tm = x_ref.shape[0]
    hh = (_rms(_ext(xp_ref, x_ref, xn_ref), gn_ref[...]) * (1.0 + sc_ref[0]) + sh_ref[0]).astype(BF16)
    h = hh[HALO:HALO + tm]
    pos = _tile_pos(tm, seg)
    up = _seg_ext(_dot(hh, w_ref[:, 0:POOL_DIM]), seg)
    outs = []
    for gi, w in enumerate(POOL_WINDOWS):
        ug = up[:, gi * POOL_GROUP:(gi + 1) * POOL_GROUP]
        back, fwd = w // 2, w - w // 2
        run, span = ug, 1
        while span < w:
            run = run + _roll_rows(run, span)
            span *= 2
        total = _roll_rows(run, -back)[HALO:HALO + tm]
        count = (jnp.minimum(pos + fwd, seg) - jnp.maximum(pos - back, 0)).astype(F32)
        outs.append(_dot((total / count - ug[HALO:HALO + tm]).astype(BF16), wg_ref[gi]))
    p_ref[...] = (jnp.concatenate(outs, axis=1) * ps_ref[...]).astype(BF16)
    uf = _dot(h, w_ref[:, POOL_DIM:POOL_DIM + FFT_DIM]).astype(BF16)
    cc, cs = cc_ref[...], cs_ref[...]
    parts = []
    for m in (cc, cs):
        for g in range(N_FFT_GROUPS):
            parts.append(_dot(uf[:, g * FFT_GROUP:(g + 1) * FFT_GROUP], m))
    ab_ref[...] = jnp.concatenate(parts, axis=1).astype(BF16)


def _in_odd(x2, seg, gn, sc, sh, w, cc, cs, wg, ps):
    rows, d = x2.shape
    tm = min(ROW_TILE, seg)
    per_seg = seg // tm
    row = lambda i: (i, 0)
    mod = pl.BlockSpec((1, 1, d), lambda i: (i // per_seg, 0, 0))
    return pl.pallas_call(
        functools.partial(_in_odd_kernel, seg=seg),
        grid=(rows // tm,),
        in_specs=_halo_specs(tm, rows, d) + [pl.BlockSpec((1, d), lambda i: (0, 0)), mod, mod, _const_spec(w),
                                             _const_spec(cc), _const_spec(cs), _const_spec(wg), _const_spec(ps)],
        out_specs=[pl.BlockSpec((tm, POOL_DIM), row), pl.BlockSpec((tm, 2 * FFT_DIM), row)],
        out_shape=[jax.ShapeDtypeStruct((rows, POOL_DIM), BF16), jax.ShapeDtypeStruct((rows, 2 * FFT_DIM), BF16)],
        compiler_params=_params("arbitrary"),
        name="in_odd",
    )(x2, x2, x2, gn, sc, sh, w, cc, cs, wg, ps)


def _attn_kernel(sink_ref, q_ref, *rest, window, qb, nblocks):
    if window:
        ktp, ktc, ktn, vp, vc, vn, kxt, vx, o_ref = rest
    else:
        kxt, vx, o_ref = rest
    n = pl.program_id(1)
    rows = 2 * BLOCK
    first_pair = lax.broadcasted_iota(jnp.int32, (rows, 1), 0) < BLOCK
    lo = lax.broadcasted_iota(jnp.int32, (1, LANES), 1) < HEAD_DIM
    if window:
        qi = lax.broadcasted_iota(jnp.int32, (rows, BLOCK), 0) % BLOCK
        j = lax.broadcasted_iota(jnp.int32, (rows, BLOCK), 1)
        tri_prev = jnp.where(j >= qi, 0.0, MASK_VALUE)
        tri_next = jnp.where(j <= qi, 0.0, MASK_VALUE)
        bias = []
        for t in range(qb):
            blk = n * qb + t
            bias.append((tri_prev + jnp.where(blk >= 1, 0.0, MASK_VALUE),
                         tri_next + jnp.where(blk <= nblocks - 2, 0.0, MASK_VALUE)))
    for h in range(N_KV_HEADS):
        c0 = 2 * LANES * h
        halves = (slice(c0, c0 + LANES), slice(c0 + LANES, c0 + 2 * LANES))
        if window:
            kall = [jnp.concatenate([ktp[sl, :], ktc[sl, :], ktn[sl, :]], axis=1) for sl in halves]
            vall = [jnp.concatenate([vp[0, :, sl], vc[0, :, sl], vn[0, :, sl]], axis=0) for sl in halves]
        for t in range(qb):
            rs = slice(t * BLOCK, (t + 1) * BLOCK)
            kparts, vparts = [], []
            for e, sl in enumerate(halves):
                if window:
                    kparts.append(kall[e][:, t * BLOCK:(t + 3) * BLOCK])
                    vparts.append(vall[e][t * BLOCK:(t + 3) * BLOCK])
                kparts.append(kxt[sl, :])
                vparts.append(vx[0, :, sl])
            kst = jnp.concatenate(kparts, axis=1)
            vst = jnp.concatenate(vparts, axis=0)
            nkeys = vst.shape[0] // 2
            qp = jnp.concatenate([q_ref[0, rs, halves[0]], q_ref[0, rs, halves[1]]], axis=0)
            s = _dot(qp, kst)
            probs, inv = [], []
            for half in range(2):
                tiles = [s[:, half * nkeys + c * LANES:half * nkeys + (c + 1) * LANES] for c in range(nkeys // LANES)]
                if window:
                    tiles[0] = tiles[0] + bias[t][0]
                    tiles[2] = tiles[2] + bias[t][1]
                head = 4 * h + half
                sink_col = jnp.where(first_pair, sink_ref[0, head], sink_ref[0, head + 2]) * LOG2E
                m = jnp.maximum(jnp.max(functools.reduce(jnp.maximum, tiles), axis=-1, keepdims=True), sink_col)
                es = [jnp.exp2(tl - m) for tl in tiles]
                den = jnp.sum(functools.reduce(jnp.add, es), axis=-1, keepdims=True) + jnp.exp2(sink_col - m)
                probs += [e.astype(BF16) for e in es]
                inv.append(1.0 / den)
            o = _dot(jnp.concatenate(probs, axis=1), vst) * jnp.where(lo, inv[0], inv[1])
            o_ref[0, rs, halves[0]] = o[:BLOCK].astype(o_ref.dtype)
            o_ref[0, rs, halves[1]] = o[BLOCK:].astype(o_ref.dtype)


def _attention(sink, q, kt, v, kxt, vx, window):
    b, t, _ = q.shape
    l = vx.shape[1]
    nb = t // BLOCK
    qb = ATTN_QBLOCKS if window else nb
    w = vx.shape[2]
    steps = nb // qb
    cur = lambda i, n: (i, n, 0)
    in_specs = [pl.BlockSpec(memory_space=pltpu.SMEM), pl.BlockSpec((1, qb * BLOCK, Q_DIM), cur)]
    args = [sink, q]
    if window:
        pb = lambda n: jnp.maximum(n * qb - 1, 0)
        nx = lambda n: jnp.minimum(n * qb + qb, nb - 1)
        in_specs += [
            pl.BlockSpec((w, BLOCK), lambda i, n: (0, i * nb + pb(n))),
            pl.BlockSpec((w, qb * BLOCK), lambda i, n: (0, i * steps + n)),
            pl.BlockSpec((w, BLOCK), lambda i, n: (0, i * nb + nx(n))),
            pl.BlockSpec((1, BLOCK, w), lambda i, n: (i, pb(n), 0)),
            pl.BlockSpec((1, qb * BLOCK, w), cur),
            pl.BlockSpec((1, BLOCK, w), lambda i, n: (i, nx(n), 0)),
        ]
        args += [kt, kt, kt, v, v, v]
    in_specs += [pl.BlockSpec((w, l), lambda i, n: (0, i)), pl.BlockSpec((1, l, w), lambda i, n: (i, 0, 0))]
    args += [kxt, vx]
    return pl.pallas_call(
        functools.partial(_attn_kernel, window=window, qb=qb, nblocks=nb),
        grid=(b, steps),
        in_specs=in_specs,
        out_specs=pl.BlockSpec((1, qb * BLOCK, Q_DIM), cur),
        out_shape=jax.ShapeDtypeStruct((b, t, Q_DIM), BF16),
        compiler_params=_params("arbitrary", "arbitrary"),
        name="attention",
    )(*args)


def _seq_dft_kernel(ct_ref, st_ref, ab_ref, o_ref, *, norm):
    a = ab_ref[0, :, 0:FFT_DIM]
    b = ab_ref[0, :, FFT_DIM:2 * FFT_DIM]
    y = _dot(ct_ref[...], a) - _dot(st_ref[...], b)
    o_ref[0] = (y * norm).astype(o_ref.dtype)


def _seq_dft(ab3, ct, st):
    b, t, _ = ab3.shape
    tk = min(ROW_TILE, t)
    return pl.pallas_call(
        functools.partial(_seq_dft_kernel, norm=float((t * FFT_GROUP) ** -0.5)),
        grid=(b, t // tk),
        in_specs=[
            pl.BlockSpec((tk, t), lambda i, j: (j, 0)),
            pl.BlockSpec((tk, t), lambda i, j: (j, 0)),
            pl.BlockSpec((1, t, 2 * FFT_DIM), lambda i, j: (i, 0, 0)),
        ],
        out_specs=pl.BlockSpec((1, tk, FFT_DIM), lambda i, j: (i, j, 0)),
        out_shape=jax.ShapeDtypeStruct((b, t, FFT_DIM), BF16),
        compiler_params=_params("arbitrary", "arbitrary"),
        name="seq_dft",
    )(ct, st, ab3)


def _post_kernel(ap_ref, a_ref, an_ref, bp_ref, b_ref, bn_ref, xp_ref, x_ref, xn_ref,
                 wa_ref, wb_ref, gn1_ref, g1_ref, gn2_ref, sc_ref, sh_ref, g2_ref, gn3_ref,
                 wu_ref, cw_ref, wd_ref, o_ref, act_scr, *, seg):
    tm = x_ref.shape[0]
    main = slice(HALO, HALO + tm)
    y = _dot(_ext(ap_ref, a_ref, an_ref), wa_ref[...]) + _dot(_ext(bp_ref, b_ref, bn_ref), wb_ref[...])
    xm = _ext(xp_ref, x_ref, xn_ref) + g1_ref[0] * _rms(y, gn1_ref[...])
    hh = (_rms(xm, gn2_ref[...]) * (1.0 + sc_ref[0]) + sh_ref[0]).astype(BF16)
    h = hh[main]
    cw = cw_ref[...]
    for c in range(D_FF // FFN_CHUNK):
        cs = slice(c * FFN_CHUNK, (c + 1) * FFN_CHUNK)
        conv = _conv3(_dot(hh, wu_ref[:, cs]), seg, cw[:, cs])
        val = _dot(h, wu_ref[:, D_FF + c * FFN_CHUNK:D_FF + (c + 1) * FFN_CHUNK])
        act_scr[:, cs] = (conv / (1.0 + jnp.exp(-conv)) * val).astype(BF16)
    y2 = _dot(act_scr[...], wd_ref[...])
    o_ref[...] = xm[main] + g2_ref[0] * _rms(y2, gn3_ref[...])


def _post(a2, b2, wa, wb, x2, seg, gn1, g1, gn2, sc, sh, g2, gn3, w_up, conv_w, w_down):
    rows, d = x2.shape
    tm = min(ROW_TILE, seg)
    per_seg = seg // tm
    mod = pl.BlockSpec((1, 1, d), (lambda i: (i // per_seg, 0, 0)) if g1.shape[0] > 1 else (lambda i: (0, 0, 0)))
    vec = pl.BlockSpec((1, d), lambda i: (0, 0))
    return pl.pallas_call(
        functools.partial(_post_kernel, seg=seg),
        grid=(rows // tm,),
        in_specs=(_halo_specs(tm, rows, a2.shape[1]) + _halo_specs(tm, rows, b2.shape[1]) + _halo_specs(tm, rows, d)
                  + [_const_spec(wa), _const_spec(wb), vec, mod, vec, mod, mod, mod, vec,
                     _const_spec(w_up), _const_spec(conv_w), _const_spec(w_down)]),
        out_specs=pl.BlockSpec((tm, d), lambda i: (i, 0)),
        out_shape=jax.ShapeDtypeStruct((rows, d), F32),
        scratch_shapes=[pltpu.VMEM((tm, D_FF), BF16)],
        compiler_params=_params("arbitrary"),
        name="post",
    )(a2, a2, a2, b2, b2, b2, x2, x2, x2, wa, wb, gn1, g1, gn2, sc, sh, g2, gn3, w_up, conv_w, w_down)


def _rope_tables(seq):
    half = HEAD_DIM // 2
    nf = half // 2
    inv = ROPE_BASE ** (-np.arange(nf, dtype=np.float64) / nf)
    t = np.arange(seq)
    lane = np.arange(LANES)
    in_head = lane % HEAD_DIM
    pos = np.where((in_head // half)[None, :] == 0, (t // GRID_W)[:, None], (t % GRID_W)[:, None])
    ang = pos * inv[lane % nf][None, :]
    first = ((lane % half) < nf)[None, :]
    cos = np.cos(ang)
    sin_a = np.where(first, -np.sin(ang), 0.0)
    sin_b = np.where(first, 0.0, np.sin(ang))
    return tuple(jnp.asarray(a, F32) for a in (cos, sin_a, sin_b))


def _dft_tables(n):
    idx = np.arange(n)
    ang = 2.0 * np.pi * ((idx[:, None] * idx[None, :]) % n) / n
    return jnp.asarray(np.cos(ang), F32).astype(BF16), jnp.asarray(np.sin(ang), F32).astype(BF16)


def kernel(x, c, ctx, c_ctx, mod_w, mod_b, norm_g, att_w_in, att_sink, sconv_w, att_w_out, mix_w_in,
           pool_w_grp, pool_scale, mix_w_out, ffn_w_up, ffn_conv, ffn_w_down):
    bsz, seq, d = x.shape
    ctx_len = ctx.shape[1]
    assert d == D_MODEL and seq % ROW_TILE == 0 and (bsz * ctx_len) % ROW_TILE == 0 and ROW_TILE % ctx_len == 0

    cc = jnp.concatenate([c, c_ctx[None, :], jnp.zeros((16 - bsz - 1, d), F32)], axis=0)
    mods = _modulation(cc, mod_w, mod_b)

    rope = _rope_tables(seq)
    chan_c, chan_s = _dft_tables(FFT_GROUP)
    seq_tabs = {seq: _dft_tables(seq), ctx_len: _dft_tables(ctx_len)}

    xs = x.reshape(bsz * seq, d)
    hc = ctx.reshape(bsz * ctx_len, d)

    for i in range(DEPTH):
        even = i % 2 == 0
        ctx_live = any(j % 2 == 0 for j in range(i + 1, DEPTH))
        gn = [norm_g[i][j][None, :] for j in range(4)]
        mx = [mods[i, :bsz, j * d:(j + 1) * d].reshape(bsz, 1, d) for j in range(6)]
        mc = [mods[i, bsz:bsz + 1, j * d:(j + 1) * d].reshape(1, 1, d) for j in range(6)]
        mcb = [jnp.broadcast_to(m, (bsz, 1, d)) for m in mc]
        ffn = (ffn_w_up[i].astype(BF16), ffn_conv[i], ffn_w_down[i].astype(BF16))

        if even:
            e = i // 2
            w_in = att_w_in[e].astype(BF16)
            w_out = att_w_out[e].astype(BF16)
            wa, wb = w_out[:Q_DIM], w_out[Q_DIM:]
            sink = att_sink[e][None, :]
            q, kt, v, sx = _in_even(xs, seq, gn[0], mx[1], mx[0], w_in, sconv_w[e], rope)
            if ctx_live:
                qc, kct, vc, sxc = _in_even(hc, ctx_len, gn[0], mcb[1], mcb[0], w_in, sconv_w[e], None)
            else:
                kct, vc = _in_kv(hc, ctx_len, gn[0], mcb[1], mcb[0], w_in[:, Q_DIM:Q_DIM + 2 * KV_DIM])
            vc3 = vc.reshape(bsz, ctx_len, -1)
            ax = _attention(sink, q.reshape(bsz, seq, -1), kt, v.reshape(bsz, seq, -1), kct, vc3, True)
            xs = _post(ax.reshape(bsz * seq, -1), sx, wa, wb, xs, seq, gn[1], mx[2], gn[2], mx[4], mx[3], mx[5],
                       gn[3], *ffn)
            if ctx_live:
                ac = _attention(sink, qc.reshape(bsz, ctx_len, -1), None, None, kct, vc3, False)
                hc = _post(ac.reshape(bsz * ctx_len, -1), sxc, wa, wb, hc, ctx_len, gn[1], mc[2], gn[2], mc[4],
                           mc[3], mc[5], gn[3], *ffn)
        else:
            o = i // 2
            w_in = mix_w_in[o].astype(BF16)
            w_out = mix_w_out[o].astype(BF16)
            wa, wb = w_out[:POOL_DIM], w_out[POOL_DIM:]
            wg = pool_w_grp[o].astype(BF16)
            ps = pool_scale[o][None, :]
            streams = [(xs, seq, mx, mx)]
            if ctx_live:
                streams.append((hc, ctx_len, mcb, mc))
            outs = []
            for h2, t, m_in, m in streams:
                yp, ab = _in_odd(h2, t, gn[0], m_in[1], m_in[0], w_in, chan_c, chan_s, wg, ps)
                ct, st = seq_tabs[t]
                yf = _seq_dft(ab.reshape(bsz, t, -1), ct, st)
                outs.append(_post(yp, yf.reshape(bsz * t, -1), wa, wb, h2, t, gn[1], m[2], gn[2], m[4], m[3], m[5],
                                  gn[3], *ffn))
            xs = outs[0]
            if ctx_live:
                hc = outs[1]
    return xs.reshape(bsz, seq, d)
```

```python
import functools

import numpy as np
import jax
import jax.numpy as jnp
from jax import lax
from jax.experimental import pallas as pl
from jax.experimental.pallas import tpu as pltpu

F32 = jnp.float32
BF16 = jnp.bfloat16

D_MODEL = 1024
DEPTH = 4
GRID_W = 64
EPS = 1e-6
N_Q_HEADS = 8
N_KV_HEADS = 2
HEAD_DIM = 64
WINDOW = 128
BLOCK = 128
ROPE_BASE = 10000.0
SCONV_DIM = 512
POOL_DIM = 512
N_POOL_GROUPS = 4
POOL_GROUP = POOL_DIM // N_POOL_GROUPS
POOL_WINDOWS = (2, 4, 8, 16)
FFT_DIM = 512
N_FFT_GROUPS = 4
FFT_GROUP = FFT_DIM // N_FFT_GROUPS
D_FF = 2816
Q_DIM = N_Q_HEADS * HEAD_DIM
KV_DIM = N_KV_HEADS * HEAD_DIM

LANES = 128
MASK_VALUE = -1e30
LOG2E = 1.4426950408889634
ATTN_QBLOCKS = 4
VMEM_LIMIT = 56 * 1024 * 1024
ROW_TILE = 512
HALO = 16
FFN_CHUNK = 256
POST_PIECES = 8

assert HALO >= max(POOL_WINDOWS) and all(w & (w - 1) == 0 for w in POOL_WINDOWS)


def _params(*sem):
    return pltpu.CompilerParams(dimension_semantics=sem, vmem_limit_bytes=VMEM_LIMIT)


def _rms(x, g):
    ms = jnp.mean(x * x, axis=-1, keepdims=True)
    return x * lax.rsqrt(ms + EPS) * g


def _dot(a, b):
    return jnp.dot(a, b, preferred_element_type=F32)


def _halo_specs(tm, rows, width, tile=lambda i: i):
    hb = tm // HALO
    return [
        pl.BlockSpec((HALO, width), lambda i: (jnp.maximum(tile(i) * hb - 1, 0), 0)),
        pl.BlockSpec((tm, width), lambda i: (tile(i), 0)),
        pl.BlockSpec((HALO, width), lambda i: (jnp.minimum((tile(i) + 1) * hb, rows // HALO - 1), 0)),
    ]


def _ext(prev_ref, main_ref, next_ref):
    return jnp.concatenate([prev_ref[...], main_ref[...], next_ref[...]], axis=0)


def _tile_pos(tm, seg):
    return (pl.program_id(0) * tm + lax.broadcasted_iota(jnp.int32, (tm, 1), 0)) % seg


def _seg_ext(x_ext, seg, tile=None):
    tm = x_ext.shape[0] - 2 * HALO
    i = pl.program_id(0) if tile is None else tile
    top_ok = (i * tm) % seg != 0
    bot_ok = ((i + 1) * tm) % seg != 0
    top, bot = x_ext[:HALO], x_ext[HALO + tm:]
    return jnp.concatenate([jnp.where(top_ok, top, jnp.zeros_like(top)), x_ext[HALO:HALO + tm],
                            jnp.where(bot_ok, bot, jnp.zeros_like(bot))], axis=0)


def _roll_rows(x_ext, off):
    rows = x_ext.shape[0]
    return pltpu.roll(x_ext, (-off) % rows, 0)


def _conv3(x_ext, seg, w, tile=None):
    tm = x_ext.shape[0] - 2 * HALO
    main = slice(HALO, HALO + tm)
    xm = _seg_ext(x_ext, seg, tile)
    return _roll_rows(xm, -1)[main] * w[0:1] + x_ext[main] * w[1:2] + _roll_rows(xm, 1)[main] * w[2:3]


def _const_spec(a):
    return pl.BlockSpec(a.shape, lambda *_: (0,) * a.ndim, pipeline_mode=pl.Buffered(1))


def _mod_kernel(c_ref, w_ref, b_ref, o_ref):
    c = c_ref[...]
    s = (c / (1.0 + jnp.exp(-c))).astype(BF16)
    o_ref[0] = _dot(s, w_ref[0].astype(BF16)) + b_ref[0]


def _modulation(cc, mod_w, mod_b):
    depth, d, n = mod_w.shape
    tn = 1024
    return pl.pallas_call(
        _mod_kernel,
        grid=(depth, n // tn),
        in_specs=[
            pl.BlockSpec((16, d), lambda i, j: (0, 0)),
            pl.BlockSpec((1, d, tn), lambda i, j: (i, 0, j)),
            pl.BlockSpec((1, 1, tn), lambda i, j: (i, 0, j)),
        ],
        out_specs=pl.BlockSpec((1, 16, tn), lambda i, j: (i, 0, j)),
        out_shape=jax.ShapeDtypeStruct((depth, 16, n), F32),
        compiler_params=_params("arbitrary", "arbitrary"),
        name="modulation",
    )(cc, mod_w, mod_b.reshape(depth, 1, n))


def _rope128(x, cos, sin_a, sin_b):
    return x * cos + pltpu.roll(x, LANES - 16, 1) * sin_a + pltpu.roll(x, 16, 1) * sin_b


def _write_kv(k, v, k_ref, v_ref):
    lo = lax.broadcasted_iota(jnp.int32, (1, LANES), 1) < HEAD_DIM
    vr = pltpu.roll(v, HEAD_DIM, 1)
    zero = jnp.zeros_like(v)
    v_ref[...] = jnp.concatenate(
        [jnp.where(lo, v, zero), jnp.where(lo, zero, vr), jnp.where(lo, vr, zero), jnp.where(lo, zero, v)],
        axis=1).astype(BF16)
    kt = k.T
    top = lax.broadcasted_iota(jnp.int32, (LANES, 1), 0) < HEAD_DIM
    ktr = pltpu.roll(kt, HEAD_DIM, 0)
    zero = jnp.zeros_like(kt)
    k_ref[...] = jnp.concatenate(
        [jnp.where(top, kt, zero), jnp.where(top, zero, ktr), jnp.where(top, ktr, zero), jnp.where(top, zero, kt)],
        axis=0).astype(BF16)


def _in_even_kernel(xp_ref, x_ref, xn_ref, gn_ref, sc_ref, sh_ref, w_ref, cw_ref, *rest, rope, seg):
    if rope:
        cos_ref, sa_ref, sb_ref = rest[:3]
        rest = rest[3:]
    q_ref, k_ref, v_ref, s_ref = rest
    tm = x_ref.shape[0]
    hh = (_rms(_ext(xp_ref, x_ref, xn_ref), gn_ref[...]) * (1.0 + sc_ref[0]) + sh_ref[0]).astype(BF16)
    h = hh[HALO:HALO + tm]
    q = _dot(h, w_ref[:, 0:Q_DIM])
    k = _dot(h, w_ref[:, Q_DIM:Q_DIM + KV_DIM])
    v = _dot(h, w_ref[:, Q_DIM + KV_DIM:Q_DIM + 2 * KV_DIM])
    if rope:
        cos, sa, sb = cos_ref[...], sa_ref[...], sb_ref[...]
        k = _rope128(k, cos, sa, sb)
        q = jnp.concatenate(
            [_rope128(q[:, LANES * j:LANES * (j + 1)], cos, sa, sb) for j in range(Q_DIM // LANES)], axis=1)
    _write_kv(k, v, k_ref, v_ref)
    q_ref[...] = (q * (HEAD_DIM ** -0.5 * LOG2E)).astype(BF16)
    o = Q_DIM + 2 * KV_DIM
    b = _dot(h, w_ref[:, o:o + SCONV_DIM])
    cz = (_dot(hh, w_ref[:, o + SCONV_DIM:o + 2 * SCONV_DIM])
          * _dot(hh, w_ref[:, o + 2 * SCONV_DIM:o + 3 * SCONV_DIM]))
    s_ref[...] = (b * _conv3(cz, seg, cw_ref[...])).astype(BF16)


def _in_even(x2, seg, gn, sc, sh, w, conv_w, tables):
    rows, d = x2.shape
    tm = min(ROW_TILE, seg)
    per_seg = seg // tm
    rope = tables is not None
    row = lambda i: (i, 0)
    mod = pl.BlockSpec((1, 1, d), lambda i: (i // per_seg, 0, 0))
    in_specs = _halo_specs(tm, rows, d) + [pl.BlockSpec((1, d), lambda i: (0, 0)), mod, mod,
                                           _const_spec(w), _const_spec(conv_w)]
    args = [x2, x2, x2, gn, sc, sh, w, conv_w]
    if rope:
        in_specs += [pl.BlockSpec((tm, LANES), lambda i: (i % per_seg, 0))] * 3
        args += list(tables)
    kv_w = 2 * KV_DIM * 2
    return pl.pallas_call(
        functools.partial(_in_even_kernel, rope=rope, seg=seg),
        grid=(rows // tm,),
        in_specs=in_specs,
        out_specs=[pl.BlockSpec((tm, Q_DIM), row), pl.BlockSpec((kv_w, tm), lambda i: (0, i)),
                   pl.BlockSpec((tm, kv_w), row), pl.BlockSpec((tm, SCONV_DIM), row)],
        out_shape=[jax.ShapeDtypeStruct((rows, Q_DIM), BF16), jax.ShapeDtypeStruct((kv_w, rows), BF16),
                   jax.ShapeDtypeStruct((rows, kv_w), BF16), jax.ShapeDtypeStruct((rows, SCONV_DIM), BF16)],
        compiler_params=_params("arbitrary"),
        name="in_even",
    )(*args)


def _in_kv_kernel(x_ref, gn_ref, sc_ref, sh_ref, w_ref, k_ref, v_ref):
    h = (_rms(x_ref[...], gn_ref[...]) * (1.0 + sc_ref[0]) + sh_ref[0]).astype(BF16)
    _write_kv(_dot(h, w_ref[:, 0:KV_DIM]), _dot(h, w_ref[:, KV_DIM:2 * KV_DIM]), k_ref, v_ref)


def _in_kv(x2, seg, gn, sc, sh, w):
    rows, d = x2.shape
    tm = min(ROW_TILE, seg)
    per_seg = seg // tm
    row = lambda i: (i, 0)
    mod = pl.BlockSpec((1, 1, d), lambda i: (i // per_seg, 0, 0))
    kv_w = 2 * KV_DIM * 2
    return pl.pallas_call(
        _in_kv_kernel,
        grid=(rows // tm,),
        in_specs=[pl.BlockSpec((tm, d), row), pl.BlockSpec((1, d), lambda i: (0, 0)), mod, mod, _const_spec(w)],
        out_specs=[pl.BlockSpec((kv_w, tm), lambda i: (0, i)), pl.BlockSpec((tm, kv_w), row)],
        out_shape=[jax.ShapeDtypeStruct((kv_w, rows), BF16), jax.ShapeDtypeStruct((rows, kv_w), BF16)],
        compiler_params=_params("arbitrary"),
        name="in_kv",
    )(x2, gn, sc, sh, w)


def _in_odd_kernel(xp_ref, x_ref, xn_ref, gn_ref, sc_ref, sh_ref, w_ref, cc_ref, cs_ref, wg_ref, ps_ref,
                   p_ref, ab_ref, *, seg):
    tm = x_ref.shape[0]
    hh = (_rms(_ext(xp_ref, x_ref, xn_ref), gn_ref[...]) * (1.0 + sc_ref[0]) + sh_ref[0]).astype(BF16)
    h = hh[HALO:HALO + tm]
    pos = _tile_pos(tm, seg)
    up = _seg_ext(_dot(hh, w_ref[:, 0:POOL_DIM]), seg)
    outs = []
    for gi, w in enumerate(POOL_WINDOWS):
        ug = up[:, gi * POOL_GROUP:(gi + 1) * POOL_GROUP]
        back, fwd = w // 2, w - w // 2
        run, span = ug, 1
        while span < w:
            run = run + _roll_rows(run, span)
            span *= 2
        total = _roll_rows(run, -back)[HALO:HALO + tm]
        count = (jnp.minimum(pos + fwd, seg) - jnp.maximum(pos - back, 0)).astype(F32)
        outs.append(_dot((total / count - ug[HALO:HALO + tm]).astype(BF16), wg_ref[gi]))
    p_ref[...] = (jnp.concatenate(outs, axis=1) * ps_ref[...]).astype(BF16)
    uf = _dot(h, w_ref[:, POOL_DIM:POOL_DIM + FFT_DIM]).astype(BF16)
    cc, cs = cc_ref[...], cs_ref[...]
    parts = []
    for m in (cc, cs):
        for g in range(N_FFT_GROUPS):
            parts.append(_dot(uf[:, g * FFT_GROUP:(g + 1) * FFT_GROUP], m))
    ab_ref[...] = jnp.concatenate(parts, axis=1).astype(BF16)


def _in_odd(x2, seg, gn, sc, sh, w, cc, cs, wg, ps):
    rows, d = x2.shape
    tm = min(ROW_TILE, seg)
    per_seg = seg // tm
    row = lambda i: (i, 0)
    mod = pl.BlockSpec((1, 1, d), lambda i: (i // per_seg, 0, 0))
    return pl.pallas_call(
        functools.partial(_in_odd_kernel, seg=seg),
        grid=(rows // tm,),
        in_specs=_halo_specs(tm, rows, d) + [pl.BlockSpec((1, d), lambda i: (0, 0)), mod, mod, _const_spec(w),
                                             _const_spec(cc), _const_spec(cs), _const_spec(wg), _const_spec(ps)],
        out_specs=[pl.BlockSpec((tm, POOL_DIM), row), pl.BlockSpec((tm, 2 * FFT_DIM), row)],
        out_shape=[jax.ShapeDtypeStruct((rows, POOL_DIM), BF16), jax.ShapeDtypeStruct((rows, 2 * FFT_DIM), BF16)],
        compiler_params=_params("arbitrary"),
        name="in_odd",
    )(x2, x2, x2, gn, sc, sh, w, cc, cs, wg, ps)


def _attn_kernel(sink_ref, q_ref, *rest, window, qb, nblocks):
    if window:
        ktp, ktc, ktn, vp, vc, vn, kxt, vx, o_ref = rest
    else:
        kxt, vx, o_ref = rest
    n = pl.program_id(1)
    rows = 2 * BLOCK
    first_pair = lax.broadcasted_iota(jnp.int32, (rows, 1), 0) < BLOCK
    lo = lax.broadcasted_iota(jnp.int32, (1, LANES), 1) < HEAD_DIM
    if window:
        qi = lax.broadcasted_iota(jnp.int32, (rows, BLOCK), 0) % BLOCK
        j = lax.broadcasted_iota(jnp.int32, (rows, BLOCK), 1)
        tri_prev = jnp.where(j >= qi, 0.0, MASK_VALUE)
        tri_next = jnp.where(j <= qi, 0.0, MASK_VALUE)
        bias = []
        for t in range(qb):
            blk = n * qb + t
            bias.append((tri_prev + jnp.where(blk >= 1, 0.0, MASK_VALUE),
                         tri_next + jnp.where(blk <= nblocks - 2, 0.0, MASK_VALUE)))
    for h in range(N_KV_HEADS):
        c0 = 2 * LANES * h
        halves = (slice(c0, c0 + LANES), slice(c0 + LANES, c0 + 2 * LANES))
        if window:
            kall = [jnp.concatenate([ktp[sl, :], ktc[sl, :], ktn[sl, :]], axis=1) for sl in halves]
            vall = [jnp.concatenate([vp[0, :, sl], vc[0, :, sl], vn[0, :, sl]], axis=0) for sl in halves]
        for t in range(qb):
            rs = slice(t * BLOCK, (t + 1) * BLOCK)
            kparts, vparts = [], []
            for e, sl in enumerate(halves):
                if window:
                    kparts.append(kall[e][:, t * BLOCK:(t + 3) * BLOCK])
                    vparts.append(vall[e][t * BLOCK:(t + 3) * BLOCK])
                kparts.append(kxt[sl, :])
                vparts.append(vx[0, :, sl])
            kst = jnp.concatenate(kparts, axis=1)
            vst = jnp.concatenate(vparts, axis=0)
            nkeys = vst.shape[0] // 2
            qp = jnp.concatenate([q_ref[0, rs, halves[0]], q_ref[0, rs, halves[1]]], axis=0)
            s = _dot(qp, kst)
            probs, inv = [], []
            for half in range(2):
                tiles = [s[:, half * nkeys + c * LANES:half * nkeys + (c + 1) * LANES] for c in range(nkeys // LANES)]
                if window:
                    tiles[0] = tiles[0] + bias[t][0]
                    tiles[2] = tiles[2] + bias[t][1]
                head = 4 * h + half
                sink_col = jnp.where(first_pair, sink_ref[0, head], sink_ref[0, head + 2]) * LOG2E
                m = jnp.maximum(jnp.max(functools.reduce(jnp.maximum, tiles), axis=-1, keepdims=True), sink_col)
                es = [jnp.exp2(tl - m) for tl in tiles]
                den = jnp.sum(functools.reduce(jnp.add, es), axis=-1, keepdims=True) + jnp.exp2(sink_col - m)
                probs += [e.astype(BF16) for e in es]
                inv.append(1.0 / den)
            o = _dot(jnp.concatenate(probs, axis=1), vst) * jnp.where(lo, inv[0], inv[1])
            o_ref[0, rs, halves[0]] = o[:BLOCK].astype(o_ref.dtype)
            o_ref[0, rs, halves[1]] = o[BLOCK:].astype(o_ref.dtype)


def _attention(sink, q, kt, v, kxt, vx, window):
    b, t, _ = q.shape
    l = vx.shape[1]
    nb = t // BLOCK
    qb = ATTN_QBLOCKS if window else nb
    w = vx.shape[2]
    steps = nb // qb
    cur = lambda i, n: (i, n, 0)
    in_specs = [pl.BlockSpec(memory_space=pltpu.SMEM), pl.BlockSpec((1, qb * BLOCK, Q_DIM), cur)]
    args = [sink, q]
    if window:
        pb = lambda n: jnp.maximum(n * qb - 1, 0)
        nx = lambda n: jnp.minimum(n * qb + qb, nb - 1)
        in_specs += [
            pl.BlockSpec((w, BLOCK), lambda i, n: (0, i * nb + pb(n))),
            pl.BlockSpec((w, qb * BLOCK), lambda i, n: (0, i * steps + n)),
            pl.BlockSpec((w, BLOCK), lambda i, n: (0, i * nb + nx(n))),
            pl.BlockSpec((1, BLOCK, w), lambda i, n: (i, pb(n), 0)),
            pl.BlockSpec((1, qb * BLOCK, w), cur),
            pl.BlockSpec((1, BLOCK, w), lambda i, n: (i, nx(n), 0)),
        ]
        args += [kt, kt, kt, v, v, v]
    in_specs += [pl.BlockSpec((w, l), lambda i, n: (0, i)), pl.BlockSpec((1, l, w), lambda i, n: (i, 0, 0))]
    args += [kxt, vx]
    return pl.pallas_call(
        functools.partial(_attn_kernel, window=window, qb=qb, nblocks=nb),
        grid=(b, steps),
        in_specs=in_specs,
        out_specs=pl.BlockSpec((1, qb * BLOCK, Q_DIM), cur),
        out_shape=jax.ShapeDtypeStruct((b, t, Q_DIM), BF16),
        compiler_params=_params("arbitrary", "arbitrary"),
        name="attention",
    )(*args)


def _seq_dft_kernel(ct_ref, st_ref, ab_ref, o_ref, *, norm):
    a = ab_ref[0, :, 0:FFT_DIM]
    b = ab_ref[0, :, FFT_DIM:2 * FFT_DIM]
    y = _dot(ct_ref[...], a) - _dot(st_ref[...], b)
    o_ref[0] = (y * norm).astype(o_ref.dtype)


def _seq_dft(ab3, ct, st):
    b, t, _ = ab3.shape
    tk = min(ROW_TILE, t)
    return pl.pallas_call(
        functools.partial(_seq_dft_kernel, norm=float((t * FFT_GROUP) ** -0.5)),
        grid=(b, t // tk),
        in_specs=[
            pl.BlockSpec((tk, t), lambda i, j: (j, 0)),
            pl.BlockSpec((tk, t), lambda i, j: (j, 0)),
            pl.BlockSpec((1, t, 2 * FFT_DIM), lambda i, j: (i, 0, 0)),
        ],
        out_specs=pl.BlockSpec((1, tk, FFT_DIM), lambda i, j: (i, j, 0)),
        out_shape=jax.ShapeDtypeStruct((b, t, FFT_DIM), BF16),
        compiler_params=_params("arbitrary", "arbitrary"),
        name="seq_dft",
    )(ct, st, ab3)


def _post_kernel(ap_ref, a_ref, an_ref, bp_ref, b_ref, bn_ref, xp_ref, x_ref, xn_ref,
                 wa_ref, wb_ref, gn1_ref, g1_ref, gn2_ref, sc_ref, sh_ref, g2_ref, gn3_ref,
                 wu_ref, cw_ref, wd_ref, o_ref, y_scr, hh_scr, xm_scr, act_scr, *, seg):
    i = pl.program_id(0)
    tm = x_ref.shape[0]
    main = slice(HALO, HALO + tm)
    rd = i % 2
    wr = 1 - rd
    piece_rows = tm // POST_PIECES
    pieces = ([(0, xp_ref, 0, HALO)]
              + [(HALO + k * piece_rows, x_ref, k * piece_rows, piece_rows) for k in range(POST_PIECES)]
              + [(HALO + tm, xn_ref, 0, HALO)])

    def prepare(piece):
        lo, ref, off, nrows = piece
        xm = ref[off:off + nrows] + g1_ref[0] * _rms(y_scr[lo:lo + nrows], gn1_ref[...])
        hn = _rms(xm, gn2_ref[...]) * (1.0 + sc_ref[0]) + sh_ref[0]
        hh_scr[wr, lo:lo + nrows] = hn.astype(BF16)
        if ref is x_ref:
            xm_scr[wr, off:off + nrows] = xm
        fold = sum(hn[r:r + 8] for r in range(0, nrows, 8))
        fold = sum(fold[:, l:l + FFN_CHUNK] for l in range(0, fold.shape[1], FFN_CHUNK))
        return jnp.minimum(jnp.abs(fold[0:3]), 0.0)

    def out_projection():
        y_scr[...] = (_dot(_ext(ap_ref, a_ref, an_ref), wa_ref[...])
                      + _dot(_ext(bp_ref, b_ref, bn_ref), wb_ref[...]))

    @pl.when(i == 0)
    def _():
        y_scr[...] = jnp.zeros_like(y_scr)

    @pl.when(i < 2)
    def _():
        for piece in pieces:
            prepare(piece)
        out_projection()

    @pl.when(i >= 2)
    def _():
        hh = hh_scr[rd]
        h = hh[main]
        cw = cw_ref[...]
        anchor = None
        for c in range(D_FF // FFN_CHUNK):
            cs = slice(c * FFN_CHUNK, (c + 1) * FFN_CHUNK)
            cwc = cw[:, cs] if anchor is None else cw[:, cs] + anchor
            conv = _conv3(_dot(hh, wu_ref[:, cs]), seg, cwc, tile=i - 2)
            val = _dot(h, wu_ref[:, D_FF + c * FFN_CHUNK:D_FF + (c + 1) * FFN_CHUNK])
            act_scr[:, cs] = (conv / (1.0 + jnp.exp(-conv)) * val).astype(BF16)
            anchor = prepare(pieces[c]) if c < len(pieces) else None
        y2 = _dot(act_scr[...], wd_ref[...])
        out_projection()
        o_ref[...] = xm_scr[rd] + g2_ref[0] * _rms(y2, gn3_ref[...])


def _post(a2, b2, wa, wb, x2, seg, gn1, g1, gn2, sc, sh, g2, gn3, w_up, conv_w, w_down):
    rows, d = x2.shape
    tm = min(ROW_TILE, seg)
    per_seg = seg // tm
    n = rows // tm
    stage_a = lambda i: jnp.minimum(i, n - 1)
    stage_b = lambda i: jnp.clip(i - 1, 0, n - 1)
    stage_c = lambda i: jnp.maximum(i - 2, 0)
    per_sample = g1.shape[0] > 1
    mod = lambda tile: pl.BlockSpec((1, 1, d), (lambda i: (tile(i) // per_seg, 0, 0)) if per_sample
                                    else (lambda i: (0, 0, 0)))
    vec = pl.BlockSpec((1, d), lambda i: (0, 0))
    return pl.pallas_call(
        functools.partial(_post_kernel, seg=seg),
        grid=(n + 2,),
        in_specs=(_halo_specs(tm, rows, a2.shape[1], stage_a) + _halo_specs(tm, rows, b2.shape[1], stage_a)
                  + _halo_specs(tm, rows, d, stage_b)
                  + [_const_spec(wa), _const_spec(wb), vec, mod(stage_b), vec, mod(stage_b), mod(stage_b),
                     mod(stage_c), vec, _const_spec(w_up), _const_spec(conv_w), _const_spec(w_down)]),
        out_specs=pl.BlockSpec((tm, d), lambda i: (stage_c(i), 0)),
        out_shape=jax.ShapeDtypeStruct((rows, d), F32),
        scratch_shapes=[pltpu.VMEM((tm + 2 * HALO, d), F32), pltpu.VMEM((2, tm + 2 * HALO, d), BF16),
                        pltpu.VMEM((2, tm, d), F32), pltpu.VMEM((tm, D_FF), BF16)],
        compiler_params=_params("arbitrary"),
        name="post",
    )(a2, a2, a2, b2, b2, b2, x2, x2, x2, wa, wb, gn1, g1, gn2, sc, sh, g2, gn3, w_up, conv_w, w_down)


def _rope_tables(seq):
    half = HEAD_DIM // 2
    nf = half // 2
    inv = ROPE_BASE ** (-np.arange(nf, dtype=np.float64) / nf)
    t = np.arange(seq)
    lane = np.arange(LANES)
    in_head = lane % HEAD_DIM
    pos = np.where((in_head // half)[None, :] == 0, (t // GRID_W)[:, None], (t % GRID_W)[:, None])
    ang = pos * inv[lane % nf][None, :]
    first = ((lane % half) < nf)[None, :]
    cos = np.cos(ang)
    sin_a = np.where(first, -np.sin(ang), 0.0)
    sin_b = np.where(first, 0.0, np.sin(ang))
    return tuple(jnp.asarray(a, F32) for a in (cos, sin_a, sin_b))


def _dft_tables(n):
    idx = np.arange(n)
    ang = 2.0 * np.pi * ((idx[:, None] * idx[None, :]) % n) / n
    return jnp.asarray(np.cos(ang), F32).astype(BF16), jnp.asarray(np.sin(ang), F32).astype(BF16)


def kernel(x, c, ctx, c_ctx, mod_w, mod_b, norm_g, att_w_in, att_sink, sconv_w, att_w_out, mix_w_in,
           pool_w_grp, pool_scale, mix_w_out, ffn_w_up, ffn_conv, ffn_w_down):
    bsz, seq, d = x.shape
    ctx_len = ctx.shape[1]
    assert d == D_MODEL and seq % ROW_TILE == 0 and (bsz * ctx_len) % ROW_TILE == 0 and ROW_TILE % ctx_len == 0

    cc = jnp.concatenate([c, c_ctx[None, :], jnp.zeros((16 - bsz - 1, d), F32)], axis=0)
    mods = _modulation(cc, mod_w, mod_b)

    rope = _rope_tables(seq)
    chan_c, chan_s = _dft_tables(FFT_GROUP)
    seq_tabs = {seq: _dft_tables(seq), ctx_len: _dft_tables(ctx_len)}

    xs = x.reshape(bsz * seq, d)
    hc = ctx.reshape(bsz * ctx_len, d)

    for i in range(DEPTH):
        even = i % 2 == 0
        ctx_live = any(j % 2 == 0 for j in range(i + 1, DEPTH))
        gn = [norm_g[i][j][None, :] for j in range(4)]
        mx = [mods[i, :bsz, j * d:(j + 1) * d].reshape(bsz, 1, d) for j in range(6)]
        mc = [mods[i, bsz:bsz + 1, j * d:(j + 1) * d].reshape(1, 1, d) for j in range(6)]
        mcb = [jnp.broadcast_to(m, (bsz, 1, d)) for m in mc]
        ffn = (ffn_w_up[i].astype(BF16), ffn_conv[i], ffn_w_down[i].astype(BF16))

        if even:
            e = i // 2
            w_in = att_w_in[e].astype(BF16)
            w_out = att_w_out[e].astype(BF16)
            wa, wb = w_out[:Q_DIM], w_out[Q_DIM:]
            sink = att_sink[e][None, :]
            q, kt, v, sx = _in_even(xs, seq, gn[0], mx[1], mx[0], w_in, sconv_w[e], rope)
            if ctx_live:
                qc, kct, vc, sxc = _in_even(hc, ctx_len, gn[0], mcb[1], mcb[0], w_in, sconv_w[e], None)
            else:
                kct, vc = _in_kv(hc, ctx_len, gn[0], mcb[1], mcb[0], w_in[:, Q_DIM:Q_DIM + 2 * KV_DIM])
            vc3 = vc.reshape(bsz, ctx_len, -1)
            ax = _attention(sink, q.reshape(bsz, seq, -1), kt, v.reshape(bsz, seq, -1), kct, vc3, True)
            xs = _post(ax.reshape(bsz * seq, -1), sx, wa, wb, xs, seq, gn[1], mx[2], gn[2], mx[4], mx[3], mx[5],
                       gn[3], *ffn)
            if ctx_live:
                ac = _attention(sink, qc.reshape(bsz, ctx_len, -1), None, None, kct, vc3, False)
                hc = _post(ac.reshape(bsz * ctx_len, -1), sxc, wa, wb, hc, ctx_len, gn[1], mc[2], gn[2], mc[4],
                           mc[3], mc[5], gn[3], *ffn)
        else:
            o = i // 2
            w_in = mix_w_in[o].astype(BF16)
            w_out = mix_w_out[o].astype(BF16)
            wa, wb = w_out[:POOL_DIM], w_out[POOL_DIM:]
            wg = pool_w_grp[o].astype(BF16)
            ps = pool_scale[o][None, :]
            streams = [(xs, seq, mx, mx)]
            if ctx_live:
                streams.append((hc, ctx_len, mcb, mc))
            outs = []
            for h2, t, m_in, m in streams:
                yp, ab = _in_odd(h2, t, gn[0], m_in[1], m_in[0], w_in, chan_c, chan_s, wg, ps)
                ct, st = seq_tabs[t]
                yf = _seq_dft(ab.reshape(bsz, t, -1), ct, st)
                outs.append(_post(yp, yf.reshape(bsz * t, -1), wa, wb, h2, t, gn[1], m[2], gn[2], m[4], m[3], m[5],
                                  gn[3], *ffn))
            xs = outs[0]
            if ctx_live:
                hc = outs[1]
    return xs.reshape(bsz, seq, d)
```

```python
import functools

import numpy as np
import jax
import jax.numpy as jnp
from jax import lax
from jax.experimental import pallas as pl
from jax.experimental.pallas import tpu as pltpu

F32 = jnp.float32
BF16 = jnp.bfloat16

D_MODEL = 1024
DEPTH = 4
GRID_W = 64
EPS = 1e-6
N_Q_HEADS = 8
N_KV_HEADS = 2
HEAD_DIM = 64
WINDOW = 128
BLOCK = 128
ROPE_BASE = 10000.0
SCONV_DIM = 512
POOL_DIM = 512
N_POOL_GROUPS = 4
POOL_GROUP = POOL_DIM // N_POOL_GROUPS
POOL_WINDOWS = (2, 4, 8, 16)
FFT_DIM = 512
N_FFT_GROUPS = 4
FFT_GROUP = FFT_DIM // N_FFT_GROUPS
D_FF = 2816
Q_DIM = N_Q_HEADS * HEAD_DIM
KV_DIM = N_KV_HEADS * HEAD_DIM

LANES = 128
MASK_VALUE = -1e30
LOG2E = 1.4426950408889634
ATTN_QBLOCKS = 4
VMEM_LIMIT = 56 * 1024 * 1024
ROW_TILE = 512
HALO = 16
FFN_CHUNK = 256
POST_PIECES = 8

assert HALO >= max(POOL_WINDOWS) and all(w & (w - 1) == 0 for w in POOL_WINDOWS)


def _params(*sem):
    return pltpu.CompilerParams(dimension_semantics=sem, vmem_limit_bytes=VMEM_LIMIT)


def _rms(x, g):
    ms = jnp.mean(x * x, axis=-1, keepdims=True)
    return x * lax.rsqrt(ms + EPS) * g


def _dot(a, b):
    return jnp.dot(a, b, preferred_element_type=F32)


def _halo_specs(tm, rows, width, tile=lambda i: i):
    hb = tm // HALO
    return [
        pl.BlockSpec((HALO, width), lambda i: (jnp.maximum(tile(i) * hb - 1, 0), 0)),
        pl.BlockSpec((tm, width), lambda i: (tile(i), 0)),
        pl.BlockSpec((HALO, width), lambda i: (jnp.minimum((tile(i) + 1) * hb, rows // HALO - 1), 0)),
    ]


def _ext(prev_ref, main_ref, next_ref):
    return jnp.concatenate([prev_ref[...], main_ref[...], next_ref[...]], axis=0)


def _tile_pos(tm, seg):
    return (pl.program_id(0) * tm + lax.broadcasted_iota(jnp.int32, (tm, 1), 0)) % seg


def _seg_ext(x_ext, seg, tile=None):
    tm = x_ext.shape[0] - 2 * HALO
    i = pl.program_id(0) if tile is None else tile
    top_ok = (i * tm) % seg != 0
    bot_ok = ((i + 1) * tm) % seg != 0
    top, bot = x_ext[:HALO], x_ext[HALO + tm:]
    return jnp.concatenate([jnp.where(top_ok, top, jnp.zeros_like(top)), x_ext[HALO:HALO + tm],
                            jnp.where(bot_ok, bot, jnp.zeros_like(bot))], axis=0)


def _roll_rows(x_ext, off):
    rows = x_ext.shape[0]
    return pltpu.roll(x_ext, (-off) % rows, 0)


def _conv3(x_ext, seg, w, tile=None):
    tm = x_ext.shape[0] - 2 * HALO
    main = slice(HALO, HALO + tm)
    xm = _seg_ext(x_ext, seg, tile)
    return _roll_rows(xm, -1)[main] * w[0:1] + x_ext[main] * w[1:2] + _roll_rows(xm, 1)[main] * w[2:3]


def _zero_anchor(v):
    fold = sum(v[r:r + 8] for r in range(0, v.shape[0], 8))
    fold = sum(fold[:, l:l + LANES] for l in range(0, fold.shape[1], LANES))
    return jnp.minimum(jnp.abs(fold[0:1]), 0.0)


def _const_spec(a):
    return pl.BlockSpec(a.shape, lambda *_: (0,) * a.ndim, pipeline_mode=pl.Buffered(1))


def _mod_kernel(c_ref, w_ref, b_ref, o_ref):
    c = c_ref[...]
    s = (c / (1.0 + jnp.exp(-c))).astype(BF16)
    o_ref[0] = _dot(s, w_ref[0].astype(BF16)) + b_ref[0]


def _modulation(cc, mod_w, mod_b):
    depth, d, n = mod_w.shape
    tn = 1024
    return pl.pallas_call(
        _mod_kernel,
        grid=(depth, n // tn),
        in_specs=[
            pl.BlockSpec((16, d), lambda i, j: (0, 0)),
            pl.BlockSpec((1, d, tn), lambda i, j: (i, 0, j)),
            pl.BlockSpec((1, 1, tn), lambda i, j: (i, 0, j)),
        ],
        out_specs=pl.BlockSpec((1, 16, tn), lambda i, j: (i, 0, j)),
        out_shape=jax.ShapeDtypeStruct((depth, 16, n), F32),
        compiler_params=_params("arbitrary", "arbitrary"),
        name="modulation",
    )(cc, mod_w, mod_b.reshape(depth, 1, n))


def _rope128(x, cos, sin_a, sin_b):
    return x * cos + pltpu.roll(x, LANES - 16, 1) * sin_a + pltpu.roll(x, 16, 1) * sin_b


def _write_kv(k, v, k_ref, v_ref):
    lo = lax.broadcasted_iota(jnp.int32, (1, LANES), 1) < HEAD_DIM
    vr = pltpu.roll(v, HEAD_DIM, 1)
    zero = jnp.zeros_like(v)
    v_ref[...] = jnp.concatenate(
        [jnp.where(lo, v, zero), jnp.where(lo, zero, vr), jnp.where(lo, vr, zero), jnp.where(lo, zero, v)],
        axis=1).astype(BF16)
    kt = k.T
    top = lax.broadcasted_iota(jnp.int32, (LANES, 1), 0) < HEAD_DIM
    ktr = pltpu.roll(kt, HEAD_DIM, 0)
    zero = jnp.zeros_like(kt)
    k_ref[...] = jnp.concatenate(
        [jnp.where(top, kt, zero), jnp.where(top, zero, ktr), jnp.where(top, ktr, zero), jnp.where(top, zero, kt)],
        axis=0).astype(BF16)


def _in_even_kernel(xp_ref, x_ref, xn_ref, gn_ref, sc_ref, sh_ref, w_ref, cw_ref, *rest, rope, seg):
    if rope:
        cos_ref, sa_ref, sb_ref = rest[:3]
        rest = rest[3:]
    q_ref, k_ref, v_ref, s_ref = rest
    tm = x_ref.shape[0]
    hh = (_rms(_ext(xp_ref, x_ref, xn_ref), gn_ref[...]) * (1.0 + sc_ref[0]) + sh_ref[0]).astype(BF16)
    h = hh[HALO:HALO + tm]
    o = Q_DIM + 2 * KV_DIM
    cz = (_dot(hh, w_ref[:, o + SCONV_DIM:o + 2 * SCONV_DIM])
          * _dot(hh, w_ref[:, o + 2 * SCONV_DIM:o + 3 * SCONV_DIM]))
    q = _dot(h, w_ref[:, 0:Q_DIM])
    kv = _dot(h, w_ref[:, Q_DIM:Q_DIM + 2 * KV_DIM])
    k, v = kv[:, 0:KV_DIM], kv[:, KV_DIM:2 * KV_DIM]
    b = _dot(h, w_ref[:, o:o + SCONV_DIM])
    s_ref[...] = (b * _conv3(cz, seg, cw_ref[...])).astype(BF16)
    if rope:
        cos, sa, sb = cos_ref[...], sa_ref[...], sb_ref[...]
        k = _rope128(k, cos, sa, sb)
        q = jnp.concatenate(
            [_rope128(q[:, LANES * j:LANES * (j + 1)], cos, sa, sb) for j in range(Q_DIM // LANES)], axis=1)
    q_ref[...] = (q * (HEAD_DIM ** -0.5 * LOG2E)).astype(BF16)
    _write_kv(k, v, k_ref, v_ref)


def _in_even(x2, seg, gn, sc, sh, w, conv_w, tables):
    rows, d = x2.shape
    tm = min(ROW_TILE, seg)
    per_seg = seg // tm
    rope = tables is not None
    row = lambda i: (i, 0)
    mod = pl.BlockSpec((1, 1, d), lambda i: (i // per_seg, 0, 0))
    in_specs = _halo_specs(tm, rows, d) + [pl.BlockSpec((1, d), lambda i: (0, 0)), mod, mod,
                                           _const_spec(w), _const_spec(conv_w)]
    args = [x2, x2, x2, gn, sc, sh, w, conv_w]
    if rope:
        in_specs += [pl.BlockSpec((tm, LANES), lambda i: (i % per_seg, 0))] * 3
        args += list(tables)
    kv_w = 2 * KV_DIM * 2
    return pl.pallas_call(
        functools.partial(_in_even_kernel, rope=rope, seg=seg),
        grid=(rows // tm,),
        in_specs=in_specs,
        out_specs=[pl.BlockSpec((tm, Q_DIM), row), pl.BlockSpec((kv_w, tm), lambda i: (0, i)),
                   pl.BlockSpec((tm, kv_w), row), pl.BlockSpec((tm, SCONV_DIM), row)],
        out_shape=[jax.ShapeDtypeStruct((rows, Q_DIM), BF16), jax.ShapeDtypeStruct((kv_w, rows), BF16),
                   jax.ShapeDtypeStruct((rows, kv_w), BF16), jax.ShapeDtypeStruct((rows, SCONV_DIM), BF16)],
        compiler_params=_params("arbitrary"),
        name="in_even",
    )(*args)


def _in_kv_kernel(x_ref, gn_ref, sc_ref, sh_ref, w_ref, k_ref, v_ref):
    h = (_rms(x_ref[...], gn_ref[...]) * (1.0 + sc_ref[0]) + sh_ref[0]).astype(BF16)
    kv = _dot(h, w_ref[...])
    _write_kv(kv[:, 0:KV_DIM], kv[:, KV_DIM:2 * KV_DIM], k_ref, v_ref)


def _in_kv(x2, seg, gn, sc, sh, w):
    rows, d = x2.shape
    tm = min(ROW_TILE, seg)
    per_seg = seg // tm
    row = lambda i: (i, 0)
    mod = pl.BlockSpec((1, 1, d), lambda i: (i // per_seg, 0, 0))
    kv_w = 2 * KV_DIM * 2
    return pl.pallas_call(
        _in_kv_kernel,
        grid=(rows // tm,),
        in_specs=[pl.BlockSpec((tm, d), row), pl.BlockSpec((1, d), lambda i: (0, 0)), mod, mod, _const_spec(w)],
        out_specs=[pl.BlockSpec((kv_w, tm), lambda i: (0, i)), pl.BlockSpec((tm, kv_w), row)],
        out_shape=[jax.ShapeDtypeStruct((kv_w, rows), BF16), jax.ShapeDtypeStruct((rows, kv_w), BF16)],
        compiler_params=_params("arbitrary"),
        name="in_kv",
    )(x2, gn, sc, sh, w)


def _in_odd_kernel(xp_ref, x_ref, xn_ref, gn_ref, sc_ref, sh_ref, w_ref, ccs_ref, wg_ref, ps_ref,
                   p_ref, ab_ref, *, seg):
    tm = x_ref.shape[0]
    hh = (_rms(_ext(xp_ref, x_ref, xn_ref), gn_ref[...]) * (1.0 + sc_ref[0]) + sh_ref[0]).astype(BF16)
    h = hh[HALO:HALO + tm]
    up = _seg_ext(_dot(hh, w_ref[:, 0:POOL_DIM]), seg)
    uf = _dot(h, w_ref[:, POOL_DIM:POOL_DIM + FFT_DIM]).astype(BF16)
    ccs = ccs_ref[...]
    both = [_dot(uf[:, g * FFT_GROUP:(g + 1) * FFT_GROUP], ccs) for g in range(N_FFT_GROUPS)]
    ab_ref[...] = jnp.concatenate([r[:, 0:FFT_GROUP] for r in both] + [r[:, FFT_GROUP:] for r in both],
                                  axis=1).astype(BF16)
    pos = _tile_pos(tm, seg)
    pooled = []
    for gi, w in enumerate(POOL_WINDOWS):
        ug = up[:, gi * POOL_GROUP:(gi + 1) * POOL_GROUP]
        back, fwd = w // 2, w - w // 2
        run, span = ug, 1
        while span < w:
            run = run + _roll_rows(run, span)
            span *= 2
        total = _roll_rows(run, -back)[HALO:HALO + tm]
        count = (jnp.minimum(pos + fwd, seg) - jnp.maximum(pos - back, 0)).astype(F32)
        pooled.append((total / count - ug[HALO:HALO + tm]).astype(BF16))
    outs = [_dot(jnp.concatenate(pooled[2 * j:2 * j + 2], axis=1), wg_ref[j]) for j in range(N_POOL_GROUPS // 2)]
    p_ref[...] = (jnp.concatenate(outs, axis=1) * ps_ref[...]).astype(BF16)


def _in_odd(x2, seg, gn, sc, sh, w, ccs, wg, ps):
    rows, d = x2.shape
    tm = min(ROW_TILE, seg)
    per_seg = seg // tm
    row = lambda i: (i, 0)
    mod = pl.BlockSpec((1, 1, d), lambda i: (i // per_seg, 0, 0))
    return pl.pallas_call(
        functools.partial(_in_odd_kernel, seg=seg),
        grid=(rows // tm,),
        in_specs=_halo_specs(tm, rows, d) + [pl.BlockSpec((1, d), lambda i: (0, 0)), mod, mod, _const_spec(w),
                                             _const_spec(ccs), _const_spec(wg), _const_spec(ps)],
        out_specs=[pl.BlockSpec((tm, POOL_DIM), row), pl.BlockSpec((tm, 2 * FFT_DIM), row)],
        out_shape=[jax.ShapeDtypeStruct((rows, POOL_DIM), BF16), jax.ShapeDtypeStruct((rows, 2 * FFT_DIM), BF16)],
        compiler_params=_params("arbitrary"),
        name="in_odd",
    )(x2, x2, x2, gn, sc, sh, w, ccs, wg, ps)


def _attn_kernel(sink_ref, q_ref, *rest, window, qb, nblocks):
    if window:
        ktp, ktc, ktn, vp, vc, vn, kxt, vx, o_ref = rest
    else:
        kxt, vx, o_ref = rest
    n = pl.program_id(1)
    rows = 2 * BLOCK
    first_pair = lax.broadcasted_iota(jnp.int32, (rows, 1), 0) < BLOCK
    lo = lax.broadcasted_iota(jnp.int32, (1, LANES), 1) < HEAD_DIM
    if window:
        qi = lax.broadcasted_iota(jnp.int32, (rows, BLOCK), 0) % BLOCK
        j = lax.broadcasted_iota(jnp.int32, (rows, BLOCK), 1)
        tri_prev = jnp.where(j >= qi, 0.0, MASK_VALUE)
        tri_next = jnp.where(j <= qi, 0.0, MASK_VALUE)
        bias = []
        for t in range(qb):
            blk = n * qb + t
            bias.append((tri_prev + jnp.where(blk >= 1, 0.0, MASK_VALUE),
                         tri_next + jnp.where(blk <= nblocks - 2, 0.0, MASK_VALUE)))
    def lane_halves(h):
        c0 = 2 * LANES * h
        return slice(c0, c0 + LANES), slice(c0 + LANES, c0 + 2 * LANES)

    def keys_values(h, t):
        kparts, vparts = [], []
        for sl in lane_halves(h):
            if window:
                kall = jnp.concatenate([ktp[sl, :], ktc[sl, :], ktn[sl, :]], axis=1)
                vall = jnp.concatenate([vp[0, :, sl], vc[0, :, sl], vn[0, :, sl]], axis=0)
                kparts.append(kall[:, t * BLOCK:(t + 3) * BLOCK])
                vparts.append(vall[t * BLOCK:(t + 3) * BLOCK])
            kparts.append(kxt[sl, :])
            vparts.append(vx[0, :, sl])
        return jnp.concatenate(kparts, axis=1), jnp.concatenate(vparts, axis=0)

    def logits(h, t):
        rs = slice(t * BLOCK, (t + 1) * BLOCK)
        lo_sl, hi_sl = lane_halves(h)
        qp = jnp.concatenate([q_ref[0, rs, lo_sl], q_ref[0, rs, hi_sl]], axis=0)
        return _dot(qp, keys_values(h, t)[0])

    def finish(h, t, s):
        vst = keys_values(h, t)[1]
        nkeys = vst.shape[0] // 2
        probs, inv = [], []
        for half in range(2):
            tiles = [s[:, half * nkeys + c * LANES:half * nkeys + (c + 1) * LANES] for c in range(nkeys // LANES)]
            if window:
                tiles[0] = tiles[0] + bias[t][0]
                tiles[2] = tiles[2] + bias[t][1]
            head = 4 * h + half
            sink_col = jnp.where(first_pair, sink_ref[0, head], sink_ref[0, head + 2]) * LOG2E
            m = jnp.maximum(jnp.max(functools.reduce(jnp.maximum, tiles), axis=-1, keepdims=True), sink_col)
            es = [jnp.exp2(tl - m) for tl in tiles]
            den = jnp.sum(functools.reduce(jnp.add, es), axis=-1, keepdims=True) + jnp.exp2(sink_col - m)
            probs += [e.astype(BF16) for e in es]
            inv.append(1.0 / den)
        o = _dot(jnp.concatenate(probs, axis=1), vst) * jnp.where(lo, inv[0], inv[1])
        rs = slice(t * BLOCK, (t + 1) * BLOCK)
        lo_sl, hi_sl = lane_halves(h)
        o_ref[0, rs, lo_sl] = o[:BLOCK].astype(o_ref.dtype)
        o_ref[0, rs, hi_sl] = o[BLOCK:].astype(o_ref.dtype)

    units = [(h, t) for h in range(N_KV_HEADS) for t in range(qb)]
    s_next = logits(*units[0])
    for u, unit in enumerate(units):
        s_cur = s_next
        if u + 1 < len(units):
            s_next = logits(*units[u + 1])
        finish(*unit, s_cur)


def _attention(sink, q, kt, v, kxt, vx, window):
    b, t, _ = q.shape
    l = vx.shape[1]
    nb = t // BLOCK
    qb = ATTN_QBLOCKS if window else nb
    w = vx.shape[2]
    steps = nb // qb
    cur = lambda i, n: (i, n, 0)
    in_specs = [pl.BlockSpec(memory_space=pltpu.SMEM), pl.BlockSpec((1, qb * BLOCK, Q_DIM), cur)]
    args = [sink, q]
    if window:
        pb = lambda n: jnp.maximum(n * qb - 1, 0)
        nx = lambda n: jnp.minimum(n * qb + qb, nb - 1)
        in_specs += [
            pl.BlockSpec((w, BLOCK), lambda i, n: (0, i * nb + pb(n))),
            pl.BlockSpec((w, qb * BLOCK), lambda i, n: (0, i * steps + n)),
            pl.BlockSpec((w, BLOCK), lambda i, n: (0, i * nb + nx(n))),
            pl.BlockSpec((1, BLOCK, w), lambda i, n: (i, pb(n), 0)),
            pl.BlockSpec((1, qb * BLOCK, w), cur),
            pl.BlockSpec((1, BLOCK, w), lambda i, n: (i, nx(n), 0)),
        ]
        args += [kt, kt, kt, v, v, v]
    in_specs += [pl.BlockSpec((w, l), lambda i, n: (0, i)), pl.BlockSpec((1, l, w), lambda i, n: (i, 0, 0))]
    args += [kxt, vx]
    return pl.pallas_call(
        functools.partial(_attn_kernel, window=window, qb=qb, nblocks=nb),
        grid=(b, steps),
        in_specs=in_specs,
        out_specs=pl.BlockSpec((1, qb * BLOCK, Q_DIM), cur),
        out_shape=jax.ShapeDtypeStruct((b, t, Q_DIM), BF16),
        compiler_params=_params("arbitrary", "arbitrary"),
        name="attention",
    )(*args)


def _seq_dft_kernel(ct_ref, st_ref, flip_ref, ab_ref, o_ref, *, norm):
    t = ab_ref.shape[1]
    half = t // 2
    a = ab_ref[0, :, 0:FFT_DIM]
    b = ab_ref[0, :, FFT_DIM:2 * FFT_DIM]
    p = _dot(ct_ref[...], a)
    q = _dot(st_ref[...], b)
    o_ref[0, 0:half] = ((p - q) * norm).astype(o_ref.dtype)
    sign = 1.0 - 2.0 * (lax.broadcasted_iota(jnp.int32, (t, 1), 0) % 2).astype(F32)
    nyquist = jnp.sum(a.astype(F32) * sign, axis=0, keepdims=True)
    first = lax.broadcasted_iota(jnp.int32, (half, 1), 0) == 0
    mirrored = (jnp.where(first, nyquist, p + q) * norm).astype(BF16)
    o_ref[0, half:t] = _dot(flip_ref[...], mirrored).astype(o_ref.dtype)


def _seq_dft(ab3, tables):
    b, t, _ = ab3.shape
    return pl.pallas_call(
        functools.partial(_seq_dft_kernel, norm=float((t * FFT_GROUP) ** -0.5)),
        grid=(b,),
        in_specs=[_const_spec(a) for a in tables] + [pl.BlockSpec((1, t, 2 * FFT_DIM), lambda i: (i, 0, 0))],
        out_specs=pl.BlockSpec((1, t, FFT_DIM), lambda i: (i, 0, 0)),
        out_shape=jax.ShapeDtypeStruct((b, t, FFT_DIM), BF16),
        compiler_params=_params("arbitrary"),
        name="seq_dft",
    )(*tables, ab3)


def _post_kernel(ap_ref, a_ref, an_ref, bp_ref, b_ref, bn_ref, xp_ref, x_ref, xn_ref,
                 wa_ref, wb_ref, gn1_ref, g1_ref, gn2_ref, sc_ref, sh_ref, g2_ref, gn3_ref,
                 wu_ref, cw_ref, wd_ref, o_ref, y_scr, hh_scr, xm_scr, act_scr, *, seg):
    i = pl.program_id(0)
    tm = x_ref.shape[0]
    main = slice(HALO, HALO + tm)
    rd = i % 2
    wr = 1 - rd
    piece_rows = tm // POST_PIECES
    pieces = ([(0, xp_ref, 0, HALO)]
              + [(HALO + k * piece_rows, x_ref, k * piece_rows, piece_rows) for k in range(POST_PIECES)]
              + [(HALO + tm, xn_ref, 0, HALO)])

    def prepare(piece):
        lo, ref, off, nrows = piece
        xm = ref[off:off + nrows] + g1_ref[0] * _rms(y_scr[lo:lo + nrows], gn1_ref[...])
        hn = _rms(xm, gn2_ref[...]) * (1.0 + sc_ref[0]) + sh_ref[0]
        hh_scr[wr, lo:lo + nrows] = hn.astype(BF16)
        if ref is x_ref:
            xm_scr[wr, off:off + nrows] = xm
        return jnp.concatenate([_zero_anchor(hn)] * (FFN_CHUNK // LANES), axis=1)

    def out_projection():
        y_scr[...] = (_dot(_ext(ap_ref, a_ref, an_ref), wa_ref[...])
                      + _dot(_ext(bp_ref, b_ref, bn_ref), wb_ref[...]))

    @pl.when(i == 0)
    def _():
        y_scr[...] = jnp.zeros_like(y_scr)

    @pl.when(i < 2)
    def _():
        for piece in pieces:
            prepare(piece)
        out_projection()

    @pl.when(i >= 2)
    def _():
        hh = hh_scr[rd]
        h = hh[main]
        cw = cw_ref[...]
        anchor = None
        for c in range(D_FF // FFN_CHUNK):
            cs = slice(c * FFN_CHUNK, (c + 1) * FFN_CHUNK)
            cwc = cw[:, cs] if anchor is None else cw[:, cs] + anchor
            conv = _conv3(_dot(hh, wu_ref[:, cs]), seg, cwc, tile=i - 2)
            val = _dot(h, wu_ref[:, D_FF + c * FFN_CHUNK:D_FF + (c + 1) * FFN_CHUNK])
            act_scr[:, cs] = (conv / (1.0 + jnp.exp(-conv)) * val).astype(BF16)
            anchor = prepare(pieces[c]) if c < len(pieces) else None
        y2 = _dot(act_scr[...], wd_ref[...])
        out_projection()
        o_ref[...] = xm_scr[rd] + g2_ref[0] * _rms(y2, gn3_ref[...])


def _post(a2, b2, wa, wb, x2, seg, gn1, g1, gn2, sc, sh, g2, gn3, w_up, conv_w, w_down):
    rows, d = x2.shape
    tm = min(ROW_TILE, seg)
    per_seg = seg // tm
    n = rows // tm
    stage_a = lambda i: jnp.minimum(i, n - 1)
    stage_b = lambda i: jnp.clip(i - 1, 0, n - 1)
    stage_c = lambda i: jnp.maximum(i - 2, 0)
    per_sample = g1.shape[0] > 1
    mod = lambda tile: pl.BlockSpec((1, 1, d), (lambda i: (tile(i) // per_seg, 0, 0)) if per_sample
                                    else (lambda i: (0, 0, 0)))
    vec = pl.BlockSpec((1, d), lambda i: (0, 0))
    return pl.pallas_call(
        functools.partial(_post_kernel, seg=seg),
        grid=(n + 2,),
        in_specs=(_halo_specs(tm, rows, a2.shape[1], stage_a) + _halo_specs(tm, rows, b2.shape[1], stage_a)
                  + _halo_specs(tm, rows, d, stage_b)
                  + [_const_spec(wa), _const_spec(wb), vec, mod(stage_b), vec, mod(stage_b), mod(stage_b),
                     mod(stage_c), vec, _const_spec(w_up), _const_spec(conv_w), _const_spec(w_down)]),
        out_specs=pl.BlockSpec((tm, d), lambda i: (stage_c(i), 0)),
        out_shape=jax.ShapeDtypeStruct((rows, d), F32),
        scratch_shapes=[pltpu.VMEM((tm + 2 * HALO, d), F32), pltpu.VMEM((2, tm + 2 * HALO, d), BF16),
                        pltpu.VMEM((2, tm, d), F32), pltpu.VMEM((tm, D_FF), BF16)],
        compiler_params=_params("arbitrary"),
        name="post",
    )(a2, a2, a2, b2, b2, b2, x2, x2, x2, wa, wb, gn1, g1, gn2, sc, sh, g2, gn3, w_up, conv_w, w_down)


def _rope_tables(seq):
    half = HEAD_DIM // 2
    nf = half // 2
    inv = ROPE_BASE ** (-np.arange(nf, dtype=np.float64) / nf)
    t = np.arange(seq)
    lane = np.arange(LANES)
    in_head = lane % HEAD_DIM
    pos = np.where((in_head // half)[None, :] == 0, (t // GRID_W)[:, None], (t % GRID_W)[:, None])
    ang = pos * inv[lane % nf][None, :]
    first = ((lane % half) < nf)[None, :]
    cos = np.cos(ang)
    sin_a = np.where(first, -np.sin(ang), 0.0)
    sin_b = np.where(first, 0.0, np.sin(ang))
    return tuple(jnp.asarray(a, F32) for a in (cos, sin_a, sin_b))


def _dft_tables(n, rows=None):
    k = np.arange(n if rows is None else rows)
    ang = 2.0 * np.pi * ((k[:, None] * np.arange(n)[None, :]) % n) / n
    return jnp.asarray(np.cos(ang), F32).astype(BF16), jnp.asarray(np.sin(ang), F32).astype(BF16)


def _seq_dft_tables(n):
    half = n // 2
    flip = np.zeros((half, half), np.float32)
    flip[np.arange(half), (half - np.arange(half)) % half] = 1.0
    return _dft_tables(n, half) + (jnp.asarray(flip).astype(BF16),)


def kernel(x, c, ctx, c_ctx, mod_w, mod_b, norm_g, att_w_in, att_sink, sconv_w, att_w_out, mix_w_in,
           pool_w_grp, pool_scale, mix_w_out, ffn_w_up, ffn_conv, ffn_w_down):
    bsz, seq, d = x.shape
    ctx_len = ctx.shape[1]
    assert d == D_MODEL and seq % ROW_TILE == 0 and (bsz * ctx_len) % ROW_TILE == 0 and ROW_TILE % ctx_len == 0

    cc = jnp.concatenate([c, c_ctx[None, :], jnp.zeros((16 - bsz - 1, d), F32)], axis=0)
    mods = _modulation(cc, mod_w, mod_b)

    rope = _rope_tables(seq)
    chan_cs = jnp.concatenate(_dft_tables(FFT_GROUP), axis=1)
    seq_tabs = {seq: _seq_dft_tables(seq), ctx_len: _seq_dft_tables(ctx_len)}

    xs = x.reshape(bsz * seq, d)
    hc = ctx.reshape(bsz * ctx_len, d)

    for i in range(DEPTH):
        even = i % 2 == 0
        ctx_live = any(j % 2 == 0 for j in range(i + 1, DEPTH))
        gn = [norm_g[i][j][None, :] for j in range(4)]
        mx = [mods[i, :bsz, j * d:(j + 1) * d].reshape(bsz, 1, d) for j in range(6)]
        mc = [mods[i, bsz:bsz + 1, j * d:(j + 1) * d].reshape(1, 1, d) for j in range(6)]
        mcb = [jnp.broadcast_to(m, (bsz, 1, d)) for m in mc]
        ffn = (ffn_w_up[i].astype(BF16), ffn_conv[i], ffn_w_down[i].astype(BF16))

        if even:
            e = i // 2
            w_in = att_w_in[e].astype(BF16)
            w_out = att_w_out[e].astype(BF16)
            wa, wb = w_out[:Q_DIM], w_out[Q_DIM:]
            sink = att_sink[e][None, :]
            q, kt, v, sx = _in_even(xs, seq, gn[0], mx[1], mx[0], w_in, sconv_w[e], rope)
            if ctx_live:
                qc, kct, vc, sxc = _in_even(hc, ctx_len, gn[0], mcb[1], mcb[0], w_in, sconv_w[e], None)
            else:
                kct, vc = _in_kv(hc, ctx_len, gn[0], mcb[1], mcb[0], w_in[:, Q_DIM:Q_DIM + 2 * KV_DIM])
            vc3 = vc.reshape(bsz, ctx_len, -1)
            ax = _attention(sink, q.reshape(bsz, seq, -1), kt, v.reshape(bsz, seq, -1), kct, vc3, True)
            xs = _post(ax.reshape(bsz * seq, -1), sx, wa, wb, xs, seq, gn[1], mx[2], gn[2], mx[4], mx[3], mx[5],
                       gn[3], *ffn)
            if ctx_live:
                ac = _attention(sink, qc.reshape(bsz, ctx_len, -1), None, None, kct, vc3, False)
                hc = _post(ac.reshape(bsz * ctx_len, -1), sxc, wa, wb, hc, ctx_len, gn[1], mc[2], gn[2], mc[4],
                           mc[3], mc[5], gn[3], *ffn)
        else:
            o = i // 2
            w_in = mix_w_in[o].astype(BF16)
            w_out = mix_w_out[o].astype(BF16)
            wa, wb = w_out[:POOL_DIM], w_out[POOL_DIM:]
            g4 = pool_w_grp[o].astype(BF16)
            zero = jnp.zeros_like(g4[0])
            wg = jnp.stack([jnp.block([[g4[2 * j], zero], [zero, g4[2 * j + 1]]]) for j in range(N_POOL_GROUPS // 2)])
            ps = pool_scale[o][None, :]
            streams = [(xs, seq, mx, mx)]
            if ctx_live:
                streams.append((hc, ctx_len, mcb, mc))
            outs = []
            for h2, t, m_in, m in streams:
                yp, ab = _in_odd(h2, t, gn[0], m_in[1], m_in[0], w_in, chan_cs, wg, ps)
                yf = _seq_dft(ab.reshape(bsz, t, -1), seq_tabs[t])
                outs.append(_post(yp, yf.reshape(bsz * t, -1), wa, wb, h2, t, gn[1], m[2], gn[2], m[4], m[3], m[5],
                                  gn[3], *ffn))
            xs = outs[0]
            if ctx_live:
                hc = outs[1]
    return xs.reshape(bsz, seq, d)
```

```python
import functools
from typing import NamedTuple

import numpy as np
import jax
import jax.numpy as jnp
from jax import lax
from jax.experimental import pallas as pl
from jax.experimental.pallas import tpu as pltpu

F32 = jnp.float32
BF16 = jnp.bfloat16

D_MODEL = 1024
DEPTH = 4
GRID_W = 64
EPS = 1e-6
N_Q_HEADS = 8
N_KV_HEADS = 2
HEAD_DIM = 64
WINDOW = 128
BLOCK = 128
ROPE_BASE = 10000.0
SCONV_DIM = 512
POOL_DIM = 512
N_POOL_GROUPS = 4
POOL_GROUP = POOL_DIM // N_POOL_GROUPS
POOL_WINDOWS = (2, 4, 8, 16)
FFT_DIM = 512
N_FFT_GROUPS = 4
FFT_GROUP = FFT_DIM // N_FFT_GROUPS
D_FF = 2816
Q_DIM = N_Q_HEADS * HEAD_DIM
KV_DIM = N_KV_HEADS * HEAD_DIM
KV_TILES = 2 * KV_DIM * 2

LANES = 128
MASK_VALUE = -1e30
LOG2E = 1.4426950408889634
ATTN_QBLOCKS = 4
VMEM_LIMIT = 56 * 1024 * 1024
ROW_TILE = 512
HALO = 16
FFN_CHUNK = 256
POST_PIECES = 8
MOD_ROWS = 16
N_GAINS = 4

assert HALO >= max(POOL_WINDOWS) and all(w & (w - 1) == 0 for w in POOL_WINDOWS)


def _params(*sem):
    return pltpu.CompilerParams(dimension_semantics=sem, vmem_limit_bytes=VMEM_LIMIT)


def _rms(x, g):
    ms = jnp.mean(x * x, axis=-1, keepdims=True)
    return x * lax.rsqrt(ms + EPS) * g


def _dot(a, b):
    return jnp.dot(a, b, preferred_element_type=F32)


class _Stream(NamedTuple):
    x: jax.Array
    seg: int
    ctx_row: int | None

    def tile(self):
        return min(ROW_TILE, self.seg)

    def mod_row(self, tm):
        if self.ctx_row is not None:
            row = self.ctx_row
            return lambda tile: row
        per_seg = self.seg // tm
        return lambda tile: tile // per_seg


class _Tables(NamedTuple):
    mods: jax.Array
    gains: jax.Array


def _layer_spec(a, layer):
    nd = a.ndim
    return pl.BlockSpec((1,) + a.shape[1:], lambda *_: (layer,) + (0,) * (nd - 1), pipeline_mode=pl.Buffered(1))


def _const_spec(a):
    return pl.BlockSpec(a.shape, lambda *_: (0,) * a.ndim, pipeline_mode=pl.Buffered(1))


def _gain_spec(layer, j):
    return pl.BlockSpec((1, 1, D_MODEL), lambda *_: (layer * N_GAINS + j, 0, 0))


def _mod_spec(layer, chunk, row, tile=lambda i: i):
    return pl.BlockSpec((1, 1, D_MODEL), lambda i: (layer * MOD_ROWS + row(tile(i)), 0, chunk))


def _halo_specs(tm, rows, width, tile=lambda i: i):
    hb = tm // HALO
    return [
        pl.BlockSpec((HALO, width), lambda i: (jnp.maximum(tile(i) * hb - 1, 0), 0)),
        pl.BlockSpec((tm, width), lambda i: (tile(i), 0)),
        pl.BlockSpec((HALO, width), lambda i: (jnp.minimum((tile(i) + 1) * hb, rows // HALO - 1), 0)),
    ]


def _ext(prev_ref, main_ref, next_ref):
    return jnp.concatenate([prev_ref[...], main_ref[...], next_ref[...]], axis=0)


def _tile_pos(tm, seg):
    return (pl.program_id(0) * tm + lax.broadcasted_iota(jnp.int32, (tm, 1), 0)) % seg


def _seg_ext(x_ext, seg, tile=None):
    tm = x_ext.shape[0] - 2 * HALO
    i = pl.program_id(0) if tile is None else tile
    top_ok = (i * tm) % seg != 0
    bot_ok = ((i + 1) * tm) % seg != 0
    top, bot = x_ext[:HALO], x_ext[HALO + tm:]
    return jnp.concatenate([jnp.where(top_ok, top, jnp.zeros_like(top)), x_ext[HALO:HALO + tm],
                            jnp.where(bot_ok, bot, jnp.zeros_like(bot))], axis=0)


def _roll_rows(x_ext, off):
    rows = x_ext.shape[0]
    return pltpu.roll(x_ext, (-off) % rows, 0)


def _conv3(x_ext, seg, w, tile=None):
    tm = x_ext.shape[0] - 2 * HALO
    main = slice(HALO, HALO + tm)
    xm = _seg_ext(x_ext, seg, tile)
    prev, nxt = _roll_rows(xm, -1)[main], _roll_rows(xm, 1)[main]
    if seg < tm:
        pos = lax.broadcasted_iota(jnp.int32, (tm, 1), 0) % seg
        prev = jnp.where(pos == 0, 0.0, prev)
        nxt = jnp.where(pos == seg - 1, 0.0, nxt)
    return prev * w[0:1] + x_ext[main] * w[1:2] + nxt * w[2:3]


def _zero_anchor(v):
    fold = sum(v[r:r + 8] for r in range(0, v.shape[0], 8))
    fold = sum(fold[:, l:l + LANES] for l in range(0, fold.shape[1], LANES))
    return jnp.minimum(jnp.abs(fold[0:1]), 0.0)


def _modulated(x, gn_ref, sc_ref, sh_ref):
    return (_rms(x, gn_ref[0]) * (1.0 + sc_ref[0]) + sh_ref[0]).astype(BF16)


def _stream_specs(stream, tabs, layer):
    tm = stream.tile()
    rows, d = stream.x.shape
    row = stream.mod_row(tm)
    specs = _halo_specs(tm, rows, d) + [_gain_spec(layer, 0), _mod_spec(layer, 1, row), _mod_spec(layer, 0, row)]
    return specs, [stream.x] * 3 + [tabs.gains, tabs.mods, tabs.mods]


def _mod_kernel(c_ref, w_ref, b_ref, o_ref):
    c = c_ref[...]
    s = (c / (1.0 + jnp.exp(-c))).astype(BF16)
    o_ref[0] = _dot(s, w_ref[0].astype(BF16)) + b_ref[0]


def _modulation(cc, mod_w, mod_b):
    depth, d, n = mod_w.shape
    tn = 1024
    return pl.pallas_call(
        _mod_kernel,
        grid=(depth, n // tn),
        in_specs=[
            pl.BlockSpec((MOD_ROWS, d), lambda i, j: (0, 0)),
            pl.BlockSpec((1, d, tn), lambda i, j: (i, 0, j)),
            pl.BlockSpec((1, 1, tn), lambda i, j: (i, 0, j)),
        ],
        out_specs=pl.BlockSpec((1, MOD_ROWS, tn), lambda i, j: (i, 0, j)),
        out_shape=jax.ShapeDtypeStruct((depth, MOD_ROWS, n), F32),
        compiler_params=_params("arbitrary", "arbitrary"),
        name="modulation",
    )(cc, mod_w, mod_b.reshape(depth, 1, n))


def _rope128(x, cos, sin_a, sin_b):
    return x * cos + pltpu.roll(x, LANES - 16, 1) * sin_a + pltpu.roll(x, 16, 1) * sin_b


def _write_kv(k, v, k_ref, v_ref):
    lo = lax.broadcasted_iota(jnp.int32, (1, LANES), 1) < HEAD_DIM
    vr = pltpu.roll(v, HEAD_DIM, 1)
    zero = jnp.zeros_like(v)
    v_ref[...] = jnp.concatenate(
        [jnp.where(lo, v, zero), jnp.where(lo, zero, vr), jnp.where(lo, vr, zero), jnp.where(lo, zero, v)],
        axis=1).astype(BF16)
    kt = k.T
    top = lax.broadcasted_iota(jnp.int32, (LANES, 1), 0) < HEAD_DIM
    ktr = pltpu.roll(kt, HEAD_DIM, 0)
    zero = jnp.zeros_like(kt)
    k_ref[...] = jnp.concatenate(
        [jnp.where(top, kt, zero), jnp.where(top, zero, ktr), jnp.where(top, ktr, zero), jnp.where(top, zero, kt)],
        axis=0).astype(BF16)


def _in_even_kernel(xp_ref, x_ref, xn_ref, gn_ref, sc_ref, sh_ref, w_ref, cw_ref, *rest, rope, seg):
    if rope:
        cos_ref, sa_ref, sb_ref = rest[:3]
        rest = rest[3:]
    q_ref, k_ref, v_ref, s_ref = rest
    tm = x_ref.shape[0]
    hh = _modulated(_ext(xp_ref, x_ref, xn_ref), gn_ref, sc_ref, sh_ref)
    h = hh[HALO:HALO + tm]
    o = Q_DIM + 2 * KV_DIM
    cz = (_dot(hh, w_ref[0, :, o + SCONV_DIM:o + 2 * SCONV_DIM])
          * _dot(hh, w_ref[0, :, o + 2 * SCONV_DIM:o + 3 * SCONV_DIM]))
    q = _dot(h, w_ref[0, :, 0:Q_DIM])
    kv = _dot(h, w_ref[0, :, Q_DIM:Q_DIM + 2 * KV_DIM])
    k, v = kv[:, 0:KV_DIM], kv[:, KV_DIM:2 * KV_DIM]
    b = _dot(h, w_ref[0, :, o:o + SCONV_DIM])
    s_ref[...] = (b * _conv3(cz, seg, cw_ref[0])).astype(BF16)
    if rope:
        cos, sa, sb = cos_ref[...], sa_ref[...], sb_ref[...]
        k = _rope128(k, cos, sa, sb)
        q = jnp.concatenate(
            [_rope128(q[:, LANES * j:LANES * (j + 1)], cos, sa, sb) for j in range(Q_DIM // LANES)], axis=1)
    q_ref[...] = (q * (HEAD_DIM ** -0.5 * LOG2E)).astype(BF16)
    _write_kv(k, v, k_ref, v_ref)


def _in_even(stream, tabs, layer, e, w_in, conv_w, rope_tables):
    rows = stream.x.shape[0]
    tm = stream.tile()
    per_seg = stream.seg // tm
    rope = rope_tables is not None
    row = lambda i: (i, 0)
    in_specs, args = _stream_specs(stream, tabs, layer)
    in_specs += [_layer_spec(w_in, e), _layer_spec(conv_w, e)]
    args += [w_in, conv_w]
    if rope:
        in_specs += [pl.BlockSpec((tm, LANES), lambda i: (i % per_seg, 0))] * 3
        args += list(rope_tables)
    return pl.pallas_call(
        functools.partial(_in_even_kernel, rope=rope, seg=stream.seg),
        grid=(rows // tm,),
        in_specs=in_specs,
        out_specs=[pl.BlockSpec((tm, Q_DIM), row), pl.BlockSpec((KV_TILES, tm), lambda i: (0, i)),
                   pl.BlockSpec((tm, KV_TILES), row), pl.BlockSpec((tm, SCONV_DIM), row)],
        out_shape=[jax.ShapeDtypeStruct((rows, Q_DIM), BF16), jax.ShapeDtypeStruct((KV_TILES, rows), BF16),
                   jax.ShapeDtypeStruct((rows, KV_TILES), BF16), jax.ShapeDtypeStruct((rows, SCONV_DIM), BF16)],
        compiler_params=_params("arbitrary"),
        name="in_even",
    )(*args)


def _in_kv_kernel(x_ref, gn_ref, sc_ref, sh_ref, w_ref, k_ref, v_ref):
    h = _modulated(x_ref[...], gn_ref, sc_ref, sh_ref)
    kv = _dot(h, w_ref[0, :, Q_DIM:Q_DIM + 2 * KV_DIM])
    _write_kv(kv[:, 0:KV_DIM], kv[:, KV_DIM:2 * KV_DIM], k_ref, v_ref)


def _in_kv(stream, tabs, layer, e, w_in):
    rows, d = stream.x.shape
    tm = stream.tile()
    row = lambda i: (i, 0)
    mrow = stream.mod_row(tm)
    return pl.pallas_call(
        _in_kv_kernel,
        grid=(rows // tm,),
        in_specs=[pl.BlockSpec((tm, d), row), _gain_spec(layer, 0), _mod_spec(layer, 1, mrow),
                  _mod_spec(layer, 0, mrow), _layer_spec(w_in, e)],
        out_specs=[pl.BlockSpec((KV_TILES, tm), lambda i: (0, i)), pl.BlockSpec((tm, KV_TILES), row)],
        out_shape=[jax.ShapeDtypeStruct((KV_TILES, rows), BF16), jax.ShapeDtypeStruct((rows, KV_TILES), BF16)],
        compiler_params=_params("arbitrary"),
        name="in_kv",
    )(stream.x, tabs.gains, tabs.mods, tabs.mods, w_in)


def _in_odd_kernel(xp_ref, x_ref, xn_ref, gn_ref, sc_ref, sh_ref, w_ref, ccs_ref, wg_ref, ps_ref,
                   p_ref, ab_ref, *, seg):
    tm = x_ref.shape[0]
    hh = _modulated(_ext(xp_ref, x_ref, xn_ref), gn_ref, sc_ref, sh_ref)
    h = hh[HALO:HALO + tm]
    up = _seg_ext(_dot(hh, w_ref[0, :, 0:POOL_DIM]), seg)
    uf = _dot(h, w_ref[0, :, POOL_DIM:POOL_DIM + FFT_DIM]).astype(BF16)
    ccs = ccs_ref[...]
    both = [_dot(uf[:, g * FFT_GROUP:(g + 1) * FFT_GROUP], ccs) for g in range(N_FFT_GROUPS)]
    ab_ref[...] = jnp.concatenate([r[:, 0:FFT_GROUP] for r in both] + [r[:, FFT_GROUP:] for r in both],
                                  axis=1).astype(BF16)
    pos = _tile_pos(tm, seg)
    pooled = []
    for gi, w in enumerate(POOL_WINDOWS):
        ug = up[:, gi * POOL_GROUP:(gi + 1) * POOL_GROUP]
        back, fwd = w // 2, w - w // 2
        run, span = ug, 1
        while span < w:
            run = run + _roll_rows(run, span)
            span *= 2
        total = _roll_rows(run, -back)[HALO:HALO + tm]
        count = (jnp.minimum(pos + fwd, seg) - jnp.maximum(pos - back, 0)).astype(F32)
        pooled.append((total / count - ug[HALO:HALO + tm]).astype(BF16))
    outs = [_dot(jnp.concatenate(pooled[2 * j:2 * j + 2], axis=1), wg_ref[0, j]) for j in range(N_POOL_GROUPS // 2)]
    p_ref[...] = (jnp.concatenate(outs, axis=1) * ps_ref[0]).astype(BF16)


def _in_odd(stream, tabs, layer, o, w_in, ccs, wg, ps):
    rows = stream.x.shape[0]
    tm = stream.tile()
    row = lambda i: (i, 0)
    in_specs, args = _stream_specs(stream, tabs, layer)
    return pl.pallas_call(
        functools.partial(_in_odd_kernel, seg=stream.seg),
        grid=(rows // tm,),
        in_specs=in_specs + [_layer_spec(w_in, o), _const_spec(ccs), _layer_spec(wg, o), _layer_spec(ps, o)],
        out_specs=[pl.BlockSpec((tm, POOL_DIM), row), pl.BlockSpec((tm, 2 * FFT_DIM), row)],
        out_shape=[jax.ShapeDtypeStruct((rows, POOL_DIM), BF16), jax.ShapeDtypeStruct((rows, 2 * FFT_DIM), BF16)],
        compiler_params=_params("arbitrary"),
        name="in_odd",
    )(*args, w_in, ccs, wg, ps)


def _attn_kernel(sink_ref, q_ref, *rest, window, qb, nblocks, e):
    if window:
        ktp, ktc, ktn, vp, vc, vn, kxt, vx, o_ref = rest
    else:
        kxt, vx, o_ref = rest
    n = pl.program_id(1)
    rows = 2 * BLOCK
    first_pair = lax.broadcasted_iota(jnp.int32, (rows, 1), 0) < BLOCK
    lo = lax.broadcasted_iota(jnp.int32, (1, LANES), 1) < HEAD_DIM
    if window:
        qi = lax.broadcasted_iota(jnp.int32, (rows, BLOCK), 0) % BLOCK
        j = lax.broadcasted_iota(jnp.int32, (rows, BLOCK), 1)
        tri_prev = jnp.where(j >= qi, 0.0, MASK_VALUE)
        tri_next = jnp.where(j <= qi, 0.0, MASK_VALUE)
        bias = []
        for t in range(qb):
            blk = n * qb + t
            bias.append((tri_prev + jnp.where(blk >= 1, 0.0, MASK_VALUE),
                         tri_next + jnp.where(blk <= nblocks - 2, 0.0, MASK_VALUE)))

    def lane_halves(h):
        c0 = 2 * LANES * h
        return slice(c0, c0 + LANES), slice(c0 + LANES, c0 + 2 * LANES)

    def keys_values(h, t):
        kparts, vparts = [], []
        for sl in lane_halves(h):
            if window:
                kall = jnp.concatenate([ktp[sl, :], ktc[sl, :], ktn[sl, :]], axis=1)
                vall = jnp.concatenate([vp[0, :, sl], vc[0, :, sl], vn[0, :, sl]], axis=0)
                kparts.append(kall[:, t * BLOCK:(t + 3) * BLOCK])
                vparts.append(vall[t * BLOCK:(t + 3) * BLOCK])
            kparts.append(kxt[sl, :])
            vparts.append(vx[0, :, sl])
        return jnp.concatenate(kparts, axis=1), jnp.concatenate(vparts, axis=0)

    def logits(h, t):
        rs = slice(t * BLOCK, (t + 1) * BLOCK)
        lo_sl, hi_sl = lane_halves(h)
        qp = jnp.concatenate([q_ref[0, rs, lo_sl], q_ref[0, rs, hi_sl]], axis=0)
        return _dot(qp, keys_values(h, t)[0])

    def finish(h, t, s):
        vst = keys_values(h, t)[1]
        nkeys = vst.shape[0] // 2
        probs, inv = [], []
        for half in range(2):
            tiles = [s[:, half * nkeys + c * LANES:half * nkeys + (c + 1) * LANES] for c in range(nkeys // LANES)]
            if window:
                tiles[0] = tiles[0] + bias[t][0]
                tiles[2] = tiles[2] + bias[t][1]
            head = 4 * h + half
            sink_col = jnp.where(first_pair, sink_ref[e, head], sink_ref[e, head + 2]) * LOG2E
            m = jnp.maximum(jnp.max(functools.reduce(jnp.maximum, tiles), axis=-1, keepdims=True), sink_col)
            es = [jnp.exp2(tl - m) for tl in tiles]
            den = jnp.sum(functools.reduce(jnp.add, es), axis=-1, keepdims=True) + jnp.exp2(sink_col - m)
            probs += [e_.astype(BF16) for e_ in es]
            inv.append(1.0 / den)
        o = _dot(jnp.concatenate(probs, axis=1), vst) * jnp.where(lo, inv[0], inv[1])
        rs = slice(t * BLOCK, (t + 1) * BLOCK)
        lo_sl, hi_sl = lane_halves(h)
        o_ref[0, rs, lo_sl] = o[:BLOCK].astype(o_ref.dtype)
        o_ref[0, rs, hi_sl] = o[BLOCK:].astype(o_ref.dtype)

    units = [(h, t) for h in range(N_KV_HEADS) for t in range(qb)]
    s_next = logits(*units[0])
    for u, unit in enumerate(units):
        s_cur = s_next
        if u + 1 < len(units):
            s_next = logits(*units[u + 1])
        finish(*unit, s_cur)


def _attention(sinks, e, q, kt, v, kxt, vx, window):
    b, t, _ = q.shape
    l = vx.shape[1]
    nb = t // BLOCK
    qb = ATTN_QBLOCKS if window else nb
    w = vx.shape[2]
    steps = nb // qb
    cur = lambda i, n: (i, n, 0)
    in_specs = [pl.BlockSpec(memory_space=pltpu.SMEM), pl.BlockSpec((1, qb * BLOCK, Q_DIM), cur)]
    args = [sinks, q]
    if window:
        pb = lambda n: jnp.maximum(n * qb - 1, 0)
        nx = lambda n: jnp.minimum(n * qb + qb, nb - 1)
        in_specs += [
            pl.BlockSpec((w, BLOCK), lambda i, n: (0, i * nb + pb(n))),
            pl.BlockSpec((w, qb * BLOCK), lambda i, n: (0, i * steps + n)),
            pl.BlockSpec((w, BLOCK), lambda i, n: (0, i * nb + nx(n))),
            pl.BlockSpec((1, BLOCK, w), lambda i, n: (i, pb(n), 0)),
            pl.BlockSpec((1, qb * BLOCK, w), cur),
            pl.BlockSpec((1, BLOCK, w), lambda i, n: (i, nx(n), 0)),
        ]
        args += [kt, kt, kt, v, v, v]
    in_specs += [pl.BlockSpec((w, l), lambda i, n: (0, i)), pl.BlockSpec((1, l, w), lambda i, n: (i, 0, 0))]
    args += [kxt, vx]
    return pl.pallas_call(
        functools.partial(_attn_kernel, window=window, qb=qb, nblocks=nb, e=e),
        grid=(b, steps),
        in_specs=in_specs,
        out_specs=pl.BlockSpec((1, qb * BLOCK, Q_DIM), cur),
        out_shape=jax.ShapeDtypeStruct((b, t, Q_DIM), BF16),
        compiler_params=_params("arbitrary", "arbitrary"),
        name="attention",
    )(*args)


def _seq_dft_kernel(ct_ref, st_ref, flip_ref, ab_ref, o_ref, *, norm):
    t = ab_ref.shape[1]
    half = t // 2
    a = ab_ref[0, :, 0:FFT_DIM]
    b = ab_ref[0, :, FFT_DIM:2 * FFT_DIM]
    p = _dot(ct_ref[...], a)
    q = _dot(st_ref[...], b)
    o_ref[0, 0:half] = ((p - q) * norm).astype(o_ref.dtype)
    sign = 1.0 - 2.0 * (lax.broadcasted_iota(jnp.int32, (t, 1), 0) % 2).astype(F32)
    nyquist = jnp.sum(a.astype(F32) * sign, axis=0, keepdims=True)
    first = lax.broadcasted_iota(jnp.int32, (half, 1), 0) == 0
    mirrored = (jnp.where(first, nyquist, p + q) * norm).astype(BF16)
    o_ref[0, half:t] = _dot(flip_ref[...], mirrored).astype(o_ref.dtype)


def _seq_dft(ab3, tables):
    b, t, _ = ab3.shape
    return pl.pallas_call(
        functools.partial(_seq_dft_kernel, norm=float((t * FFT_GROUP) ** -0.5)),
        grid=(b,),
        in_specs=[_const_spec(a) for a in tables] + [pl.BlockSpec((1, t, 2 * FFT_DIM), lambda i: (i, 0, 0))],
        out_specs=pl.BlockSpec((1, t, FFT_DIM), lambda i: (i, 0, 0)),
        out_shape=jax.ShapeDtypeStruct((b, t, FFT_DIM), BF16),
        compiler_params=_params("arbitrary"),
        name="seq_dft",
    )(*tables, ab3)


def _post_kernel(ap_ref, a_ref, an_ref, bp_ref, b_ref, bn_ref, xp_ref, x_ref, xn_ref,
                 wo_ref, gn1_ref, g1_ref, gn2_ref, sc_ref, sh_ref, g2_ref, gn3_ref,
                 wu_ref, cw_ref, wd_ref, o_ref, y_scr, hh_scr, xm_scr, act_scr, *, seg):
    i = pl.program_id(0)
    tm = x_ref.shape[0]
    main = slice(HALO, HALO + tm)
    rd = i % 2
    wr = 1 - rd
    piece_rows = tm // POST_PIECES
    pieces = ([(0, xp_ref, 0, HALO)]
              + [(HALO + k * piece_rows, x_ref, k * piece_rows, piece_rows) for k in range(POST_PIECES)]
              + [(HALO + tm, xn_ref, 0, HALO)])
    ka = a_ref.shape[1]

    def prepare(piece):
        lo, ref, off, nrows = piece
        xm = ref[off:off + nrows] + g1_ref[0] * _rms(y_scr[lo:lo + nrows], gn1_ref[0])
        hn = _rms(xm, gn2_ref[0]) * (1.0 + sc_ref[0]) + sh_ref[0]
        hh_scr[wr, lo:lo + nrows] = hn.astype(BF16)
        if ref is x_ref:
            xm_scr[wr, off:off + nrows] = xm
        return jnp.concatenate([_zero_anchor(hn)] * (FFN_CHUNK // LANES), axis=1)

    def out_projection():
        y_scr[...] = (_dot(_ext(ap_ref, a_ref, an_ref), wo_ref[0, 0:ka, :])
                      + _dot(_ext(bp_ref, b_ref, bn_ref), wo_ref[0, ka:, :]))

    @pl.when(i == 0)
    def _():
        y_scr[...] = jnp.zeros_like(y_scr)

    @pl.when(i < 2)
    def _():
        for piece in pieces:
            prepare(piece)
        out_projection()

    @pl.when(i >= 2)
    def _():
        hh = hh_scr[rd]
        h = hh[main]
        cw = cw_ref[0]
        anchor = None
        for c in range(D_FF // FFN_CHUNK):
            cs = slice(c * FFN_CHUNK, (c + 1) * FFN_CHUNK)
            cwc = cw[:, cs] if anchor is None else cw[:, cs] + anchor
            conv = _conv3(_dot(hh, wu_ref[0, :, cs]), seg, cwc, tile=i - 2)
            val = _dot(h, wu_ref[0, :, D_FF + c * FFN_CHUNK:D_FF + (c + 1) * FFN_CHUNK])
            act_scr[:, cs] = (conv / (1.0 + jnp.exp(-conv)) * val).astype(BF16)
            anchor = prepare(pieces[c]) if c < len(pieces) else None
        y2 = _dot(act_scr[...], wd_ref[0])
        out_projection()
        o_ref[...] = xm_scr[rd] + g2_ref[0] * _rms(y2, gn3_ref[0])


def _post(a2, b2, stream, tabs, layer, w_out, mix_layer, w_up, conv_w, w_down):
    x2 = stream.x
    rows, d = x2.shape
    tm = ROW_TILE
    seg = stream.seg
    assert seg % tm == 0 or tm % seg == 0
    n = rows // tm
    stage_a = lambda i: jnp.minimum(i, n - 1)
    stage_b = lambda i: jnp.clip(i - 1, 0, n - 1)
    stage_c = lambda i: jnp.maximum(i - 2, 0)
    row = stream.mod_row(min(tm, seg))
    mod = lambda chunk, stage: _mod_spec(layer, chunk, row, stage)
    return pl.pallas_call(
        functools.partial(_post_kernel, seg=seg),
        grid=(n + 2,),
        in_specs=(_halo_specs(tm, rows, a2.shape[1], stage_a) + _halo_specs(tm, rows, b2.shape[1], stage_a)
                  + _halo_specs(tm, rows, d, stage_b)
                  + [_layer_spec(w_out, mix_layer), _gain_spec(layer, 1), mod(2, stage_b), _gain_spec(layer, 2),
                     mod(4, stage_b), mod(3, stage_b), mod(5, stage_c), _gain_spec(layer, 3),
                     _layer_spec(w_up, layer), _layer_spec(conv_w, layer), _layer_spec(w_down, layer)]),
        out_specs=pl.BlockSpec((tm, d), lambda i: (stage_c(i), 0)),
        out_shape=jax.ShapeDtypeStruct((rows, d), F32),
        scratch_shapes=[pltpu.VMEM((tm + 2 * HALO, d), F32), pltpu.VMEM((2, tm + 2 * HALO, d), BF16),
                        pltpu.VMEM((2, tm, d), F32), pltpu.VMEM((tm, D_FF), BF16)],
        compiler_params=_params("arbitrary"),
        name="post",
    )(a2, a2, a2, b2, b2, b2, x2, x2, x2, w_out, tabs.gains, tabs.mods, tabs.gains, tabs.mods, tabs.mods,
      tabs.mods, tabs.gains, w_up, conv_w, w_down)


def _rope_tables(seq):
    half = HEAD_DIM // 2
    nf = half // 2
    inv = ROPE_BASE ** (-np.arange(nf, dtype=np.float64) / nf)
    t = np.arange(seq)
    lane = np.arange(LANES)
    in_head = lane % HEAD_DIM
    pos = np.where((in_head // half)[None, :] == 0, (t // GRID_W)[:, None], (t % GRID_W)[:, None])
    ang = pos * inv[lane % nf][None, :]
    first = ((lane % half) < nf)[None, :]
    cos = np.cos(ang)
    sin_a = np.where(first, -np.sin(ang), 0.0)
    sin_b = np.where(first, 0.0, np.sin(ang))
    return tuple(jnp.asarray(a, F32) for a in (cos, sin_a, sin_b))


def _dft_tables(n, rows=None):
    k = np.arange(n if rows is None else rows)
    ang = 2.0 * np.pi * ((k[:, None] * np.arange(n)[None, :]) % n) / n
    return jnp.asarray(np.cos(ang), F32).astype(BF16), jnp.asarray(np.sin(ang), F32).astype(BF16)


def _seq_dft_tables(n):
    half = n // 2
    flip = np.zeros((half, half), np.float32)
    flip[np.arange(half), (half - np.arange(half)) % half] = 1.0
    return _dft_tables(n, half) + (jnp.asarray(flip).astype(BF16),)


def kernel(x, c, ctx, c_ctx, mod_w, mod_b, norm_g, att_w_in, att_sink, sconv_w, att_w_out, mix_w_in,
           pool_w_grp, pool_scale, mix_w_out, ffn_w_up, ffn_conv, ffn_w_down):
    bsz, seq, d = x.shape
    ctx_len = ctx.shape[1]
    assert d == D_MODEL and seq % ROW_TILE == 0 and (bsz * ctx_len) % ROW_TILE == 0 and ROW_TILE % ctx_len == 0
    assert bsz < MOD_ROWS and norm_g.shape[1] == N_GAINS

    cc = jnp.concatenate([c, c_ctx[None, :], jnp.zeros((MOD_ROWS - bsz - 1, d), F32)], axis=0)
    mods = _modulation(cc, mod_w, mod_b)
    tabs = _Tables(mods.reshape(DEPTH * MOD_ROWS, 1, 6 * d), norm_g.reshape(DEPTH * N_GAINS, 1, d))

    rope = _rope_tables(seq)
    chan_cs = jnp.concatenate(_dft_tables(FFT_GROUP), axis=1)
    seq_tabs = {seq: _seq_dft_tables(seq), ctx_len: _seq_dft_tables(ctx_len)}

    w_att_in, w_att_out = att_w_in.astype(BF16), att_w_out.astype(BF16)
    w_mix_in, w_mix_out = mix_w_in.astype(BF16), mix_w_out.astype(BF16)
    w_up, w_down = ffn_w_up.astype(BF16), ffn_w_down.astype(BF16)
    g4 = pool_w_grp.astype(BF16)
    zero = jnp.zeros_like(g4[:, 0])
    w_grp = jnp.stack([jnp.concatenate([jnp.concatenate([g4[:, 2 * j], zero], axis=2),
                                        jnp.concatenate([zero, g4[:, 2 * j + 1]], axis=2)], axis=1)
                       for j in range(N_POOL_GROUPS // 2)], axis=1)
    p_scale = pool_scale.reshape(-1, 1, POOL_DIM)

    xs = x.reshape(bsz * seq, d)
    hc = ctx.reshape(bsz * ctx_len, d)

    for i in range(DEPTH):
        even = i % 2 == 0
        ctx_live = any(j % 2 == 0 for j in range(i + 1, DEPTH))
        lat = _Stream(xs, seq, None)
        con = _Stream(hc, ctx_len, bsz)
        ffn = (w_up, ffn_conv, w_down)
        if even:
            e = i // 2
            q, kt, v, sx = _in_even(lat, tabs, i, e, w_att_in, sconv_w, rope)
            if ctx_live:
                qc, kct, vc, sxc = _in_even(con, tabs, i, e, w_att_in, sconv_w, None)
            else:
                kct, vc = _in_kv(con, tabs, i, e, w_att_in)
            vc3 = vc.reshape(bsz, ctx_len, -1)
            ax = _attention(att_sink, e, q.reshape(bsz, seq, -1), kt, v.reshape(bsz, seq, -1), kct, vc3, True)
            xs = _post(ax.reshape(bsz * seq, -1), sx, lat, tabs, i, w_att_out, e, *ffn)
            if ctx_live:
                ac = _attention(att_sink, e, qc.reshape(bsz, ctx_len, -1), None, None, kct, vc3, False)
                hc = _post(ac.reshape(bsz * ctx_len, -1), sxc, con, tabs, i, w_att_out, e, *ffn)
        else:
            o = i // 2
            outs = []
            for stream in (lat, con) if ctx_live else (lat,):
                yp, ab = _in_odd(stream, tabs, i, o, w_mix_in, chan_cs, w_grp, p_scale)
                yf = _seq_dft(ab.reshape(bsz, stream.seg, -1), seq_tabs[stream.seg])
                outs.append(_post(yp, yf.reshape(bsz * stream.seg, -1), stream, tabs, i, w_mix_out, o, *ffn))
            xs = outs[0]
            if ctx_live:
                hc = outs[1]
    return xs.reshape(bsz, seq, d)
```

```python
import functools
from typing import NamedTuple

import numpy as np
import jax
import jax.numpy as jnp
from jax import lax
from jax.experimental import pallas as pl
from jax.experimental.pallas import tpu as pltpu

F32 = jnp.float32
BF16 = jnp.bfloat16

D_MODEL = 1024
DEPTH = 4
GRID_W = 64
EPS = 1e-6
N_Q_HEADS = 8
N_KV_HEADS = 2
HEAD_DIM = 64
WINDOW = 128
BLOCK = 128
ROPE_BASE = 10000.0
SCONV_DIM = 512
POOL_DIM = 512
N_POOL_GROUPS = 4
POOL_GROUP = POOL_DIM // N_POOL_GROUPS
POOL_WINDOWS = (2, 4, 8, 16)
FFT_DIM = 512
N_FFT_GROUPS = 4
FFT_GROUP = FFT_DIM // N_FFT_GROUPS
D_FF = 2816
Q_DIM = N_Q_HEADS * HEAD_DIM
KV_DIM = N_KV_HEADS * HEAD_DIM
KV_TILES = 2 * KV_DIM * 2

LANES = 128
MASK_VALUE = -1e30
LOG2E = 1.4426950408889634
ATTN_QBLOCKS = 16
VMEM_LIMIT = 56 * 1024 * 1024
ROW_TILE = 512
IN_TILE = 1024
HALO = 16
FFN_CHUNK = 256
POST_PIECES = 8
MOD_ROWS = 16
N_GAINS = 4

assert HALO >= max(POOL_WINDOWS) and all(w & (w - 1) == 0 for w in POOL_WINDOWS)


def _params(*sem):
    return pltpu.CompilerParams(dimension_semantics=sem, vmem_limit_bytes=VMEM_LIMIT)


def _rms(x, g):
    ms = jnp.mean(x * x, axis=-1, keepdims=True)
    return x * lax.rsqrt(ms + EPS) * g


def _dot(a, b):
    return jnp.dot(a, b, preferred_element_type=F32)


class _Stream(NamedTuple):
    x: jax.Array
    seg: int
    ctx_row: int | None

    def tile(self):
        return min(IN_TILE, self.seg)

    def mod_row(self, tm):
        if self.ctx_row is not None:
            row = self.ctx_row
            return lambda tile: row
        per_seg = self.seg // tm
        return lambda tile: tile // per_seg


class _Tables(NamedTuple):
    mods: jax.Array
    gains: jax.Array


def _layer_spec(a, layer):
    nd = a.ndim
    return pl.BlockSpec((1,) + a.shape[1:], lambda *_: (layer,) + (0,) * (nd - 1), pipeline_mode=pl.Buffered(1))


def _const_spec(a):
    return pl.BlockSpec(a.shape, lambda *_: (0,) * a.ndim, pipeline_mode=pl.Buffered(1))


def _gain_spec(layer, j):
    return pl.BlockSpec((1, 1, D_MODEL), lambda *_: (layer * N_GAINS + j, 0, 0))


def _mod_spec(layer, chunk, row, tile=lambda i: i):
    return pl.BlockSpec((1, 1, D_MODEL), lambda i: (layer * MOD_ROWS + row(tile(i)), 0, chunk))


def _halo_specs(tm, rows, width, tile=lambda i: i):
    hb = tm // HALO
    return [
        pl.BlockSpec((HALO, width), lambda i: (jnp.maximum(tile(i) * hb - 1, 0), 0)),
        pl.BlockSpec((tm, width), lambda i: (tile(i), 0)),
        pl.BlockSpec((HALO, width), lambda i: (jnp.minimum((tile(i) + 1) * hb, rows // HALO - 1), 0)),
    ]


def _ext(prev_ref, main_ref, next_ref):
    return jnp.concatenate([prev_ref[...], main_ref[...], next_ref[...]], axis=0)


def _tile_pos(tm, seg):
    return (pl.program_id(0) * tm + lax.broadcasted_iota(jnp.int32, (tm, 1), 0)) % seg


def _seg_ext(x_ext, seg, tile=None):
    tm = x_ext.shape[0] - 2 * HALO
    i = pl.program_id(0) if tile is None else tile
    top_ok = (i * tm) % seg != 0
    bot_ok = ((i + 1) * tm) % seg != 0
    top, bot = x_ext[:HALO], x_ext[HALO + tm:]
    return jnp.concatenate([jnp.where(top_ok, top, jnp.zeros_like(top)), x_ext[HALO:HALO + tm],
                            jnp.where(bot_ok, bot, jnp.zeros_like(bot))], axis=0)


def _roll_rows(x_ext, off):
    rows = x_ext.shape[0]
    return pltpu.roll(x_ext, (-off) % rows, 0)


def _conv3(x_ext, seg, w, tile=None):
    tm = x_ext.shape[0] - 2 * HALO
    main = slice(HALO, HALO + tm)
    xm = _seg_ext(x_ext, seg, tile)
    prev, nxt = _roll_rows(xm, -1)[main], _roll_rows(xm, 1)[main]
    if seg < tm:
        pos = lax.broadcasted_iota(jnp.int32, (tm, 1), 0) % seg
        prev = jnp.where(pos == 0, 0.0, prev)
        nxt = jnp.where(pos == seg - 1, 0.0, nxt)
    return prev * w[0:1] + x_ext[main] * w[1:2] + nxt * w[2:3]


def _zero_anchor(v):
    fold = sum(v[r:r + 8] for r in range(0, v.shape[0], 8))
    fold = sum(fold[:, l:l + LANES] for l in range(0, fold.shape[1], LANES))
    return jnp.minimum(jnp.abs(fold[0:1]), 0.0)


def _modulated(x, gn_ref, sc_ref, sh_ref):
    return (_rms(x, gn_ref[0]) * (1.0 + sc_ref[0]) + sh_ref[0]).astype(BF16)


def _stream_specs(stream, tabs, layer):
    tm = stream.tile()
    rows, d = stream.x.shape
    row = stream.mod_row(tm)
    specs = _halo_specs(tm, rows, d) + [_gain_spec(layer, 0), _mod_spec(layer, 1, row), _mod_spec(layer, 0, row)]
    return specs, [stream.x] * 3 + [tabs.gains, tabs.mods, tabs.mods]


def _mod_kernel(c_ref, w_ref, b_ref, o_ref):
    c = c_ref[...]
    s = (c / (1.0 + jnp.exp(-c))).astype(BF16)
    o_ref[0] = _dot(s, w_ref[0].astype(BF16)) + b_ref[0]


def _modulation(cc, mod_w, mod_b):
    depth, d, n = mod_w.shape
    tn = 1024
    return pl.pallas_call(
        _mod_kernel,
        grid=(depth, n // tn),
        in_specs=[
            pl.BlockSpec((MOD_ROWS, d), lambda i, j: (0, 0)),
            pl.BlockSpec((1, d, tn), lambda i, j: (i, 0, j)),
            pl.BlockSpec((1, 1, tn), lambda i, j: (i, 0, j)),
        ],
        out_specs=pl.BlockSpec((1, MOD_ROWS, tn), lambda i, j: (i, 0, j)),
        out_shape=jax.ShapeDtypeStruct((depth, MOD_ROWS, n), F32),
        compiler_params=_params("arbitrary", "arbitrary"),
        name="modulation",
    )(cc, mod_w, mod_b.reshape(depth, 1, n))


def _rope128(x, cos, sin_a, sin_b):
    return x * cos + pltpu.roll(x, LANES - 16, 1) * sin_a + pltpu.roll(x, 16, 1) * sin_b


def _write_kv(k, v, k_ref, v_ref):
    lo = lax.broadcasted_iota(jnp.int32, (1, LANES), 1) < HEAD_DIM
    vr = pltpu.roll(v, HEAD_DIM, 1)
    zero = jnp.zeros_like(v)
    v_ref[...] = jnp.concatenate(
        [jnp.where(lo, v, zero), jnp.where(lo, zero, vr), jnp.where(lo, vr, zero), jnp.where(lo, zero, v)],
        axis=1).astype(BF16)
    kt = k.T
    top = lax.broadcasted_iota(jnp.int32, (LANES, 1), 0) < HEAD_DIM
    ktr = pltpu.roll(kt, HEAD_DIM, 0)
    zero = jnp.zeros_like(kt)
    k_ref[...] = jnp.concatenate(
        [jnp.where(top, kt, zero), jnp.where(top, zero, ktr), jnp.where(top, ktr, zero), jnp.where(top, zero, kt)],
        axis=0).astype(BF16)


def _in_even_kernel(xp_ref, x_ref, xn_ref, gn_ref, sc_ref, sh_ref, w_ref, cw_ref, *rest, rope, seg):
    if rope:
        cos_ref, sa_ref, sb_ref = rest[:3]
        rest = rest[3:]
    q_ref, k_ref, v_ref, s_ref = rest
    tm = x_ref.shape[0]
    hh = _modulated(_ext(xp_ref, x_ref, xn_ref), gn_ref, sc_ref, sh_ref)
    h = hh[HALO:HALO + tm]
    o = Q_DIM + 2 * KV_DIM
    cz = (_dot(hh, w_ref[0, :, o + SCONV_DIM:o + 2 * SCONV_DIM])
          * _dot(hh, w_ref[0, :, o + 2 * SCONV_DIM:o + 3 * SCONV_DIM]))
    q = _dot(h, w_ref[0, :, 0:Q_DIM])
    kv = _dot(h, w_ref[0, :, Q_DIM:Q_DIM + 2 * KV_DIM])
    k, v = kv[:, 0:KV_DIM], kv[:, KV_DIM:2 * KV_DIM]
    b = _dot(h, w_ref[0, :, o:o + SCONV_DIM])
    s_ref[...] = (b * _conv3(cz, seg, cw_ref[0])).astype(BF16)
    if rope:
        cos, sa, sb = cos_ref[...], sa_ref[...], sb_ref[...]
        k = _rope128(k, cos, sa, sb)
        q = jnp.concatenate(
            [_rope128(q[:, LANES * j:LANES * (j + 1)], cos, sa, sb) for j in range(Q_DIM // LANES)], axis=1)
    q_ref[...] = (q * (HEAD_DIM ** -0.5 * LOG2E)).astype(BF16)
    _write_kv(k, v, k_ref, v_ref)


def _in_even(stream, tabs, layer, e, w_in, conv_w, rope_tables):
    rows = stream.x.shape[0]
    tm = stream.tile()
    per_seg = stream.seg // tm
    rope = rope_tables is not None
    row = lambda i: (i, 0)
    in_specs, args = _stream_specs(stream, tabs, layer)
    in_specs += [_layer_spec(w_in, e), _layer_spec(conv_w, e)]
    args += [w_in, conv_w]
    if rope:
        in_specs += [pl.BlockSpec((tm, LANES), lambda i: (i % per_seg, 0))] * 3
        args += list(rope_tables)
    return pl.pallas_call(
        functools.partial(_in_even_kernel, rope=rope, seg=stream.seg),
        grid=(rows // tm,),
        in_specs=in_specs,
        out_specs=[pl.BlockSpec((tm, Q_DIM), row), pl.BlockSpec((KV_TILES, tm), lambda i: (0, i)),
                   pl.BlockSpec((tm, KV_TILES), row), pl.BlockSpec((tm, SCONV_DIM), row)],
        out_shape=[jax.ShapeDtypeStruct((rows, Q_DIM), BF16), jax.ShapeDtypeStruct((KV_TILES, rows), BF16),
                   jax.ShapeDtypeStruct((rows, KV_TILES), BF16), jax.ShapeDtypeStruct((rows, SCONV_DIM), BF16)],
        compiler_params=_params("arbitrary"),
        name="in_even",
    )(*args)


def _in_kv_kernel(x_ref, gn_ref, sc_ref, sh_ref, w_ref, k_ref, v_ref):
    h = _modulated(x_ref[...], gn_ref, sc_ref, sh_ref)
    kv = _dot(h, w_ref[0, :, Q_DIM:Q_DIM + 2 * KV_DIM])
    _write_kv(kv[:, 0:KV_DIM], kv[:, KV_DIM:2 * KV_DIM], k_ref, v_ref)


def _in_kv(stream, tabs, layer, e, w_in):
    rows, d = stream.x.shape
    tm = stream.tile()
    row = lambda i: (i, 0)
    mrow = stream.mod_row(tm)
    return pl.pallas_call(
        _in_kv_kernel,
        grid=(rows // tm,),
        in_specs=[pl.BlockSpec((tm, d), row), _gain_spec(layer, 0), _mod_spec(layer, 1, mrow),
                  _mod_spec(layer, 0, mrow), _layer_spec(w_in, e)],
        out_specs=[pl.BlockSpec((KV_TILES, tm), lambda i: (0, i)), pl.BlockSpec((tm, KV_TILES), row)],
        out_shape=[jax.ShapeDtypeStruct((KV_TILES, rows), BF16), jax.ShapeDtypeStruct((rows, KV_TILES), BF16)],
        compiler_params=_params("arbitrary"),
        name="in_kv",
    )(stream.x, tabs.gains, tabs.mods, tabs.mods, w_in)


def _in_odd_kernel(xp_ref, x_ref, xn_ref, gn_ref, sc_ref, sh_ref, w_ref, ccs_ref, wg_ref, ps_ref,
                   p_ref, ab_ref, *, seg):
    tm = x_ref.shape[0]
    hh = _modulated(_ext(xp_ref, x_ref, xn_ref), gn_ref, sc_ref, sh_ref)
    h = hh[HALO:HALO + tm]
    up = _seg_ext(_dot(hh, w_ref[0, :, 0:POOL_DIM]), seg)
    uf = _dot(h, w_ref[0, :, POOL_DIM:POOL_DIM + FFT_DIM]).astype(BF16)
    ccs = ccs_ref[...]
    both = [_dot(uf[:, g * FFT_GROUP:(g + 1) * FFT_GROUP], ccs) for g in range(N_FFT_GROUPS)]
    ab_ref[...] = jnp.concatenate([r[:, 0:FFT_GROUP] for r in both] + [r[:, FFT_GROUP:] for r in both],
                                  axis=1).astype(BF16)
    pos = _tile_pos(tm, seg)
    pooled = []
    for gi, w in enumerate(POOL_WINDOWS):
        ug = up[:, gi * POOL_GROUP:(gi + 1) * POOL_GROUP]
        back, fwd = w // 2, w - w // 2
        run, span = ug, 1
        while span < w:
            run = run + _roll_rows(run, span)
            span *= 2
        total = _roll_rows(run, -back)[HALO:HALO + tm]
        count = (jnp.minimum(pos + fwd, seg) - jnp.maximum(pos - back, 0)).astype(F32)
        pooled.append((total / count - ug[HALO:HALO + tm]).astype(BF16))
    outs = [_dot(jnp.concatenate(pooled[2 * j:2 * j + 2], axis=1), wg_ref[0, j]) for j in range(N_POOL_GROUPS // 2)]
    p_ref[...] = (jnp.concatenate(outs, axis=1) * ps_ref[0]).astype(BF16)


def _in_odd(stream, tabs, layer, o, w_in, ccs, wg, ps):
    rows = stream.x.shape[0]
    tm = stream.tile()
    row = lambda i: (i, 0)
    in_specs, args = _stream_specs(stream, tabs, layer)
    return pl.pallas_call(
        functools.partial(_in_odd_kernel, seg=stream.seg),
        grid=(rows // tm,),
        in_specs=in_specs + [_layer_spec(w_in, o), _const_spec(ccs), _layer_spec(wg, o), _layer_spec(ps, o)],
        out_specs=[pl.BlockSpec((tm, POOL_DIM), row), pl.BlockSpec((tm, 2 * FFT_DIM), row)],
        out_shape=[jax.ShapeDtypeStruct((rows, POOL_DIM), BF16), jax.ShapeDtypeStruct((rows, 2 * FFT_DIM), BF16)],
        compiler_params=_params("arbitrary"),
        name="in_odd",
    )(*args, w_in, ccs, wg, ps)


def _attn_kernel(sink_ref, q_ref, *rest, window, qb, nblocks, e):
    if window:
        ktp, ktc, ktn, vp, vc, vn, kxt, vx, o_ref = rest
    else:
        kxt, vx, o_ref = rest
    n = pl.program_id(1)
    rows = 2 * BLOCK
    first_pair = lax.broadcasted_iota(jnp.int32, (rows, 1), 0) < BLOCK
    lo = lax.broadcasted_iota(jnp.int32, (1, LANES), 1) < HEAD_DIM
    if window:
        qi = lax.broadcasted_iota(jnp.int32, (rows, BLOCK), 0) % BLOCK
        j = lax.broadcasted_iota(jnp.int32, (rows, BLOCK), 1)
        tri_prev = jnp.where(j >= qi, 0.0, MASK_VALUE)
        tri_next = jnp.where(j <= qi, 0.0, MASK_VALUE)
        bias = []
        for t in range(qb):
            blk = n * qb + t
            bias.append((tri_prev + jnp.where(blk >= 1, 0.0, MASK_VALUE),
                         tri_next + jnp.where(blk <= nblocks - 2, 0.0, MASK_VALUE)))

    def lane_halves(h):
        c0 = 2 * LANES * h
        return slice(c0, c0 + LANES), slice(c0 + LANES, c0 + 2 * LANES)

    def keys_values(h, t):
        kparts, vparts = [], []
        for sl in lane_halves(h):
            if window:
                kall = jnp.concatenate([ktp[sl, :], ktc[sl, :], ktn[sl, :]], axis=1)
                vall = jnp.concatenate([vp[0, :, sl], vc[0, :, sl], vn[0, :, sl]], axis=0)
                kparts.append(kall[:, t * BLOCK:(t + 3) * BLOCK])
                vparts.append(vall[t * BLOCK:(t + 3) * BLOCK])
            kparts.append(kxt[sl, :])
            vparts.append(vx[0, :, sl])
        return jnp.concatenate(kparts, axis=1), jnp.concatenate(vparts, axis=0)

    def logits(h, t):
        rs = slice(t * BLOCK, (t + 1) * BLOCK)
        lo_sl, hi_sl = lane_halves(h)
        qp = jnp.concatenate([q_ref[0, rs, lo_sl], q_ref[0, rs, hi_sl]], axis=0)
        return _dot(qp, keys_values(h, t)[0])

    def finish(h, t, s):
        vst = keys_values(h, t)[1]
        nkeys = vst.shape[0] // 2
        probs, inv = [], []
        for half in range(2):
            tiles = [s[:, half * nkeys + c * LANES:half * nkeys + (c + 1) * LANES] for c in range(nkeys // LANES)]
            if window:
                tiles[0] = tiles[0] + bias[t][0]
                tiles[2] = tiles[2] + bias[t][1]
            head = 4 * h + half
            sink_col = jnp.where(first_pair, sink_ref[e, head], sink_ref[e, head + 2]) * LOG2E
            m = jnp.maximum(jnp.max(functools.reduce(jnp.maximum, tiles), axis=-1, keepdims=True), sink_col)
            es = [jnp.exp2(tl - m) for tl in tiles]
            den = jnp.sum(functools.reduce(jnp.add, es), axis=-1, keepdims=True) + jnp.exp2(sink_col - m)
            probs += [e_.astype(BF16) for e_ in es]
            inv.append(1.0 / den)
        o = _dot(jnp.concatenate(probs, axis=1), vst) * jnp.where(lo, inv[0], inv[1])
        rs = slice(t * BLOCK, (t + 1) * BLOCK)
        lo_sl, hi_sl = lane_halves(h)
        o_ref[0, rs, lo_sl] = o[:BLOCK].astype(o_ref.dtype)
        o_ref[0, rs, hi_sl] = o[BLOCK:].astype(o_ref.dtype)

    units = [(h, t) for h in range(N_KV_HEADS) for t in range(qb)]
    s_next = logits(*units[0])
    for u, unit in enumerate(units):
        s_cur = s_next
        if u + 1 < len(units):
            s_next = logits(*units[u + 1])
        finish(*unit, s_cur)


def _attention(sinks, e, q, kt, v, kxt, vx, window):
    b, t, _ = q.shape
    l = vx.shape[1]
    nb = t // BLOCK
    qb = ATTN_QBLOCKS if window else nb
    w = vx.shape[2]
    steps = nb // qb
    cur = lambda i, n: (i, n, 0)
    in_specs = [pl.BlockSpec(memory_space=pltpu.SMEM), pl.BlockSpec((1, qb * BLOCK, Q_DIM), cur)]
    args = [sinks, q]
    if window:
        pb = lambda n: jnp.maximum(n * qb - 1, 0)
        nx = lambda n: jnp.minimum(n * qb + qb, nb - 1)
        in_specs += [
            pl.BlockSpec((w, BLOCK), lambda i, n: (0, i * nb + pb(n))),
            pl.BlockSpec((w, qb * BLOCK), lambda i, n: (0, i * steps + n)),
            pl.BlockSpec((w, BLOCK), lambda i, n: (0, i * nb + nx(n))),
            pl.BlockSpec((1, BLOCK, w), lambda i, n: (i, pb(n), 0)),
            pl.BlockSpec((1, qb * BLOCK, w), cur),
            pl.BlockSpec((1, BLOCK, w), lambda i, n: (i, nx(n), 0)),
        ]
        args += [kt, kt, kt, v, v, v]
    in_specs += [pl.BlockSpec((w, l), lambda i, n: (0, i)), pl.BlockSpec((1, l, w), lambda i, n: (i, 0, 0))]
    args += [kxt, vx]
    return pl.pallas_call(
        functools.partial(_attn_kernel, window=window, qb=qb, nblocks=nb, e=e),
        grid=(b, steps),
        in_specs=in_specs,
        out_specs=pl.BlockSpec((1, qb * BLOCK, Q_DIM), cur),
        out_shape=jax.ShapeDtypeStruct((b, t, Q_DIM), BF16),
        compiler_params=_params("arbitrary", "arbitrary"),
        name="attention",
    )(*args)


def _seq_dft_kernel(ct_ref, st_ref, flip_ref, ab_ref, o_ref, *, norm):
    t = ab_ref.shape[1]
    half = t // 2
    a = ab_ref[0, :, 0:FFT_DIM]
    b = ab_ref[0, :, FFT_DIM:2 * FFT_DIM]
    p = _dot(ct_ref[...], a)
    q = _dot(st_ref[...], b)
    o_ref[0, 0:half] = ((p - q) * norm).astype(o_ref.dtype)
    sign = 1.0 - 2.0 * (lax.broadcasted_iota(jnp.int32, (t, 1), 0) % 2).astype(F32)
    nyquist = jnp.sum(a.astype(F32) * sign, axis=0, keepdims=True)
    first = lax.broadcasted_iota(jnp.int32, (half, 1), 0) == 0
    mirrored = (jnp.where(first, nyquist, p + q) * norm).astype(BF16)
    o_ref[0, half:t] = _dot(flip_ref[...], mirrored).astype(o_ref.dtype)


def _seq_dft(ab3, tables):
    b, t, _ = ab3.shape
    return pl.pallas_call(
        functools.partial(_seq_dft_kernel, norm=float((t * FFT_GROUP) ** -0.5)),
        grid=(b,),
        in_specs=[_const_spec(a) for a in tables] + [pl.BlockSpec((1, t, 2 * FFT_DIM), lambda i: (i, 0, 0))],
        out_specs=pl.BlockSpec((1, t, FFT_DIM), lambda i: (i, 0, 0)),
        out_shape=jax.ShapeDtypeStruct((b, t, FFT_DIM), BF16),
        compiler_params=_params("arbitrary"),
        name="seq_dft",
    )(*tables, ab3)


def _post_kernel(ap_ref, a_ref, an_ref, bp_ref, b_ref, bn_ref, xp_ref, x_ref, xn_ref,
                 wo_ref, gn1_ref, g1_ref, gn2_ref, sc_ref, sh_ref, g2_ref, gn3_ref,
                 wu_ref, cw_ref, wd_ref, o_ref, y_scr, hh_scr, xm_scr, act_scr, *, seg):
    i = pl.program_id(0)
    tm = x_ref.shape[0]
    main = slice(HALO, HALO + tm)
    rd = i % 2
    wr = 1 - rd
    piece_rows = tm // POST_PIECES
    pieces = ([(0, xp_ref, 0, HALO)]
              + [(HALO + k * piece_rows, x_ref, k * piece_rows, piece_rows) for k in range(POST_PIECES)]
              + [(HALO + tm, xn_ref, 0, HALO)])
    ka = a_ref.shape[1]

    def prepare(piece):
        lo, ref, off, nrows = piece
        xm = ref[off:off + nrows] + g1_ref[0] * _rms(y_scr[lo:lo + nrows], gn1_ref[0])
        hn = _rms(xm, gn2_ref[0]) * (1.0 + sc_ref[0]) + sh_ref[0]
        hh_scr[wr, lo:lo + nrows] = hn.astype(BF16)
        if ref is x_ref:
            xm_scr[wr, off:off + nrows] = xm
        return jnp.concatenate([_zero_anchor(hn)] * (FFN_CHUNK // LANES), axis=1)

    def out_projection():
        y_scr[...] = (_dot(_ext(ap_ref, a_ref, an_ref), wo_ref[0, 0:ka, :])
                      + _dot(_ext(bp_ref, b_ref, bn_ref), wo_ref[0, ka:, :]))

    @pl.when(i == 0)
    def _():
        y_scr[...] = jnp.zeros_like(y_scr)

    @pl.when(i < 2)
    def _():
        for piece in pieces:
            prepare(piece)
        out_projection()

    @pl.when(i >= 2)
    def _():
        hh = hh_scr[rd]
        h = hh[main]
        cw = cw_ref[0]
        anchor = None
        for c in range(D_FF // FFN_CHUNK):
            cs = slice(c * FFN_CHUNK, (c + 1) * FFN_CHUNK)
            cwc = cw[:, cs] if anchor is None else cw[:, cs] + anchor
            conv = _conv3(_dot(hh, wu_ref[0, :, cs]), seg, cwc, tile=i - 2)
            val = _dot(h, wu_ref[0, :, D_FF + c * FFN_CHUNK:D_FF + (c + 1) * FFN_CHUNK])
            act_scr[:, cs] = (conv / (1.0 + jnp.exp(-conv)) * val).astype(BF16)
            anchor = prepare(pieces[c]) if c < len(pieces) else None
        y2 = _dot(act_scr[...], wd_ref[0])
        out_projection()
        o_ref[...] = xm_scr[rd] + g2_ref[0] * _rms(y2, gn3_ref[0])


def _post(a2, b2, stream, tabs, layer, w_out, mix_layer, w_up, conv_w, w_down):
    x2 = stream.x
    rows, d = x2.shape
    tm = ROW_TILE
    seg = stream.seg
    assert seg % tm == 0 or tm % seg == 0
    n = rows // tm
    stage_a = lambda i: jnp.minimum(i, n - 1)
    stage_b = lambda i: jnp.clip(i - 1, 0, n - 1)
    stage_c = lambda i: jnp.maximum(i - 2, 0)
    row = stream.mod_row(min(tm, seg))
    mod = lambda chunk, stage: _mod_spec(layer, chunk, row, stage)
    return pl.pallas_call(
        functools.partial(_post_kernel, seg=seg),
        grid=(n + 2,),
        in_specs=(_halo_specs(tm, rows, a2.shape[1], stage_a) + _halo_specs(tm, rows, b2.shape[1], stage_a)
                  + _halo_specs(tm, rows, d, stage_b)
                  + [_layer_spec(w_out, mix_layer), _gain_spec(layer, 1), mod(2, stage_b), _gain_spec(layer, 2),
                     mod(4, stage_b), mod(3, stage_b), mod(5, stage_c), _gain_spec(layer, 3),
                     _layer_spec(w_up, layer), _layer_spec(conv_w, layer), _layer_spec(w_down, layer)]),
        out_specs=pl.BlockSpec((tm, d), lambda i: (stage_c(i), 0)),
        out_shape=jax.ShapeDtypeStruct((rows, d), F32),
        scratch_shapes=[pltpu.VMEM((tm + 2 * HALO, d), F32), pltpu.VMEM((2, tm + 2 * HALO, d), BF16),
                        pltpu.VMEM((2, tm, d), F32), pltpu.VMEM((tm, D_FF), BF16)],
        compiler_params=_params("arbitrary"),
        name="post",
    )(a2, a2, a2, b2, b2, b2, x2, x2, x2, w_out, tabs.gains, tabs.mods, tabs.gains, tabs.mods, tabs.mods,
      tabs.mods, tabs.gains, w_up, conv_w, w_down)


def _rope_tables(seq):
    half = HEAD_DIM // 2
    nf = half // 2
    inv = ROPE_BASE ** (-np.arange(nf, dtype=np.float64) / nf)
    t = np.arange(seq)
    lane = np.arange(LANES)
    in_head = lane % HEAD_DIM
    pos = np.where((in_head // half)[None, :] == 0, (t // GRID_W)[:, None], (t % GRID_W)[:, None])
    ang = pos * inv[lane % nf][None, :]
    first = ((lane % half) < nf)[None, :]
    cos = np.cos(ang)
    sin_a = np.where(first, -np.sin(ang), 0.0)
    sin_b = np.where(first, 0.0, np.sin(ang))
    return tuple(jnp.asarray(a, F32) for a in (cos, sin_a, sin_b))


def _dft_tables(n, rows=None):
    k = np.arange(n if rows is None else rows)
    ang = 2.0 * np.pi * ((k[:, None] * np.arange(n)[None, :]) % n) / n
    return jnp.asarray(np.cos(ang), F32).astype(BF16), jnp.asarray(np.sin(ang), F32).astype(BF16)


def _seq_dft_tables(n):
    half = n // 2
    flip = np.zeros((half, half), np.float32)
    flip[np.arange(half), (half - np.arange(half)) % half] = 1.0
    return _dft_tables(n, half) + (jnp.asarray(flip).astype(BF16),)


def kernel(x, c, ctx, c_ctx, mod_w, mod_b, norm_g, att_w_in, att_sink, sconv_w, att_w_out, mix_w_in,
           pool_w_grp, pool_scale, mix_w_out, ffn_w_up, ffn_conv, ffn_w_down):
    bsz, seq, d = x.shape
    ctx_len = ctx.shape[1]
    assert d == D_MODEL and seq % ROW_TILE == 0 and (bsz * ctx_len) % ROW_TILE == 0 and ROW_TILE % ctx_len == 0
    assert bsz < MOD_ROWS and norm_g.shape[1] == N_GAINS

    cc = jnp.concatenate([c, c_ctx[None, :], jnp.zeros((MOD_ROWS - bsz - 1, d), F32)], axis=0)
    mods = _modulation(cc, mod_w, mod_b)
    tabs = _Tables(mods.reshape(DEPTH * MOD_ROWS, 1, 6 * d), norm_g.reshape(DEPTH * N_GAINS, 1, d))

    rope = _rope_tables(seq)
    chan_cs = jnp.concatenate(_dft_tables(FFT_GROUP), axis=1)
    seq_tabs = {seq: _seq_dft_tables(seq), ctx_len: _seq_dft_tables(ctx_len)}

    w_att_in, w_att_out = att_w_in.astype(BF16), att_w_out.astype(BF16)
    w_mix_in, w_mix_out = mix_w_in.astype(BF16), mix_w_out.astype(BF16)
    w_up, w_down = ffn_w_up.astype(BF16), ffn_w_down.astype(BF16)
    g4 = pool_w_grp.astype(BF16)
    zero = jnp.zeros_like(g4[:, 0])
    w_grp = jnp.stack([jnp.concatenate([jnp.concatenate([g4[:, 2 * j], zero], axis=2),
                                        jnp.concatenate([zero, g4[:, 2 * j + 1]], axis=2)], axis=1)
                       for j in range(N_POOL_GROUPS // 2)], axis=1)
    p_scale = pool_scale.reshape(-1, 1, POOL_DIM)

    xs = x.reshape(bsz * seq, d)
    hc = ctx.reshape(bsz * ctx_len, d)

    for i in range(DEPTH):
        even = i % 2 == 0
        ctx_live = any(j % 2 == 0 for j in range(i + 1, DEPTH))
        lat = _Stream(xs, seq, None)
        con = _Stream(hc, ctx_len, bsz)
        ffn = (w_up, ffn_conv, w_down)
        if even:
            e = i // 2
            q, kt, v, sx = _in_even(lat, tabs, i, e, w_att_in, sconv_w, rope)
            if ctx_live:
                qc, kct, vc, sxc = _in_even(con, tabs, i, e, w_att_in, sconv_w, None)
            else:
                kct, vc = _in_kv(con, tabs, i, e, w_att_in)
            vc3 = vc.reshape(bsz, ctx_len, -1)
            ax = _attention(att_sink, e, q.reshape(bsz, seq, -1), kt, v.reshape(bsz, seq, -1), kct, vc3, True)
            xs = _post(ax.reshape(bsz * seq, -1), sx, lat, tabs, i, w_att_out, e, *ffn)
            if ctx_live:
                ac = _attention(att_sink, e, qc.reshape(bsz, ctx_len, -1), None, None, kct, vc3, False)
                hc = _post(ac.reshape(bsz * ctx_len, -1), sxc, con, tabs, i, w_att_out, e, *ffn)
        else:
            o = i // 2
            outs = []
            for stream in (lat, con) if ctx_live else (lat,):
                yp, ab = _in_odd(stream, tabs, i, o, w_mix_in, chan_cs, w_grp, p_scale)
                yf = _seq_dft(ab.reshape(bsz, stream.seg, -1), seq_tabs[stream.seg])
                outs.append(_post(yp, yf.reshape(bsz * stream.seg, -1), stream, tabs, i, w_mix_out, o, *ffn))
            xs = outs[0]
            if ctx_live:
                hc = outs[1]
    return xs.reshape(bsz, seq, d)
```

```python
import functools
from typing import NamedTuple

import numpy as np
import jax
import jax.numpy as jnp
from jax import lax
from jax.experimental import pallas as pl
from jax.experimental.pallas import tpu as pltpu

F32 = jnp.float32
BF16 = jnp.bfloat16

D_MODEL = 1024
DEPTH = 4
GRID_W = 64
EPS = 1e-6
N_Q_HEADS = 8
N_KV_HEADS = 2
HEAD_DIM = 64
WINDOW = 128
BLOCK = 128
ROPE_BASE = 10000.0
SCONV_DIM = 512
POOL_DIM = 512
N_POOL_GROUPS = 4
POOL_GROUP = POOL_DIM // N_POOL_GROUPS
POOL_WINDOWS = (2, 4, 8, 16)
FFT_DIM = 512
N_FFT_GROUPS = 4
FFT_GROUP = FFT_DIM // N_FFT_GROUPS
D_FF = 2816
Q_DIM = N_Q_HEADS * HEAD_DIM
KV_DIM = N_KV_HEADS * HEAD_DIM
KV_TILES = 2 * KV_DIM * 2

LANES = 128
MASK_VALUE = -1e30
LOG2E = 1.4426950408889634
ATTN_QBLOCKS = 16
VMEM_LIMIT = 56 * 1024 * 1024
ROW_TILE = 512
IN_TILE = 1024
HALO = 16
FFN_CHUNK = 256
POST_PIECES = 8
MOD_ROWS = 16
N_GAINS = 4

assert HALO >= max(POOL_WINDOWS) and all(w & (w - 1) == 0 for w in POOL_WINDOWS)


def _params(*sem):
    return pltpu.CompilerParams(dimension_semantics=sem, vmem_limit_bytes=VMEM_LIMIT)


def _rms(x, g):
    ms = jnp.mean(x * x, axis=-1, keepdims=True)
    return x * lax.rsqrt(ms + EPS) * g


def _dot(a, b):
    return jnp.dot(a, b, preferred_element_type=F32)


class _Stream(NamedTuple):
    x: jax.Array
    seg: int
    ctx_row: int | None

    def tile(self):
        return min(IN_TILE, self.seg)

    def mod_row(self, tm):
        if self.ctx_row is not None:
            row = self.ctx_row
            return lambda tile: row
        per_seg = self.seg // tm
        return lambda tile: tile // per_seg


class _Tables(NamedTuple):
    mods: jax.Array
    gains: jax.Array


def _layer_spec(a, layer):
    nd = a.ndim
    return pl.BlockSpec((1,) + a.shape[1:], lambda *_: (layer,) + (0,) * (nd - 1), pipeline_mode=pl.Buffered(1))


def _const_spec(a):
    return pl.BlockSpec(a.shape, lambda *_: (0,) * a.ndim, pipeline_mode=pl.Buffered(1))


def _gain_spec(layer, j):
    return pl.BlockSpec((1, 1, D_MODEL), lambda *_: (layer * N_GAINS + j, 0, 0))


def _mod_spec(layer, chunk, row, tile=lambda i: i):
    return pl.BlockSpec((1, 1, D_MODEL), lambda i: (layer * MOD_ROWS + row(tile(i)), 0, chunk))


def _halo_specs(tm, rows, width, tile=lambda i: i):
    hb = tm // HALO
    return [
        pl.BlockSpec((HALO, width), lambda i: (jnp.maximum(tile(i) * hb - 1, 0), 0)),
        pl.BlockSpec((tm, width), lambda i: (tile(i), 0)),
        pl.BlockSpec((HALO, width), lambda i: (jnp.minimum((tile(i) + 1) * hb, rows // HALO - 1), 0)),
    ]


def _ext(prev_ref, main_ref, next_ref):
    return jnp.concatenate([prev_ref[...], main_ref[...], next_ref[...]], axis=0)


def _tile_pos(tm, seg):
    return (pl.program_id(0) * tm + lax.broadcasted_iota(jnp.int32, (tm, 1), 0)) % seg


def _seg_ext(x_ext, seg, tile=None):
    tm = x_ext.shape[0] - 2 * HALO
    i = pl.program_id(0) if tile is None else tile
    top_ok = (i * tm) % seg != 0
    bot_ok = ((i + 1) * tm) % seg != 0
    top, bot = x_ext[:HALO], x_ext[HALO + tm:]
    return jnp.concatenate([jnp.where(top_ok, top, jnp.zeros_like(top)), x_ext[HALO:HALO + tm],
                            jnp.where(bot_ok, bot, jnp.zeros_like(bot))], axis=0)


def _roll_rows(x_ext, off):
    rows = x_ext.shape[0]
    return pltpu.roll(x_ext, (-off) % rows, 0)


def _conv3(x_ext, seg, w, tile=None):
    tm = x_ext.shape[0] - 2 * HALO
    main = slice(HALO, HALO + tm)
    xm = _seg_ext(x_ext, seg, tile)
    prev, nxt = _roll_rows(xm, -1)[main], _roll_rows(xm, 1)[main]
    if seg < tm:
        pos = lax.broadcasted_iota(jnp.int32, (tm, 1), 0) % seg
        prev = jnp.where(pos == 0, 0.0, prev)
        nxt = jnp.where(pos == seg - 1, 0.0, nxt)
    return prev * w[0:1] + x_ext[main] * w[1:2] + nxt * w[2:3]


def _zero_anchor(v):
    fold = sum(v[r:r + 8] for r in range(0, v.shape[0], 8))
    fold = sum(fold[:, l:l + LANES] for l in range(0, fold.shape[1], LANES))
    return jnp.minimum(jnp.abs(fold[0:1]), 0.0)


def _modulated(x, gn_ref, sc_ref, sh_ref):
    return (_rms(x, gn_ref[0]) * (1.0 + sc_ref[0]) + sh_ref[0]).astype(BF16)


def _stream_specs(stream, tabs, layer):
    tm = stream.tile()
    rows, d = stream.x.shape
    row = stream.mod_row(tm)
    specs = _halo_specs(tm, rows, d) + [_gain_spec(layer, 0), _mod_spec(layer, 1, row), _mod_spec(layer, 0, row)]
    return specs, [stream.x] * 3 + [tabs.gains, tabs.mods, tabs.mods]


def _cast_rider(params, n_steps):
    in_specs, out_specs, out_shape = [], [], []
    for a, layer in params:
        r, c = a.shape[1:]
        chunk = r // n_steps
        assert chunk * n_steps == r and chunk % 16 == 0
        in_specs.append(pl.BlockSpec((1, chunk, c), lambda i, layer=layer: (layer, i, 0)))
        out_specs.append(pl.BlockSpec((1, chunk, c), lambda i: (0, i, 0)))
        out_shape.append(jax.ShapeDtypeStruct((1, r, c), BF16))
    return in_specs, out_specs, out_shape


def _run_casts(in_refs, out_refs):
    for src, dst in zip(in_refs, out_refs):
        dst[...] = src[...].astype(BF16)


def _mod_kernel(c_ref, w_ref, b_ref, o_ref):
    c = c_ref[...]
    s = (c / (1.0 + jnp.exp(-c))).astype(BF16)
    o_ref[0] = _dot(s, w_ref[0].astype(BF16)) + b_ref[0]


def _modulation(cc, mod_w, mod_b):
    depth, d, n = mod_w.shape
    tn = 1024
    return pl.pallas_call(
        _mod_kernel,
        grid=(depth, n // tn),
        in_specs=[
            pl.BlockSpec((MOD_ROWS, d), lambda i, j: (0, 0)),
            pl.BlockSpec((1, d, tn), lambda i, j: (i, 0, j)),
            pl.BlockSpec((1, 1, tn), lambda i, j: (i, 0, j)),
        ],
        out_specs=pl.BlockSpec((1, MOD_ROWS, tn), lambda i, j: (i, 0, j)),
        out_shape=jax.ShapeDtypeStruct((depth, MOD_ROWS, n), F32),
        compiler_params=_params("arbitrary", "arbitrary"),
        name="modulation",
    )(cc, mod_w, mod_b.reshape(depth, 1, n))


def _rope128(x, cos, sin_a, sin_b):
    return x * cos + pltpu.roll(x, LANES - 16, 1) * sin_a + pltpu.roll(x, 16, 1) * sin_b


def _write_kv(k, v, k_ref, v_ref):
    lo = lax.broadcasted_iota(jnp.int32, (1, LANES), 1) < HEAD_DIM
    vr = pltpu.roll(v, HEAD_DIM, 1)
    zero = jnp.zeros_like(v)
    v_ref[...] = jnp.concatenate(
        [jnp.where(lo, v, zero), jnp.where(lo, zero, vr), jnp.where(lo, vr, zero), jnp.where(lo, zero, v)],
        axis=1).astype(BF16)
    kt = k.T
    top = lax.broadcasted_iota(jnp.int32, (LANES, 1), 0) < HEAD_DIM
    ktr = pltpu.roll(kt, HEAD_DIM, 0)
    zero = jnp.zeros_like(kt)
    k_ref[...] = jnp.concatenate(
        [jnp.where(top, kt, zero), jnp.where(top, zero, ktr), jnp.where(top, ktr, zero), jnp.where(top, zero, kt)],
        axis=0).astype(BF16)


def _in_even_kernel(xp_ref, x_ref, xn_ref, gn_ref, sc_ref, sh_ref, w_ref, cw_ref, *rest, rope, seg, n_cast):
    if rope:
        cos_ref, sa_ref, sb_ref = rest[:3]
        rest = rest[3:]
    cast_in, rest = rest[:n_cast], rest[n_cast:]
    q_ref, k_ref, v_ref, s_ref = rest[:4]
    _run_casts(cast_in, rest[4:])
    tm = x_ref.shape[0]
    hh = _modulated(_ext(xp_ref, x_ref, xn_ref), gn_ref, sc_ref, sh_ref)
    h = hh[HALO:HALO + tm]
    o = Q_DIM + 2 * KV_DIM
    cz = (_dot(hh, w_ref[0, :, o + SCONV_DIM:o + 2 * SCONV_DIM])
          * _dot(hh, w_ref[0, :, o + 2 * SCONV_DIM:o + 3 * SCONV_DIM]))
    q = _dot(h, w_ref[0, :, 0:Q_DIM])
    kv = _dot(h, w_ref[0, :, Q_DIM:Q_DIM + 2 * KV_DIM])
    k, v = kv[:, 0:KV_DIM], kv[:, KV_DIM:2 * KV_DIM]
    b = _dot(h, w_ref[0, :, o:o + SCONV_DIM])
    s_ref[...] = (b * _conv3(cz, seg, cw_ref[0])).astype(BF16)
    if rope:
        cos, sa, sb = cos_ref[...], sa_ref[...], sb_ref[...]
        k = _rope128(k, cos, sa, sb)
        q = jnp.concatenate(
            [_rope128(q[:, LANES * j:LANES * (j + 1)], cos, sa, sb) for j in range(Q_DIM // LANES)], axis=1)
    q_ref[...] = (q * (HEAD_DIM ** -0.5 * LOG2E)).astype(BF16)
    _write_kv(k, v, k_ref, v_ref)


def _in_even(stream, tabs, layer, e, w_in, conv_w, rope_tables, casts=()):
    rows = stream.x.shape[0]
    tm = stream.tile()
    per_seg = stream.seg // tm
    rope = rope_tables is not None
    row = lambda i: (i, 0)
    in_specs, args = _stream_specs(stream, tabs, layer)
    in_specs += [_layer_spec(w_in, e), _layer_spec(conv_w, e)]
    args += [w_in, conv_w]
    if rope:
        in_specs += [pl.BlockSpec((tm, LANES), lambda i: (i % per_seg, 0))] * 3
        args += list(rope_tables)
    cast_in, cast_out, cast_shape = _cast_rider(casts, rows // tm)
    return pl.pallas_call(
        functools.partial(_in_even_kernel, rope=rope, seg=stream.seg, n_cast=len(casts)),
        grid=(rows // tm,),
        in_specs=in_specs + cast_in,
        out_specs=[pl.BlockSpec((tm, Q_DIM), row), pl.BlockSpec((KV_TILES, tm), lambda i: (0, i)),
                   pl.BlockSpec((tm, KV_TILES), row), pl.BlockSpec((tm, SCONV_DIM), row)] + cast_out,
        out_shape=[jax.ShapeDtypeStruct((rows, Q_DIM), BF16), jax.ShapeDtypeStruct((KV_TILES, rows), BF16),
                   jax.ShapeDtypeStruct((rows, KV_TILES), BF16), jax.ShapeDtypeStruct((rows, SCONV_DIM), BF16)]
        + cast_shape,
        compiler_params=_params("arbitrary"),
        name="in_even",
    )(*args, *[a for a, _ in casts])


def _in_kv_kernel(x_ref, gn_ref, sc_ref, sh_ref, w_ref, k_ref, v_ref):
    h = _modulated(x_ref[...], gn_ref, sc_ref, sh_ref)
    kv = _dot(h, w_ref[0, :, Q_DIM:Q_DIM + 2 * KV_DIM])
    _write_kv(kv[:, 0:KV_DIM], kv[:, KV_DIM:2 * KV_DIM], k_ref, v_ref)


def _in_kv(stream, tabs, layer, e, w_in):
    rows, d = stream.x.shape
    tm = stream.tile()
    row = lambda i: (i, 0)
    mrow = stream.mod_row(tm)
    return pl.pallas_call(
        _in_kv_kernel,
        grid=(rows // tm,),
        in_specs=[pl.BlockSpec((tm, d), row), _gain_spec(layer, 0), _mod_spec(layer, 1, mrow),
                  _mod_spec(layer, 0, mrow), _layer_spec(w_in, e)],
        out_specs=[pl.BlockSpec((KV_TILES, tm), lambda i: (0, i)), pl.BlockSpec((tm, KV_TILES), row)],
        out_shape=[jax.ShapeDtypeStruct((KV_TILES, rows), BF16), jax.ShapeDtypeStruct((rows, KV_TILES), BF16)],
        compiler_params=_params("arbitrary"),
        name="in_kv",
    )(stream.x, tabs.gains, tabs.mods, tabs.mods, w_in)


def _in_odd_kernel(xp_ref, x_ref, xn_ref, gn_ref, sc_ref, sh_ref, w_ref, ccs_ref, wg_ref, ps_ref,
                   *rest, seg, n_cast):
    cast_in, rest = rest[:n_cast], rest[n_cast:]
    p_ref, ab_ref = rest[:2]
    _run_casts(cast_in, rest[2:])
    tm = x_ref.shape[0]
    hh = _modulated(_ext(xp_ref, x_ref, xn_ref), gn_ref, sc_ref, sh_ref)
    h = hh[HALO:HALO + tm]
    up = _seg_ext(_dot(hh, w_ref[0, :, 0:POOL_DIM]), seg)
    uf = _dot(h, w_ref[0, :, POOL_DIM:POOL_DIM + FFT_DIM]).astype(BF16)
    ccs = ccs_ref[...]
    both = [_dot(uf[:, g * FFT_GROUP:(g + 1) * FFT_GROUP], ccs) for g in range(N_FFT_GROUPS)]
    ab_ref[...] = jnp.concatenate([r[:, 0:FFT_GROUP] for r in both] + [r[:, FFT_GROUP:] for r in both],
                                  axis=1).astype(BF16)
    pos = _tile_pos(tm, seg)
    pooled = []
    for gi, w in enumerate(POOL_WINDOWS):
        ug = up[:, gi * POOL_GROUP:(gi + 1) * POOL_GROUP]
        back, fwd = w // 2, w - w // 2
        run, span = ug, 1
        while span < w:
            run = run + _roll_rows(run, span)
            span *= 2
        total = _roll_rows(run, -back)[HALO:HALO + tm]
        count = (jnp.minimum(pos + fwd, seg) - jnp.maximum(pos - back, 0)).astype(F32)
        pooled.append((total / count - ug[HALO:HALO + tm]).astype(BF16))
    outs = [_dot(jnp.concatenate(pooled[2 * j:2 * j + 2], axis=1), wg_ref[0, j]) for j in range(N_POOL_GROUPS // 2)]
    p_ref[...] = (jnp.concatenate(outs, axis=1) * ps_ref[0]).astype(BF16)


def _in_odd(stream, tabs, layer, o, w_in, ccs, wg, ps, casts=()):
    rows = stream.x.shape[0]
    tm = stream.tile()
    row = lambda i: (i, 0)
    in_specs, args = _stream_specs(stream, tabs, layer)
    cast_in, cast_out, cast_shape = _cast_rider(casts, rows // tm)
    return pl.pallas_call(
        functools.partial(_in_odd_kernel, seg=stream.seg, n_cast=len(casts)),
        grid=(rows // tm,),
        in_specs=(in_specs + [_layer_spec(w_in, o), _const_spec(ccs), _layer_spec(wg, o), _layer_spec(ps, o)]
                  + cast_in),
        out_specs=[pl.BlockSpec((tm, POOL_DIM), row), pl.BlockSpec((tm, 2 * FFT_DIM), row)] + cast_out,
        out_shape=[jax.ShapeDtypeStruct((rows, POOL_DIM), BF16), jax.ShapeDtypeStruct((rows, 2 * FFT_DIM), BF16)]
        + cast_shape,
        compiler_params=_params("arbitrary"),
        name="in_odd",
    )(*args, w_in, ccs, wg, ps, *[a for a, _ in casts])


def _attn_kernel(sink_ref, q_ref, *rest, window, qb, nblocks, e):
    if window:
        ktp, ktc, ktn, vp, vc, vn, kxt, vx, o_ref = rest
    else:
        kxt, vx, o_ref = rest
    n = pl.program_id(1)
    rows = 2 * BLOCK
    first_pair = lax.broadcasted_iota(jnp.int32, (rows, 1), 0) < BLOCK
    lo = lax.broadcasted_iota(jnp.int32, (1, LANES), 1) < HEAD_DIM
    if window:
        qi = lax.broadcasted_iota(jnp.int32, (rows, BLOCK), 0) % BLOCK
        j = lax.broadcasted_iota(jnp.int32, (rows, BLOCK), 1)
        tri_prev = jnp.where(j >= qi, 0.0, MASK_VALUE)
        tri_next = jnp.where(j <= qi, 0.0, MASK_VALUE)
        bias = []
        for t in range(qb):
            blk = n * qb + t
            bias.append((tri_prev + jnp.where(blk >= 1, 0.0, MASK_VALUE),
                         tri_next + jnp.where(blk <= nblocks - 2, 0.0, MASK_VALUE)))

    def lane_halves(h):
        c0 = 2 * LANES * h
        return slice(c0, c0 + LANES), slice(c0 + LANES, c0 + 2 * LANES)

    def keys_values(h, t):
        kparts, vparts = [], []
        for sl in lane_halves(h):
            if window:
                kall = jnp.concatenate([ktp[sl, :], ktc[sl, :], ktn[sl, :]], axis=1)
                vall = jnp.concatenate([vp[0, :, sl], vc[0, :, sl], vn[0, :, sl]], axis=0)
                kparts.append(kall[:, t * BLOCK:(t + 3) * BLOCK])
                vparts.append(vall[t * BLOCK:(t + 3) * BLOCK])
            kparts.append(kxt[sl, :])
            vparts.append(vx[0, :, sl])
        return jnp.concatenate(kparts, axis=1), jnp.concatenate(vparts, axis=0)

    def logits(h, t):
        rs = slice(t * BLOCK, (t + 1) * BLOCK)
        lo_sl, hi_sl = lane_halves(h)
        qp = jnp.concatenate([q_ref[0, rs, lo_sl], q_ref[0, rs, hi_sl]], axis=0)
        return _dot(qp, keys_values(h, t)[0])

    def finish(h, t, s):
        vst = keys_values(h, t)[1]
        nkeys = vst.shape[0] // 2
        probs, inv = [], []
        for half in range(2):
            tiles = [s[:, half * nkeys + c * LANES:half * nkeys + (c + 1) * LANES] for c in range(nkeys // LANES)]
            if window:
                tiles[0] = tiles[0] + bias[t][0]
                tiles[2] = tiles[2] + bias[t][1]
            head = 4 * h + half
            sink_col = jnp.where(first_pair, sink_ref[e, head], sink_ref[e, head + 2]) * LOG2E
            m = jnp.maximum(jnp.max(functools.reduce(jnp.maximum, tiles), axis=-1, keepdims=True), sink_col)
            es = [jnp.exp2(tl - m) for tl in tiles]
            den = jnp.sum(functools.reduce(jnp.add, es), axis=-1, keepdims=True) + jnp.exp2(sink_col - m)
            probs += [e_.astype(BF16) for e_ in es]
            inv.append(1.0 / den)
        o = _dot(jnp.concatenate(probs, axis=1), vst) * jnp.where(lo, inv[0], inv[1])
        rs = slice(t * BLOCK, (t + 1) * BLOCK)
        lo_sl, hi_sl = lane_halves(h)
        o_ref[0, rs, lo_sl] = o[:BLOCK].astype(o_ref.dtype)
        o_ref[0, rs, hi_sl] = o[BLOCK:].astype(o_ref.dtype)

    units = [(h, t) for h in range(N_KV_HEADS) for t in range(qb)]
    s_next = logits(*units[0])
    for u, unit in enumerate(units):
        s_cur = s_next
        if u + 1 < len(units):
            s_next = logits(*units[u + 1])
        finish(*unit, s_cur)


def _attention(sinks, e, q, kt, v, kxt, vx, window):
    b, t, _ = q.shape
    l = vx.shape[1]
    nb = t // BLOCK
    qb = ATTN_QBLOCKS if window else nb
    w = vx.shape[2]
    steps = nb // qb
    cur = lambda i, n: (i, n, 0)
    in_specs = [pl.BlockSpec(memory_space=pltpu.SMEM), pl.BlockSpec((1, qb * BLOCK, Q_DIM), cur)]
    args = [sinks, q]
    if window:
        pb = lambda n: jnp.maximum(n * qb - 1, 0)
        nx = lambda n: jnp.minimum(n * qb + qb, nb - 1)
        in_specs += [
            pl.BlockSpec((w, BLOCK), lambda i, n: (0, i * nb + pb(n))),
            pl.BlockSpec((w, qb * BLOCK), lambda i, n: (0, i * steps + n)),
            pl.BlockSpec((w, BLOCK), lambda i, n: (0, i * nb + nx(n))),
            pl.BlockSpec((1, BLOCK, w), lambda i, n: (i, pb(n), 0)),
            pl.BlockSpec((1, qb * BLOCK, w), cur),
            pl.BlockSpec((1, BLOCK, w), lambda i, n: (i, nx(n), 0)),
        ]
        args += [kt, kt, kt, v, v, v]
    in_specs += [pl.BlockSpec((w, l), lambda i, n: (0, i)), pl.BlockSpec((1, l, w), lambda i, n: (i, 0, 0))]
    args += [kxt, vx]
    return pl.pallas_call(
        functools.partial(_attn_kernel, window=window, qb=qb, nblocks=nb, e=e),
        grid=(b, steps),
        in_specs=in_specs,
        out_specs=pl.BlockSpec((1, qb * BLOCK, Q_DIM), cur),
        out_shape=jax.ShapeDtypeStruct((b, t, Q_DIM), BF16),
        compiler_params=_params("arbitrary", "arbitrary"),
        name="attention",
    )(*args)


def _seq_dft_kernel(ct_ref, st_ref, flip_ref, ab_ref, o_ref, *, norm):
    t = ab_ref.shape[1]
    half = t // 2
    a = ab_ref[0, :, 0:FFT_DIM]
    b = ab_ref[0, :, FFT_DIM:2 * FFT_DIM]
    p = _dot(ct_ref[...], a)
    q = _dot(st_ref[...], b)
    o_ref[0, 0:half] = ((p - q) * norm).astype(o_ref.dtype)
    sign = 1.0 - 2.0 * (lax.broadcasted_iota(jnp.int32, (t, 1), 0) % 2).astype(F32)
    nyquist = jnp.sum(a.astype(F32) * sign, axis=0, keepdims=True)
    first = lax.broadcasted_iota(jnp.int32, (half, 1), 0) == 0
    mirrored = (jnp.where(first, nyquist, p + q) * norm).astype(BF16)
    o_ref[0, half:t] = _dot(flip_ref[...], mirrored).astype(o_ref.dtype)


def _seq_dft(ab3, tables):
    b, t, _ = ab3.shape
    return pl.pallas_call(
        functools.partial(_seq_dft_kernel, norm=float((t * FFT_GROUP) ** -0.5)),
        grid=(b,),
        in_specs=[_const_spec(a) for a in tables] + [pl.BlockSpec((1, t, 2 * FFT_DIM), lambda i: (i, 0, 0))],
        out_specs=pl.BlockSpec((1, t, FFT_DIM), lambda i: (i, 0, 0)),
        out_shape=jax.ShapeDtypeStruct((b, t, FFT_DIM), BF16),
        compiler_params=_params("arbitrary"),
        name="seq_dft",
    )(*tables, ab3)


def _post_kernel(ap_ref, a_ref, an_ref, bp_ref, b_ref, bn_ref, xp_ref, x_ref, xn_ref,
                 wo_ref, gn1_ref, g1_ref, gn2_ref, sc_ref, sh_ref, g2_ref, gn3_ref,
                 wu_ref, cw_ref, wd_ref, o_ref, y_scr, hh_scr, xm_scr, act_scr, *, seg):
    i = pl.program_id(0)
    tm = x_ref.shape[0]
    main = slice(HALO, HALO + tm)
    rd = i % 2
    wr = 1 - rd
    piece_rows = tm // POST_PIECES
    pieces = ([(0, xp_ref, 0, HALO)]
              + [(HALO + k * piece_rows, x_ref, k * piece_rows, piece_rows) for k in range(POST_PIECES)]
              + [(HALO + tm, xn_ref, 0, HALO)])
    ka = a_ref.shape[1]

    def prepare(piece):
        lo, ref, off, nrows = piece
        xm = ref[off:off + nrows] + g1_ref[0] * _rms(y_scr[lo:lo + nrows], gn1_ref[0])
        hn = _rms(xm, gn2_ref[0]) * (1.0 + sc_ref[0]) + sh_ref[0]
        hh_scr[wr, lo:lo + nrows] = hn.astype(BF16)
        if ref is x_ref:
            xm_scr[wr, off:off + nrows] = xm
        return jnp.concatenate([_zero_anchor(hn)] * (FFN_CHUNK // LANES), axis=1)

    def out_projection():
        y_scr[...] = (_dot(_ext(ap_ref, a_ref, an_ref), wo_ref[0, 0:ka, :])
                      + _dot(_ext(bp_ref, b_ref, bn_ref), wo_ref[0, ka:, :]))

    @pl.when(i == 0)
    def _():
        y_scr[...] = jnp.zeros_like(y_scr)

    @pl.when(i < 2)
    def _():
        for piece in pieces:
            prepare(piece)
        out_projection()

    @pl.when(i >= 2)
    def _():
        hh = hh_scr[rd]
        h = hh[main]
        cw = cw_ref[0]
        anchor = None
        for c in range(D_FF // FFN_CHUNK):
            cs = slice(c * FFN_CHUNK, (c + 1) * FFN_CHUNK)
            cwc = cw[:, cs] if anchor is None else cw[:, cs] + anchor
            conv = _conv3(_dot(hh, wu_ref[0, :, cs]), seg, cwc, tile=i - 2)
            val = _dot(h, wu_ref[0, :, D_FF + c * FFN_CHUNK:D_FF + (c + 1) * FFN_CHUNK])
            act_scr[:, cs] = (conv / (1.0 + jnp.exp(-conv)) * val).astype(BF16)
            anchor = prepare(pieces[c]) if c < len(pieces) else None
        y2 = _dot(act_scr[...], wd_ref[0])
        out_projection()
        o_ref[...] = xm_scr[rd] + g2_ref[0] * _rms(y2, gn3_ref[0])


def _post(a2, b2, stream, tabs, layer, w_out, w_up, conv_w, w_down):
    x2 = stream.x
    rows, d = x2.shape
    tm = ROW_TILE
    seg = stream.seg
    assert seg % tm == 0 or tm % seg == 0
    n = rows // tm
    stage_a = lambda i: jnp.minimum(i, n - 1)
    stage_b = lambda i: jnp.clip(i - 1, 0, n - 1)
    stage_c = lambda i: jnp.maximum(i - 2, 0)
    row = stream.mod_row(min(tm, seg))
    mod = lambda chunk, stage: _mod_spec(layer, chunk, row, stage)
    return pl.pallas_call(
        functools.partial(_post_kernel, seg=seg),
        grid=(n + 2,),
        in_specs=(_halo_specs(tm, rows, a2.shape[1], stage_a) + _halo_specs(tm, rows, b2.shape[1], stage_a)
                  + _halo_specs(tm, rows, d, stage_b)
                  + [_layer_spec(w_out, 0), _gain_spec(layer, 1), mod(2, stage_b), _gain_spec(layer, 2),
                     mod(4, stage_b), mod(3, stage_b), mod(5, stage_c), _gain_spec(layer, 3),
                     _layer_spec(w_up, 0), _layer_spec(conv_w, layer), _layer_spec(w_down, 0)]),
        out_specs=pl.BlockSpec((tm, d), lambda i: (stage_c(i), 0)),
        out_shape=jax.ShapeDtypeStruct((rows, d), F32),
        scratch_shapes=[pltpu.VMEM((tm + 2 * HALO, d), F32), pltpu.VMEM((2, tm + 2 * HALO, d), BF16),
                        pltpu.VMEM((2, tm, d), F32), pltpu.VMEM((tm, D_FF), BF16)],
        compiler_params=_params("arbitrary"),
        name="post",
    )(a2, a2, a2, b2, b2, b2, x2, x2, x2, w_out, tabs.gains, tabs.mods, tabs.gains, tabs.mods, tabs.mods,
      tabs.mods, tabs.gains, w_up, conv_w, w_down)


def _rope_tables(seq):
    half = HEAD_DIM // 2
    nf = half // 2
    inv = ROPE_BASE ** (-np.arange(nf, dtype=np.float64) / nf)
    t = np.arange(seq)
    lane = np.arange(LANES)
    in_head = lane % HEAD_DIM
    pos = np.where((in_head // half)[None, :] == 0, (t // GRID_W)[:, None], (t % GRID_W)[:, None])
    ang = pos * inv[lane % nf][None, :]
    first = ((lane % half) < nf)[None, :]
    cos = np.cos(ang)
    sin_a = np.where(first, -np.sin(ang), 0.0)
    sin_b = np.where(first, 0.0, np.sin(ang))
    return tuple(jnp.asarray(a, F32) for a in (cos, sin_a, sin_b))


def _dft_tables(n, rows=None):
    k = np.arange(n if rows is None else rows)
    ang = 2.0 * np.pi * ((k[:, None] * np.arange(n)[None, :]) % n) / n
    return jnp.asarray(np.cos(ang), F32).astype(BF16), jnp.asarray(np.sin(ang), F32).astype(BF16)


def _seq_dft_tables(n):
    half = n // 2
    flip = np.zeros((half, half), np.float32)
    flip[np.arange(half), (half - np.arange(half)) % half] = 1.0
    return _dft_tables(n, half) + (jnp.asarray(flip).astype(BF16),)


def kernel(x, c, ctx, c_ctx, mod_w, mod_b, norm_g, att_w_in, att_sink, sconv_w, att_w_out, mix_w_in,
           pool_w_grp, pool_scale, mix_w_out, ffn_w_up, ffn_conv, ffn_w_down):
    bsz, seq, d = x.shape
    ctx_len = ctx.shape[1]
    assert d == D_MODEL and seq % ROW_TILE == 0 and (bsz * ctx_len) % ROW_TILE == 0 and ROW_TILE % ctx_len == 0
    assert bsz < MOD_ROWS and norm_g.shape[1] == N_GAINS

    cc = jnp.concatenate([c, c_ctx[None, :], jnp.zeros((MOD_ROWS - bsz - 1, d), F32)], axis=0)
    mods = _modulation(cc, mod_w, mod_b)
    tabs = _Tables(mods.reshape(DEPTH * MOD_ROWS, 1, 6 * d), norm_g.reshape(DEPTH * N_GAINS, 1, d))

    rope = _rope_tables(seq)
    chan_cs = jnp.concatenate(_dft_tables(FFT_GROUP), axis=1)
    seq_tabs = {seq: _seq_dft_tables(seq), ctx_len: _seq_dft_tables(ctx_len)}

    w_att_in, w_mix_in = att_w_in.astype(BF16), mix_w_in.astype(BF16)
    g4 = pool_w_grp.astype(BF16)
    zero = jnp.zeros_like(g4[:, 0])
    w_grp = jnp.stack([jnp.concatenate([jnp.concatenate([g4[:, 2 * j], zero], axis=2),
                                        jnp.concatenate([zero, g4[:, 2 * j + 1]], axis=2)], axis=1)
                       for j in range(N_POOL_GROUPS // 2)], axis=1)
    p_scale = pool_scale.reshape(-1, 1, POOL_DIM)

    xs = x.reshape(bsz * seq, d)
    hc = ctx.reshape(bsz * ctx_len, d)

    for i in range(DEPTH):
        even = i % 2 == 0
        ctx_live = any(j % 2 == 0 for j in range(i + 1, DEPTH))
        lat = _Stream(xs, seq, None)
        con = _Stream(hc, ctx_len, bsz)
        if even:
            e = i // 2
            casts = ((att_w_out, e), (ffn_w_up, i), (ffn_w_down, i))
            q, kt, v, sx, w_out, w_up, w_down = _in_even(lat, tabs, i, e, w_att_in, sconv_w, rope, casts)
            post = (w_out, w_up, ffn_conv, w_down)
            if ctx_live:
                qc, kct, vc, sxc = _in_even(con, tabs, i, e, w_att_in, sconv_w, None)
            else:
                kct, vc = _in_kv(con, tabs, i, e, w_att_in)
            vc3 = vc.reshape(bsz, ctx_len, -1)
            ax = _attention(att_sink, e, q.reshape(bsz, seq, -1), kt, v.reshape(bsz, seq, -1), kct, vc3, True)
            xs = _post(ax.reshape(bsz * seq, -1), sx, lat, tabs, i, *post)
            if ctx_live:
                ac = _attention(att_sink, e, qc.reshape(bsz, ctx_len, -1), None, None, kct, vc3, False)
                hc = _post(ac.reshape(bsz * ctx_len, -1), sxc, con, tabs, i, *post)
        else:
            o = i // 2
            casts = ((mix_w_out, o), (ffn_w_up, i), (ffn_w_down, i))
            yp, ab, w_out, w_up, w_down = _in_odd(lat, tabs, i, o, w_mix_in, chan_cs, w_grp, p_scale, casts)
            post = (w_out, w_up, ffn_conv, w_down)
            yf = _seq_dft(ab.reshape(bsz, seq, -1), seq_tabs[seq])
            if ctx_live:
                ypc, abc = _in_odd(con, tabs, i, o, w_mix_in, chan_cs, w_grp, p_scale)
                yfc = _seq_dft(abc.reshape(bsz, ctx_len, -1), seq_tabs[ctx_len])
            xs = _post(yp, yf.reshape(bsz * seq, -1), lat, tabs, i, *post)
            if ctx_live:
                hc = _post(ypc, yfc.reshape(bsz * ctx_len, -1), con, tabs, i, *post)
    return xs.reshape(bsz, seq, d)
```

```python
import functools
from typing import NamedTuple

import numpy as np
import jax
import jax.numpy as jnp
from jax import lax
from jax.experimental import pallas as pl
from jax.experimental.pallas import tpu as pltpu

F32 = jnp.float32
BF16 = jnp.bfloat16

D_MODEL = 1024
DEPTH = 4
GRID_W = 64
EPS = 1e-6
N_Q_HEADS = 8
N_KV_HEADS = 2
HEAD_DIM = 64
WINDOW = 128
BLOCK = 128
ROPE_BASE = 10000.0
SCONV_DIM = 512
POOL_DIM = 512
N_POOL_GROUPS = 4
POOL_GROUP = POOL_DIM // N_POOL_GROUPS
POOL_WINDOWS = (2, 4, 8, 16)
FFT_DIM = 512
N_FFT_GROUPS = 4
FFT_GROUP = FFT_DIM // N_FFT_GROUPS
D_FF = 2816
Q_DIM = N_Q_HEADS * HEAD_DIM
KV_DIM = N_KV_HEADS * HEAD_DIM
KV_TILES = 2 * KV_DIM * 2

LANES = 128
MASK_VALUE = -1e30
LOG2E = 1.4426950408889634
ATTN_QBLOCKS = 16
VMEM_LIMIT = 56 * 1024 * 1024
ROW_TILE = 512
IN_TILE = 1024
HALO = 16
FFN_CHUNK = 256
POST_PIECES = 8
MOD_ROWS = 16
N_GAINS = 4

assert HALO >= max(POOL_WINDOWS) and all(w & (w - 1) == 0 for w in POOL_WINDOWS)


def _params(*sem):
    return pltpu.CompilerParams(dimension_semantics=sem, vmem_limit_bytes=VMEM_LIMIT)


def _rms(x, g):
    ms = jnp.mean(x * x, axis=-1, keepdims=True)
    return x * lax.rsqrt(ms + EPS) * g


def _dot(a, b):
    return jnp.dot(a, b, preferred_element_type=F32)


class _Stream(NamedTuple):
    x: jax.Array
    seg: int
    ctx_row: int | None

    def tile(self):
        return min(IN_TILE, self.seg)

    def mod_row(self, tm):
        if self.ctx_row is not None:
            row = self.ctx_row
            return lambda tile: row
        per_seg = self.seg // tm
        return lambda tile: tile // per_seg


class _Tables(NamedTuple):
    mods: jax.Array
    gains: jax.Array


def _layer_spec(a, layer):
    nd = a.ndim
    return pl.BlockSpec((1,) + a.shape[1:], lambda *_: (layer,) + (0,) * (nd - 1), pipeline_mode=pl.Buffered(1))


def _const_spec(a):
    return pl.BlockSpec(a.shape, lambda *_: (0,) * a.ndim, pipeline_mode=pl.Buffered(1))


def _gain_spec(layer, j):
    return pl.BlockSpec((1, 1, D_MODEL), lambda *_: (layer * N_GAINS + j, 0, 0))


def _mod_spec(layer, chunk, row, tile=lambda i: i):
    return pl.BlockSpec((1, 1, D_MODEL), lambda i: (layer * MOD_ROWS + row(tile(i)), 0, chunk))


def _halo_specs(tm, rows, width, tile=lambda i: i):
    hb = tm // HALO
    return [
        pl.BlockSpec((HALO, width), lambda i: (jnp.maximum(tile(i) * hb - 1, 0), 0)),
        pl.BlockSpec((tm, width), lambda i: (tile(i), 0)),
        pl.BlockSpec((HALO, width), lambda i: (jnp.minimum((tile(i) + 1) * hb, rows // HALO - 1), 0)),
    ]


def _ext(prev_ref, main_ref, next_ref):
    return jnp.concatenate([prev_ref[...], main_ref[...], next_ref[...]], axis=0)


def _tile_pos(tm, seg):
    return (pl.program_id(0) * tm + lax.broadcasted_iota(jnp.int32, (tm, 1), 0)) % seg


def _seg_ext(x_ext, seg, tile=None):
    tm = x_ext.shape[0] - 2 * HALO
    i = pl.program_id(0) if tile is None else tile
    top_ok = (i * tm) % seg != 0
    bot_ok = ((i + 1) * tm) % seg != 0
    top, bot = x_ext[:HALO], x_ext[HALO + tm:]
    return jnp.concatenate([jnp.where(top_ok, top, jnp.zeros_like(top)), x_ext[HALO:HALO + tm],
                            jnp.where(bot_ok, bot, jnp.zeros_like(bot))], axis=0)


def _roll_rows(x_ext, off):
    rows = x_ext.shape[0]
    return pltpu.roll(x_ext, (-off) % rows, 0)


def _conv3(x_ext, seg, w, tile=None):
    tm = x_ext.shape[0] - 2 * HALO
    main = slice(HALO, HALO + tm)
    xm = _seg_ext(x_ext, seg, tile)
    prev, nxt = _roll_rows(xm, -1)[main], _roll_rows(xm, 1)[main]
    if seg < tm:
        pos = lax.broadcasted_iota(jnp.int32, (tm, 1), 0) % seg
        prev = jnp.where(pos == 0, 0.0, prev)
        nxt = jnp.where(pos == seg - 1, 0.0, nxt)
    return prev * w[0:1] + x_ext[main] * w[1:2] + nxt * w[2:3]


def _zero_anchor(v):
    fold = sum(v[r:r + 8] for r in range(0, v.shape[0], 8))
    fold = sum(fold[:, l:l + LANES] for l in range(0, fold.shape[1], LANES))
    return jnp.minimum(jnp.abs(fold[0:1]), 0.0)


def _modulated(x, gn_ref, sc_ref, sh_ref):
    return (_rms(x, gn_ref[0]) * (1.0 + sc_ref[0]) + sh_ref[0]).astype(BF16)


def _stream_specs(stream, tabs, layer):
    tm = stream.tile()
    rows, d = stream.x.shape
    row = stream.mod_row(tm)
    specs = _halo_specs(tm, rows, d) + [_gain_spec(layer, 0), _mod_spec(layer, 1, row), _mod_spec(layer, 0, row)]
    return specs, [stream.x] * 3 + [tabs.gains, tabs.mods, tabs.mods]


def _cast_rider(params, n_steps):
    in_specs, out_specs, out_shape = [], [], []
    for a, layer in params:
        r, c = a.shape[1:]
        chunk = r // n_steps
        assert chunk * n_steps == r and chunk % 16 == 0
        in_specs.append(pl.BlockSpec((1, chunk, c), lambda i, layer=layer: (layer, i, 0)))
        out_specs.append(pl.BlockSpec((1, chunk, c), lambda i: (0, i, 0)))
        out_shape.append(jax.ShapeDtypeStruct((1, r, c), BF16))
    return in_specs, out_specs, out_shape


def _run_casts(in_refs, out_refs):
    for src, dst in zip(in_refs, out_refs):
        dst[...] = src[...].astype(BF16)


def _cast_once(w_ref, w_scr):
    @pl.when(pl.program_id(0) == 0)
    def _():
        w_scr[...] = w_ref[0].astype(BF16)


def _mod_kernel(c_ref, w_ref, b_ref, o_ref):
    c = c_ref[...]
    s = (c / (1.0 + jnp.exp(-c))).astype(BF16)
    o_ref[0] = _dot(s, w_ref[0].astype(BF16)) + b_ref[0]


def _modulation(cc, mod_w, mod_b):
    depth, d, n = mod_w.shape
    tn = 2048
    return pl.pallas_call(
        _mod_kernel,
        grid=(depth, n // tn),
        in_specs=[
            pl.BlockSpec((MOD_ROWS, d), lambda i, j: (0, 0)),
            pl.BlockSpec((1, d, tn), lambda i, j: (i, 0, j)),
            pl.BlockSpec((1, 1, tn), lambda i, j: (i, 0, j)),
        ],
        out_specs=pl.BlockSpec((1, MOD_ROWS, tn), lambda i, j: (i, 0, j)),
        out_shape=jax.ShapeDtypeStruct((depth, MOD_ROWS, n), F32),
        compiler_params=_params("arbitrary", "arbitrary"),
        name="modulation",
    )(cc, mod_w, mod_b.reshape(depth, 1, n))


def _rope128(x, cos, sin_a, sin_b):
    return x * cos + pltpu.roll(x, LANES - 16, 1) * sin_a + pltpu.roll(x, 16, 1) * sin_b


def _write_kv(k, v, k_ref, v_ref):
    lo = lax.broadcasted_iota(jnp.int32, (1, LANES), 1) < HEAD_DIM
    vr = pltpu.roll(v, HEAD_DIM, 1)
    zero = jnp.zeros_like(v)
    v_ref[...] = jnp.concatenate(
        [jnp.where(lo, v, zero), jnp.where(lo, zero, vr), jnp.where(lo, vr, zero), jnp.where(lo, zero, v)],
        axis=1).astype(BF16)
    kt = k.T
    top = lax.broadcasted_iota(jnp.int32, (LANES, 1), 0) < HEAD_DIM
    ktr = pltpu.roll(kt, HEAD_DIM, 0)
    zero = jnp.zeros_like(kt)
    k_ref[...] = jnp.concatenate(
        [jnp.where(top, kt, zero), jnp.where(top, zero, ktr), jnp.where(top, ktr, zero), jnp.where(top, zero, kt)],
        axis=0).astype(BF16)


def _in_even_kernel(xp_ref, x_ref, xn_ref, gn_ref, sc_ref, sh_ref, w_ref, cw_ref, *rest, rope, seg, n_cast):
    if rope:
        cos_ref, sa_ref, sb_ref = rest[:3]
        rest = rest[3:]
    cast_in, rest = rest[:n_cast], rest[n_cast:]
    q_ref, k_ref, v_ref, s_ref = rest[:4]
    _run_casts(cast_in, rest[4:4 + n_cast])
    w_scr = rest[-1]
    _cast_once(w_ref, w_scr)
    tm = x_ref.shape[0]
    hh = _modulated(_ext(xp_ref, x_ref, xn_ref), gn_ref, sc_ref, sh_ref)
    h = hh[HALO:HALO + tm]
    o = Q_DIM + 2 * KV_DIM
    cz = (_dot(hh, w_scr[:, o + SCONV_DIM:o + 2 * SCONV_DIM])
          * _dot(hh, w_scr[:, o + 2 * SCONV_DIM:o + 3 * SCONV_DIM]))
    q = _dot(h, w_scr[:, 0:Q_DIM])
    kv = _dot(h, w_scr[:, Q_DIM:Q_DIM + 2 * KV_DIM])
    k, v = kv[:, 0:KV_DIM], kv[:, KV_DIM:2 * KV_DIM]
    b = _dot(h, w_scr[:, o:o + SCONV_DIM])
    s_ref[...] = (b * _conv3(cz, seg, cw_ref[0])).astype(BF16)
    if rope:
        cos, sa, sb = cos_ref[...], sa_ref[...], sb_ref[...]
        k = _rope128(k, cos, sa, sb)
        q = jnp.concatenate(
            [_rope128(q[:, LANES * j:LANES * (j + 1)], cos, sa, sb) for j in range(Q_DIM // LANES)], axis=1)
    q_ref[...] = (q * (HEAD_DIM ** -0.5 * LOG2E)).astype(BF16)
    _write_kv(k, v, k_ref, v_ref)


def _in_even(stream, tabs, layer, e, w_in, conv_w, rope_tables, casts=()):
    rows = stream.x.shape[0]
    tm = stream.tile()
    per_seg = stream.seg // tm
    rope = rope_tables is not None
    row = lambda i: (i, 0)
    in_specs, args = _stream_specs(stream, tabs, layer)
    in_specs += [_layer_spec(w_in, e), _layer_spec(conv_w, e)]
    args += [w_in, conv_w]
    if rope:
        in_specs += [pl.BlockSpec((tm, LANES), lambda i: (i % per_seg, 0))] * 3
        args += list(rope_tables)
    cast_in, cast_out, cast_shape = _cast_rider(casts, rows // tm)
    return pl.pallas_call(
        functools.partial(_in_even_kernel, rope=rope, seg=stream.seg, n_cast=len(casts)),
        grid=(rows // tm,),
        in_specs=in_specs + cast_in,
        out_specs=[pl.BlockSpec((tm, Q_DIM), row), pl.BlockSpec((KV_TILES, tm), lambda i: (0, i)),
                   pl.BlockSpec((tm, KV_TILES), row), pl.BlockSpec((tm, SCONV_DIM), row)] + cast_out,
        out_shape=[jax.ShapeDtypeStruct((rows, Q_DIM), BF16), jax.ShapeDtypeStruct((KV_TILES, rows), BF16),
                   jax.ShapeDtypeStruct((rows, KV_TILES), BF16), jax.ShapeDtypeStruct((rows, SCONV_DIM), BF16)]
        + cast_shape,
        scratch_shapes=[pltpu.VMEM(w_in.shape[1:], BF16)],
        compiler_params=_params("arbitrary"),
        name="in_even",
    )(*args, *[a for a, _ in casts])


def _in_kv_kernel(x_ref, gn_ref, sc_ref, sh_ref, w_ref, k_ref, v_ref):
    h = _modulated(x_ref[...], gn_ref, sc_ref, sh_ref)
    kv = _dot(h, w_ref[0].astype(BF16))
    _write_kv(kv[:, 0:KV_DIM], kv[:, KV_DIM:2 * KV_DIM], k_ref, v_ref)


def _in_kv(stream, tabs, layer, e, w_in):
    rows, d = stream.x.shape
    tm = stream.tile()
    row = lambda i: (i, 0)
    mrow = stream.mod_row(tm)
    return pl.pallas_call(
        _in_kv_kernel,
        grid=(rows // tm,),
        in_specs=[pl.BlockSpec((tm, d), row), _gain_spec(layer, 0), _mod_spec(layer, 1, mrow),
                  _mod_spec(layer, 0, mrow),
                  pl.BlockSpec((1, d, 2 * KV_DIM), lambda i: (e, 0, Q_DIM // (2 * KV_DIM)))],
        out_specs=[pl.BlockSpec((KV_TILES, tm), lambda i: (0, i)), pl.BlockSpec((tm, KV_TILES), row)],
        out_shape=[jax.ShapeDtypeStruct((KV_TILES, rows), BF16), jax.ShapeDtypeStruct((rows, KV_TILES), BF16)],
        compiler_params=_params("arbitrary"),
        name="in_kv",
    )(stream.x, tabs.gains, tabs.mods, tabs.mods, w_in)


def _in_odd_kernel(xp_ref, x_ref, xn_ref, gn_ref, sc_ref, sh_ref, w_ref, ccs_ref, wg_ref, ps_ref,
                   p_ref, ab_ref, w_scr, *, seg):
    _cast_once(w_ref, w_scr)
    tm = x_ref.shape[0]
    hh = _modulated(_ext(xp_ref, x_ref, xn_ref), gn_ref, sc_ref, sh_ref)
    h = hh[HALO:HALO + tm]
    up = _seg_ext(_dot(hh, w_scr[:, 0:POOL_DIM]), seg)
    uf = _dot(h, w_scr[:, POOL_DIM:POOL_DIM + FFT_DIM]).astype(BF16)
    ccs = ccs_ref[...]
    both = [_dot(uf[:, g * FFT_GROUP:(g + 1) * FFT_GROUP], ccs) for g in range(N_FFT_GROUPS)]
    ab_ref[...] = jnp.concatenate([r[:, 0:FFT_GROUP] for r in both] + [r[:, FFT_GROUP:] for r in both],
                                  axis=1).astype(BF16)
    pos = _tile_pos(tm, seg)
    pooled = []
    for gi, w in enumerate(POOL_WINDOWS):
        ug = up[:, gi * POOL_GROUP:(gi + 1) * POOL_GROUP]
        back, fwd = w // 2, w - w // 2
        run, span = ug, 1
        while span < w:
            run = run + _roll_rows(run, span)
            span *= 2
        total = _roll_rows(run, -back)[HALO:HALO + tm]
        count = (jnp.minimum(pos + fwd, seg) - jnp.maximum(pos - back, 0)).astype(F32)
        pooled.append((total / count - ug[HALO:HALO + tm]).astype(BF16))
    outs = [_dot(jnp.concatenate(pooled[2 * j:2 * j + 2], axis=1), wg_ref[0, j]) for j in range(N_POOL_GROUPS // 2)]
    p_ref[...] = (jnp.concatenate(outs, axis=1) * ps_ref[0]).astype(BF16)


def _in_odd(stream, tabs, layer, o, w_in, ccs, wg, ps):
    rows = stream.x.shape[0]
    tm = stream.tile()
    row = lambda i: (i, 0)
    in_specs, args = _stream_specs(stream, tabs, layer)
    return pl.pallas_call(
        functools.partial(_in_odd_kernel, seg=stream.seg),
        grid=(rows // tm,),
        in_specs=in_specs + [_layer_spec(w_in, o), _const_spec(ccs), _layer_spec(wg, o), _layer_spec(ps, o)],
        out_specs=[pl.BlockSpec((tm, POOL_DIM), row), pl.BlockSpec((tm, 2 * FFT_DIM), row)],
        out_shape=[jax.ShapeDtypeStruct((rows, POOL_DIM), BF16), jax.ShapeDtypeStruct((rows, 2 * FFT_DIM), BF16)],
        scratch_shapes=[pltpu.VMEM(w_in.shape[1:], BF16)],
        compiler_params=_params("arbitrary"),
        name="in_odd",
    )(*args, w_in, ccs, wg, ps)


def _attn_kernel(sink_ref, q_ref, *rest, window, qb, nblocks, e):
    if window:
        ktp, ktc, ktn, vp, vc, vn, kxt, vx, o_ref = rest
    else:
        kxt, vx, o_ref = rest
    n = pl.program_id(1)
    rows = 2 * BLOCK
    first_pair = lax.broadcasted_iota(jnp.int32, (rows, 1), 0) < BLOCK
    lo = lax.broadcasted_iota(jnp.int32, (1, LANES), 1) < HEAD_DIM
    if window:
        qi = lax.broadcasted_iota(jnp.int32, (rows, BLOCK), 0) % BLOCK
        j = lax.broadcasted_iota(jnp.int32, (rows, BLOCK), 1)
        tri_prev = jnp.where(j >= qi, 0.0, MASK_VALUE)
        tri_next = jnp.where(j <= qi, 0.0, MASK_VALUE)
        bias = []
        for t in range(qb):
            blk = n * qb + t
            bias.append((tri_prev + jnp.where(blk >= 1, 0.0, MASK_VALUE),
                         tri_next + jnp.where(blk <= nblocks - 2, 0.0, MASK_VALUE)))

    def lane_halves(h):
        c0 = 2 * LANES * h
        return slice(c0, c0 + LANES), slice(c0 + LANES, c0 + 2 * LANES)

    def keys_values(h, t):
        kparts, vparts = [], []
        for sl in lane_halves(h):
            if window:
                kall = jnp.concatenate([ktp[sl, :], ktc[sl, :], ktn[sl, :]], axis=1)
                vall = jnp.concatenate([vp[0, :, sl], vc[0, :, sl], vn[0, :, sl]], axis=0)
                kparts.append(kall[:, t * BLOCK:(t + 3) * BLOCK])
                vparts.append(vall[t * BLOCK:(t + 3) * BLOCK])
            kparts.append(kxt[sl, :])
            vparts.append(vx[0, :, sl])
        return jnp.concatenate(kparts, axis=1), jnp.concatenate(vparts, axis=0)

    def logits(h, t):
        rs = slice(t * BLOCK, (t + 1) * BLOCK)
        lo_sl, hi_sl = lane_halves(h)
        qp = jnp.concatenate([q_ref[0, rs, lo_sl], q_ref[0, rs, hi_sl]], axis=0)
        return _dot(qp, keys_values(h, t)[0])

    def finish(h, t, s):
        vst = keys_values(h, t)[1]
        nkeys = vst.shape[0] // 2
        probs, inv = [], []
        for half in range(2):
            tiles = [s[:, half * nkeys + c * LANES:half * nkeys + (c + 1) * LANES] for c in range(nkeys // LANES)]
            if window:
                tiles[0] = tiles[0] + bias[t][0]
                tiles[2] = tiles[2] + bias[t][1]
            head = 4 * h + half
            sink_col = jnp.where(first_pair, sink_ref[e, head], sink_ref[e, head + 2]) * LOG2E
            m = jnp.maximum(jnp.max(functools.reduce(jnp.maximum, tiles), axis=-1, keepdims=True), sink_col)
            es = [jnp.exp2(tl - m) for tl in tiles]
            den = jnp.sum(functools.reduce(jnp.add, es), axis=-1, keepdims=True) + jnp.exp2(sink_col - m)
            probs += [e_.astype(BF16) for e_ in es]
            inv.append(1.0 / den)
        o = _dot(jnp.concatenate(probs, axis=1), vst) * jnp.where(lo, inv[0], inv[1])
        rs = slice(t * BLOCK, (t + 1) * BLOCK)
        lo_sl, hi_sl = lane_halves(h)
        o_ref[0, rs, lo_sl] = o[:BLOCK].astype(o_ref.dtype)
        o_ref[0, rs, hi_sl] = o[BLOCK:].astype(o_ref.dtype)

    units = [(h, t) for h in range(N_KV_HEADS) for t in range(qb)]
    s_next = logits(*units[0])
    for u, unit in enumerate(units):
        s_cur = s_next
        if u + 1 < len(units):
            s_next = logits(*units[u + 1])
        finish(*unit, s_cur)


def _attention(sinks, e, q, kt, v, kxt, vx, window):
    b, t, _ = q.shape
    l = vx.shape[1]
    nb = t // BLOCK
    qb = ATTN_QBLOCKS if window else nb
    w = vx.shape[2]
    steps = nb // qb
    cur = lambda i, n: (i, n, 0)
    in_specs = [pl.BlockSpec(memory_space=pltpu.SMEM), pl.BlockSpec((1, qb * BLOCK, Q_DIM), cur)]
    args = [sinks, q]
    if window:
        pb = lambda n: jnp.maximum(n * qb - 1, 0)
        nx = lambda n: jnp.minimum(n * qb + qb, nb - 1)
        in_specs += [
            pl.BlockSpec((w, BLOCK), lambda i, n: (0, i * nb + pb(n))),
            pl.BlockSpec((w, qb * BLOCK), lambda i, n: (0, i * steps + n)),
            pl.BlockSpec((w, BLOCK), lambda i, n: (0, i * nb + nx(n))),
            pl.BlockSpec((1, BLOCK, w), lambda i, n: (i, pb(n), 0)),
            pl.BlockSpec((1, qb * BLOCK, w), cur),
            pl.BlockSpec((1, BLOCK, w), lambda i, n: (i, nx(n), 0)),
        ]
        args += [kt, kt, kt, v, v, v]
    in_specs += [pl.BlockSpec((w, l), lambda i, n: (0, i)), pl.BlockSpec((1, l, w), lambda i, n: (i, 0, 0))]
    args += [kxt, vx]
    return pl.pallas_call(
        functools.partial(_attn_kernel, window=window, qb=qb, nblocks=nb, e=e),
        grid=(b, steps),
        in_specs=in_specs,
        out_specs=pl.BlockSpec((1, qb * BLOCK, Q_DIM), cur),
        out_shape=jax.ShapeDtypeStruct((b, t, Q_DIM), BF16),
        compiler_params=_params("arbitrary", "arbitrary"),
        name="attention",
    )(*args)


def _seq_dft_kernel(ct_ref, st_ref, flip_ref, ab_ref, *rest, norm, n_cast):
    o_ref = rest[n_cast]
    _run_casts(rest[:n_cast], rest[n_cast + 1:])
    t = ab_ref.shape[1]
    half = t // 2
    a = ab_ref[0, :, 0:FFT_DIM]
    b = ab_ref[0, :, FFT_DIM:2 * FFT_DIM]
    p = _dot(ct_ref[...], a)
    q = _dot(st_ref[...], b)
    o_ref[0, 0:half] = ((p - q) * norm).astype(o_ref.dtype)
    sign = 1.0 - 2.0 * (lax.broadcasted_iota(jnp.int32, (t, 1), 0) % 2).astype(F32)
    nyquist = jnp.sum(a.astype(F32) * sign, axis=0, keepdims=True)
    first = lax.broadcasted_iota(jnp.int32, (half, 1), 0) == 0
    mirrored = (jnp.where(first, nyquist, p + q) * norm).astype(BF16)
    o_ref[0, half:t] = _dot(flip_ref[...], mirrored).astype(o_ref.dtype)


def _seq_dft(ab3, tables, casts=()):
    b, t, _ = ab3.shape
    cast_in, cast_out, cast_shape = _cast_rider(casts, b)
    return pl.pallas_call(
        functools.partial(_seq_dft_kernel, norm=float((t * FFT_GROUP) ** -0.5), n_cast=len(casts)),
        grid=(b,),
        in_specs=([_const_spec(a) for a in tables] + [pl.BlockSpec((1, t, 2 * FFT_DIM), lambda i: (i, 0, 0))]
                  + cast_in),
        out_specs=[pl.BlockSpec((1, t, FFT_DIM), lambda i: (i, 0, 0))] + cast_out,
        out_shape=[jax.ShapeDtypeStruct((b, t, FFT_DIM), BF16)] + cast_shape,
        compiler_params=_params("arbitrary"),
        name="seq_dft",
    )(*tables, ab3, *[a for a, _ in casts])


def _post_kernel(ap_ref, a_ref, an_ref, bp_ref, b_ref, bn_ref, xp_ref, x_ref, xn_ref,
                 wo_ref, gn1_ref, g1_ref, gn2_ref, sc_ref, sh_ref, g2_ref, gn3_ref,
                 wu_ref, cw_ref, wd_ref, o_ref, y_scr, hh_scr, xm_scr, act_scr, *, seg):
    i = pl.program_id(0)
    tm = x_ref.shape[0]
    main = slice(HALO, HALO + tm)
    rd = i % 2
    wr = 1 - rd
    piece_rows = tm // POST_PIECES
    pieces = ([(0, xp_ref, 0, HALO)]
              + [(HALO + k * piece_rows, x_ref, k * piece_rows, piece_rows) for k in range(POST_PIECES)]
              + [(HALO + tm, xn_ref, 0, HALO)])
    ka = a_ref.shape[1]

    def prepare(piece):
        lo, ref, off, nrows = piece
        xm = ref[off:off + nrows] + g1_ref[0] * _rms(y_scr[lo:lo + nrows], gn1_ref[0])
        hn = _rms(xm, gn2_ref[0]) * (1.0 + sc_ref[0]) + sh_ref[0]
        hh_scr[wr, lo:lo + nrows] = hn.astype(BF16)
        if ref is x_ref:
            xm_scr[wr, off:off + nrows] = xm
        return jnp.concatenate([_zero_anchor(hn)] * (FFN_CHUNK // LANES), axis=1)

    def out_projection():
        y_scr[...] = (_dot(_ext(ap_ref, a_ref, an_ref), wo_ref[0, 0:ka, :])
                      + _dot(_ext(bp_ref, b_ref, bn_ref), wo_ref[0, ka:, :]))

    @pl.when(i == 0)
    def _():
        y_scr[...] = jnp.zeros_like(y_scr)

    @pl.when(i < 2)
    def _():
        for piece in pieces:
            prepare(piece)
        out_projection()

    @pl.when(i >= 2)
    def _():
        hh = hh_scr[rd]
        h = hh[main]
        cw = cw_ref[0]
        anchor = None
        for c in range(D_FF // FFN_CHUNK):
            cs = slice(c * FFN_CHUNK, (c + 1) * FFN_CHUNK)
            cwc = cw[:, cs] if anchor is None else cw[:, cs] + anchor
            conv = _conv3(_dot(hh, wu_ref[0, :, cs]), seg, cwc, tile=i - 2)
            val = _dot(h, wu_ref[0, :, D_FF + c * FFN_CHUNK:D_FF + (c + 1) * FFN_CHUNK])
            act_scr[:, cs] = (conv / (1.0 + jnp.exp(-conv)) * val).astype(BF16)
            anchor = prepare(pieces[c]) if c < len(pieces) else None
        y2 = _dot(act_scr[...], wd_ref[0])
        out_projection()
        o_ref[...] = xm_scr[rd] + g2_ref[0] * _rms(y2, gn3_ref[0])


def _post(a2, b2, stream, tabs, layer, w_out, w_up, conv_w, w_down):
    x2 = stream.x
    rows, d = x2.shape
    tm = ROW_TILE
    seg = stream.seg
    assert seg % tm == 0 or tm % seg == 0
    n = rows // tm
    stage_a = lambda i: jnp.minimum(i, n - 1)
    stage_b = lambda i: jnp.clip(i - 1, 0, n - 1)
    stage_c = lambda i: jnp.maximum(i - 2, 0)
    row = stream.mod_row(min(tm, seg))
    mod = lambda chunk, stage: _mod_spec(layer, chunk, row, stage)
    return pl.pallas_call(
        functools.partial(_post_kernel, seg=seg),
        grid=(n + 2,),
        in_specs=(_halo_specs(tm, rows, a2.shape[1], stage_a) + _halo_specs(tm, rows, b2.shape[1], stage_a)
                  + _halo_specs(tm, rows, d, stage_b)
                  + [_layer_spec(w_out, 0), _gain_spec(layer, 1), mod(2, stage_b), _gain_spec(layer, 2),
                     mod(4, stage_b), mod(3, stage_b), mod(5, stage_c), _gain_spec(layer, 3),
                     _layer_spec(w_up, 0), _layer_spec(conv_w, layer), _layer_spec(w_down, 0)]),
        out_specs=pl.BlockSpec((tm, d), lambda i: (stage_c(i), 0)),
        out_shape=jax.ShapeDtypeStruct((rows, d), F32),
        scratch_shapes=[pltpu.VMEM((tm + 2 * HALO, d), F32), pltpu.VMEM((2, tm + 2 * HALO, d), BF16),
                        pltpu.VMEM((2, tm, d), F32), pltpu.VMEM((tm, D_FF), BF16)],
        compiler_params=_params("arbitrary"),
        name="post",
    )(a2, a2, a2, b2, b2, b2, x2, x2, x2, w_out, tabs.gains, tabs.mods, tabs.gains, tabs.mods, tabs.mods,
      tabs.mods, tabs.gains, w_up, conv_w, w_down)


def _rope_tables(seq):
    half = HEAD_DIM // 2
    nf = half // 2
    inv = ROPE_BASE ** (-np.arange(nf, dtype=np.float64) / nf)
    t = np.arange(seq)
    lane = np.arange(LANES)
    in_head = lane % HEAD_DIM
    pos = np.where((in_head // half)[None, :] == 0, (t // GRID_W)[:, None], (t % GRID_W)[:, None])
    ang = pos * inv[lane % nf][None, :]
    first = ((lane % half) < nf)[None, :]
    cos = np.cos(ang)
    sin_a = np.where(first, -np.sin(ang), 0.0)
    sin_b = np.where(first, 0.0, np.sin(ang))
    return tuple(jnp.asarray(a, F32) for a in (cos, sin_a, sin_b))


def _dft_tables(n, rows=None):
    k = np.arange(n if rows is None else rows)
    ang = 2.0 * np.pi * ((k[:, None] * np.arange(n)[None, :]) % n) / n
    return jnp.asarray(np.cos(ang), F32).astype(BF16), jnp.asarray(np.sin(ang), F32).astype(BF16)


def _seq_dft_tables(n):
    half = n // 2
    flip = np.zeros((half, half), np.float32)
    flip[np.arange(half), (half - np.arange(half)) % half] = 1.0
    return _dft_tables(n, half) + (jnp.asarray(flip).astype(BF16),)


def kernel(x, c, ctx, c_ctx, mod_w, mod_b, norm_g, att_w_in, att_sink, sconv_w, att_w_out, mix_w_in,
           pool_w_grp, pool_scale, mix_w_out, ffn_w_up, ffn_conv, ffn_w_down):
    bsz, seq, d = x.shape
    ctx_len = ctx.shape[1]
    assert d == D_MODEL and seq % ROW_TILE == 0 and (bsz * ctx_len) % ROW_TILE == 0 and ROW_TILE % ctx_len == 0
    assert bsz < MOD_ROWS and norm_g.shape[1] == N_GAINS

    cc = jnp.concatenate([c, c_ctx[None, :], jnp.zeros((MOD_ROWS - bsz - 1, d), F32)], axis=0)
    mods = _modulation(cc, mod_w, mod_b)
    tabs = _Tables(mods.reshape(DEPTH * MOD_ROWS, 1, 6 * d), norm_g.reshape(DEPTH * N_GAINS, 1, d))

    rope = _rope_tables(seq)
    chan_cs = jnp.concatenate(_dft_tables(FFT_GROUP), axis=1)
    seq_tabs = {seq: _seq_dft_tables(seq), ctx_len: _seq_dft_tables(ctx_len)}

    w_att_in, w_mix_in = att_w_in, mix_w_in
    g4 = pool_w_grp.astype(BF16)
    zero = jnp.zeros_like(g4[:, 0])
    w_grp = jnp.stack([jnp.concatenate([jnp.concatenate([g4[:, 2 * j], zero], axis=2),
                                        jnp.concatenate([zero, g4[:, 2 * j + 1]], axis=2)], axis=1)
                       for j in range(N_POOL_GROUPS // 2)], axis=1)
    p_scale = pool_scale.reshape(-1, 1, POOL_DIM)

    xs = x.reshape(bsz * seq, d)
    hc = ctx.reshape(bsz * ctx_len, d)

    for i in range(DEPTH):
        even = i % 2 == 0
        ctx_live = any(j % 2 == 0 for j in range(i + 1, DEPTH))
        lat = _Stream(xs, seq, None)
        con = _Stream(hc, ctx_len, bsz)
        if even:
            e = i // 2
            casts = ((att_w_out, e), (ffn_w_up, i), (ffn_w_down, i))
            q, kt, v, sx, w_out, w_up, w_down = _in_even(lat, tabs, i, e, w_att_in, sconv_w, rope, casts)
            post = (w_out, w_up, ffn_conv, w_down)
            if ctx_live:
                qc, kct, vc, sxc = _in_even(con, tabs, i, e, w_att_in, sconv_w, None)
            else:
                kct, vc = _in_kv(con, tabs, i, e, w_att_in)
            vc3 = vc.reshape(bsz, ctx_len, -1)
            ax = _attention(att_sink, e, q.reshape(bsz, seq, -1), kt, v.reshape(bsz, seq, -1), kct, vc3, True)
            xs = _post(ax.reshape(bsz * seq, -1), sx, lat, tabs, i, *post)
            if ctx_live:
                ac = _attention(att_sink, e, qc.reshape(bsz, ctx_len, -1), None, None, kct, vc3, False)
                hc = _post(ac.reshape(bsz * ctx_len, -1), sxc, con, tabs, i, *post)
        else:
            o = i // 2
            casts = ((mix_w_out, o), (ffn_w_up, i), (ffn_w_down, i))
            yp, ab = _in_odd(lat, tabs, i, o, w_mix_in, chan_cs, w_grp, p_scale)
            yf, w_out, w_up, w_down = _seq_dft(ab.reshape(bsz, seq, -1), seq_tabs[seq], casts)
            post = (w_out, w_up, ffn_conv, w_down)
            if ctx_live:
                ypc, abc = _in_odd(con, tabs, i, o, w_mix_in, chan_cs, w_grp, p_scale)
                yfc, = _seq_dft(abc.reshape(bsz, ctx_len, -1), seq_tabs[ctx_len])
            xs = _post(yp, yf.reshape(bsz * seq, -1), lat, tabs, i, *post)
            if ctx_live:
                hc = _post(ypc, yfc.reshape(bsz * ctx_len, -1), con, tabs, i, *post)
    return xs.reshape(bsz, seq, d)
```

```python
import functools
from typing import NamedTuple

import numpy as np
import jax
import jax.numpy as jnp
from jax import lax
from jax.experimental import pallas as pl
from jax.experimental.pallas import tpu as pltpu

F32 = jnp.float32
BF16 = jnp.bfloat16

D_MODEL = 1024
DEPTH = 4
GRID_W = 64
EPS = 1e-6
N_Q_HEADS = 8
N_KV_HEADS = 2
HEAD_DIM = 64
WINDOW = 128
BLOCK = 128
ROPE_BASE = 10000.0
SCONV_DIM = 512
POOL_DIM = 512
N_POOL_GROUPS = 4
POOL_GROUP = POOL_DIM // N_POOL_GROUPS
POOL_WINDOWS = (2, 4, 8, 16)
FFT_DIM = 512
N_FFT_GROUPS = 4
FFT_GROUP = FFT_DIM // N_FFT_GROUPS
D_FF = 2816
Q_DIM = N_Q_HEADS * HEAD_DIM
KV_DIM = N_KV_HEADS * HEAD_DIM
KV_TILES = 2 * KV_DIM * 2

LANES = 128
MASK_VALUE = -1e30
LOG2E = 1.4426950408889634
ATTN_QBLOCKS = 16
VMEM_LIMIT = 56 * 1024 * 1024
ROW_TILE = 512
IN_TILE = 1024
HALO = 16
FFN_CHUNK = 256
POST_PIECES = 8
MOD_ROWS = 16
N_GAINS = 4

assert HALO >= max(POOL_WINDOWS) and all(w & (w - 1) == 0 for w in POOL_WINDOWS)


def _params(*sem):
    return pltpu.CompilerParams(dimension_semantics=sem, vmem_limit_bytes=VMEM_LIMIT)


def _rms(x, g):
    ms = jnp.mean(x * x, axis=-1, keepdims=True)
    return x * lax.rsqrt(ms + EPS) * g


def _dot(a, b):
    return jnp.dot(a, b, preferred_element_type=F32)


class _Stream(NamedTuple):
    x: jax.Array
    seg: int
    ctx_row: int | None

    def tile(self):
        return min(IN_TILE, self.seg)

    def mod_row(self, tm):
        if self.ctx_row is not None:
            row = self.ctx_row
            return lambda tile: row
        per_seg = self.seg // tm
        return lambda tile: tile // per_seg


class _Tables(NamedTuple):
    mods: jax.Array
    gains: jax.Array


def _layer_spec(a, layer):
    nd = a.ndim
    return pl.BlockSpec((1,) + a.shape[1:], lambda *_: (layer,) + (0,) * (nd - 1), pipeline_mode=pl.Buffered(1))


def _const_spec(a):
    return pl.BlockSpec(a.shape, lambda *_: (0,) * a.ndim, pipeline_mode=pl.Buffered(1))


def _gain_spec(layer, j):
    return pl.BlockSpec((1, 1, D_MODEL), lambda *_: (layer * N_GAINS + j, 0, 0))


def _mod_spec(layer, chunk, row, tile=lambda i: i):
    return pl.BlockSpec((1, 1, D_MODEL), lambda i: (layer * MOD_ROWS + row(tile(i)), 0, chunk))


def _halo_specs(tm, rows, width, tile=lambda i: i):
    hb = tm // HALO
    return [
        pl.BlockSpec((HALO, width), lambda i: (jnp.maximum(tile(i) * hb - 1, 0), 0)),
        pl.BlockSpec((tm, width), lambda i: (tile(i), 0)),
        pl.BlockSpec((HALO, width), lambda i: (jnp.minimum((tile(i) + 1) * hb, rows // HALO - 1), 0)),
    ]


def _ext(prev_ref, main_ref, next_ref):
    return jnp.concatenate([prev_ref[...], main_ref[...], next_ref[...]], axis=0)


def _tile_pos(tm, seg):
    return (pl.program_id(0) * tm + lax.broadcasted_iota(jnp.int32, (tm, 1), 0)) % seg


def _seg_ext(x_ext, seg, tile=None):
    tm = x_ext.shape[0] - 2 * HALO
    i = pl.program_id(0) if tile is None else tile
    top_ok = (i * tm) % seg != 0
    bot_ok = ((i + 1) * tm) % seg != 0
    top, bot = x_ext[:HALO], x_ext[HALO + tm:]
    return jnp.concatenate([jnp.where(top_ok, top, jnp.zeros_like(top)), x_ext[HALO:HALO + tm],
                            jnp.where(bot_ok, bot, jnp.zeros_like(bot))], axis=0)


def _roll_rows(x_ext, off):
    rows = x_ext.shape[0]
    return pltpu.roll(x_ext, (-off) % rows, 0)


def _conv3(x_ext, seg, w, tile=None):
    tm = x_ext.shape[0] - 2 * HALO
    main = slice(HALO, HALO + tm)
    xm = _seg_ext(x_ext, seg, tile)
    prev, nxt = _roll_rows(xm, -1)[main], _roll_rows(xm, 1)[main]
    if seg < tm:
        pos = lax.broadcasted_iota(jnp.int32, (tm, 1), 0) % seg
        prev = jnp.where(pos == 0, 0.0, prev)
        nxt = jnp.where(pos == seg - 1, 0.0, nxt)
    return prev * w[0:1] + x_ext[main] * w[1:2] + nxt * w[2:3]


def _zero_anchor(v):
    fold = sum(v[r:r + 8] for r in range(0, v.shape[0], 8))
    fold = sum(fold[:, l:l + LANES] for l in range(0, fold.shape[1], LANES))
    return jnp.minimum(jnp.abs(fold[0:1]), 0.0)


def _modulated(x, gn_ref, sc_ref, sh_ref):
    return (_rms(x, gn_ref[0]) * (1.0 + sc_ref[0]) + sh_ref[0]).astype(BF16)


def _stream_specs(stream, tabs, layer):
    tm = stream.tile()
    rows, d = stream.x.shape
    row = stream.mod_row(tm)
    specs = _halo_specs(tm, rows, d) + [_gain_spec(layer, 0), _mod_spec(layer, 1, row), _mod_spec(layer, 0, row)]
    return specs, [stream.x] * 3 + [tabs.gains, tabs.mods, tabs.mods]


def _cast_rider(params, n_steps):
    in_specs, out_specs, out_shape = [], [], []
    for a, layer in params:
        r, c = a.shape[1:]
        chunk = r // n_steps
        assert chunk * n_steps == r and chunk % 16 == 0
        in_specs.append(pl.BlockSpec((1, chunk, c), lambda i, layer=layer: (layer, i, 0)))
        out_specs.append(pl.BlockSpec((1, chunk, c), lambda i: (0, i, 0)))
        out_shape.append(jax.ShapeDtypeStruct((1, r, c), BF16))
    return in_specs, out_specs, out_shape


def _run_casts(in_refs, out_refs):
    for src, dst in zip(in_refs, out_refs):
        dst[...] = src[...].astype(BF16)


def _cast_once(w_ref, w_scr):
    @pl.when(pl.program_id(0) == 0)
    def _():
        w_scr[...] = w_ref[0].astype(BF16)


def _mod_kernel(c_ref, w_ref, b_ref, o_ref):
    c = c_ref[...]
    s = (c / (1.0 + jnp.exp(-c))).astype(BF16)
    o_ref[0] = _dot(s, w_ref[0].astype(BF16)) + b_ref[0]


def _modulation(cc, mod_w, mod_b):
    depth, d, n = mod_w.shape
    tn = 2048
    return pl.pallas_call(
        _mod_kernel,
        grid=(depth, n // tn),
        in_specs=[
            pl.BlockSpec((MOD_ROWS, d), lambda i, j: (0, 0)),
            pl.BlockSpec((1, d, tn), lambda i, j: (i, 0, j)),
            pl.BlockSpec((1, 1, tn), lambda i, j: (i, 0, j)),
        ],
        out_specs=pl.BlockSpec((1, MOD_ROWS, tn), lambda i, j: (i, 0, j)),
        out_shape=jax.ShapeDtypeStruct((depth, MOD_ROWS, n), F32),
        compiler_params=_params("arbitrary", "arbitrary"),
        name="modulation",
    )(cc, mod_w, mod_b.reshape(depth, 1, n))


def _rope128(x, cos, sin_a, sin_b):
    return x * cos + pltpu.roll(x, LANES - 16, 1) * sin_a + pltpu.roll(x, 16, 1) * sin_b


def _write_kv(k, v, k_ref, v_ref):
    lo = lax.broadcasted_iota(jnp.int32, (1, LANES), 1) < HEAD_DIM
    vr = pltpu.roll(v, HEAD_DIM, 1)
    zero = jnp.zeros_like(v)
    v_ref[...] = jnp.concatenate(
        [jnp.where(lo, v, zero), jnp.where(lo, zero, vr), jnp.where(lo, vr, zero), jnp.where(lo, zero, v)],
        axis=1).astype(BF16)
    kt = k.T
    top = lax.broadcasted_iota(jnp.int32, (LANES, 1), 0) < HEAD_DIM
    ktr = pltpu.roll(kt, HEAD_DIM, 0)
    zero = jnp.zeros_like(kt)
    k_ref[...] = jnp.concatenate(
        [jnp.where(top, kt, zero), jnp.where(top, zero, ktr), jnp.where(top, ktr, zero), jnp.where(top, zero, kt)],
        axis=0).astype(BF16)


def _in_even_kernel(xp_ref, x_ref, xn_ref, gn_ref, sc_ref, sh_ref, w_ref, cw_ref, *rest, rope, seg, n_cast):
    if rope:
        cos_ref, sa_ref, sb_ref = rest[:3]
        rest = rest[3:]
    cast_in, rest = rest[:n_cast], rest[n_cast:]
    q_ref, k_ref, v_ref, s_ref = rest[:4]
    _run_casts(cast_in, rest[4:4 + n_cast])
    w_scr = rest[-1]
    _cast_once(w_ref, w_scr)
    tm = x_ref.shape[0]
    hh = _modulated(_ext(xp_ref, x_ref, xn_ref), gn_ref, sc_ref, sh_ref)
    h = hh[HALO:HALO + tm]
    o = Q_DIM + 2 * KV_DIM
    cz = (_dot(hh, w_scr[:, o + SCONV_DIM:o + 2 * SCONV_DIM])
          * _dot(hh, w_scr[:, o + 2 * SCONV_DIM:o + 3 * SCONV_DIM]))
    q = _dot(h, w_scr[:, 0:Q_DIM])
    kv = _dot(h, w_scr[:, Q_DIM:Q_DIM + 2 * KV_DIM])
    k, v = kv[:, 0:KV_DIM], kv[:, KV_DIM:2 * KV_DIM]
    b = _dot(h, w_scr[:, o:o + SCONV_DIM])
    s_ref[...] = (b * _conv3(cz, seg, cw_ref[0])).astype(BF16)
    if rope:
        cos, sa, sb = cos_ref[...], sa_ref[...], sb_ref[...]
        k = _rope128(k, cos, sa, sb)
        q = jnp.concatenate(
            [_rope128(q[:, LANES * j:LANES * (j + 1)], cos, sa, sb) for j in range(Q_DIM // LANES)], axis=1)
    q_ref[...] = (q * (HEAD_DIM ** -0.5 * LOG2E)).astype(BF16)
    _write_kv(k, v, k_ref, v_ref)


def _in_even(stream, tabs, layer, e, w_in, conv_w, rope_tables, casts=()):
    rows = stream.x.shape[0]
    tm = stream.tile()
    per_seg = stream.seg // tm
    rope = rope_tables is not None
    row = lambda i: (i, 0)
    in_specs, args = _stream_specs(stream, tabs, layer)
    in_specs += [_layer_spec(w_in, e), _layer_spec(conv_w, e)]
    args += [w_in, conv_w]
    if rope:
        in_specs += [pl.BlockSpec((tm, LANES), lambda i: (i % per_seg, 0))] * 3
        args += list(rope_tables)
    cast_in, cast_out, cast_shape = _cast_rider(casts, rows // tm)
    return pl.pallas_call(
        functools.partial(_in_even_kernel, rope=rope, seg=stream.seg, n_cast=len(casts)),
        grid=(rows // tm,),
        in_specs=in_specs + cast_in,
        out_specs=[pl.BlockSpec((tm, Q_DIM), row), pl.BlockSpec((KV_TILES, tm), lambda i: (0, i)),
                   pl.BlockSpec((tm, KV_TILES), row), pl.BlockSpec((tm, SCONV_DIM), row)] + cast_out,
        out_shape=[jax.ShapeDtypeStruct((rows, Q_DIM), BF16), jax.ShapeDtypeStruct((KV_TILES, rows), BF16),
                   jax.ShapeDtypeStruct((rows, KV_TILES), BF16), jax.ShapeDtypeStruct((rows, SCONV_DIM), BF16)]
        + cast_shape,
        scratch_shapes=[pltpu.VMEM(w_in.shape[1:], BF16)],
        compiler_params=_params("arbitrary"),
        name="in_even",
    )(*args, *[a for a, _ in casts])


def _in_kv_kernel(x_ref, gn_ref, sc_ref, sh_ref, w_ref, k_ref, v_ref):
    h = _modulated(x_ref[...], gn_ref, sc_ref, sh_ref)
    kv = _dot(h, w_ref[0].astype(BF16))
    _write_kv(kv[:, 0:KV_DIM], kv[:, KV_DIM:2 * KV_DIM], k_ref, v_ref)


def _in_kv(stream, tabs, layer, e, w_in):
    rows, d = stream.x.shape
    tm = stream.tile()
    row = lambda i: (i, 0)
    mrow = stream.mod_row(tm)
    return pl.pallas_call(
        _in_kv_kernel,
        grid=(rows // tm,),
        in_specs=[pl.BlockSpec((tm, d), row), _gain_spec(layer, 0), _mod_spec(layer, 1, mrow),
                  _mod_spec(layer, 0, mrow),
                  pl.BlockSpec((1, d, 2 * KV_DIM), lambda i: (e, 0, Q_DIM // (2 * KV_DIM)))],
        out_specs=[pl.BlockSpec((KV_TILES, tm), lambda i: (0, i)), pl.BlockSpec((tm, KV_TILES), row)],
        out_shape=[jax.ShapeDtypeStruct((KV_TILES, rows), BF16), jax.ShapeDtypeStruct((rows, KV_TILES), BF16)],
        compiler_params=_params("arbitrary"),
        name="in_kv",
    )(stream.x, tabs.gains, tabs.mods, tabs.mods, w_in)


def _in_odd_kernel(xp_ref, x_ref, xn_ref, gn_ref, sc_ref, sh_ref, w_ref, ccs_ref, wg_ref, ps_ref,
                   p_ref, ab_ref, w_scr, *, seg):
    _cast_once(w_ref, w_scr)
    tm = x_ref.shape[0]
    hh = _modulated(_ext(xp_ref, x_ref, xn_ref), gn_ref, sc_ref, sh_ref)
    h = hh[HALO:HALO + tm]
    up = _seg_ext(_dot(hh, w_scr[:, 0:POOL_DIM]), seg)
    uf = _dot(h, w_scr[:, POOL_DIM:POOL_DIM + FFT_DIM]).astype(BF16)
    ccs = ccs_ref[...]
    both = [_dot(uf[:, g * FFT_GROUP:(g + 1) * FFT_GROUP], ccs) for g in range(N_FFT_GROUPS)]
    ab_ref[...] = jnp.concatenate([r[:, 0:FFT_GROUP] for r in both] + [r[:, FFT_GROUP:] for r in both],
                                  axis=1).astype(BF16)
    pos = _tile_pos(tm, seg)
    pooled = []
    for gi, w in enumerate(POOL_WINDOWS):
        ug = up[:, gi * POOL_GROUP:(gi + 1) * POOL_GROUP]
        back, fwd = w // 2, w - w // 2
        run, span = ug, 1
        while span < w:
            run = run + _roll_rows(run, span)
            span *= 2
        total = _roll_rows(run, -back)[HALO:HALO + tm]
        count = (jnp.minimum(pos + fwd, seg) - jnp.maximum(pos - back, 0)).astype(F32)
        pooled.append((total / count - ug[HALO:HALO + tm]).astype(BF16))
    outs = [_dot(jnp.concatenate(pooled[2 * j:2 * j + 2], axis=1), wg_ref[0, j]) for j in range(N_POOL_GROUPS // 2)]
    p_ref[...] = (jnp.concatenate(outs, axis=1) * ps_ref[0]).astype(BF16)


def _in_odd(stream, tabs, layer, o, w_in, ccs, wg, ps):
    rows = stream.x.shape[0]
    tm = stream.tile()
    row = lambda i: (i, 0)
    in_specs, args = _stream_specs(stream, tabs, layer)
    return pl.pallas_call(
        functools.partial(_in_odd_kernel, seg=stream.seg),
        grid=(rows // tm,),
        in_specs=in_specs + [_layer_spec(w_in, o), _const_spec(ccs), _layer_spec(wg, o), _layer_spec(ps, o)],
        out_specs=[pl.BlockSpec((tm, POOL_DIM), row), pl.BlockSpec((tm, 2 * FFT_DIM), row)],
        out_shape=[jax.ShapeDtypeStruct((rows, POOL_DIM), BF16), jax.ShapeDtypeStruct((rows, 2 * FFT_DIM), BF16)],
        scratch_shapes=[pltpu.VMEM(w_in.shape[1:], BF16)],
        compiler_params=_params("arbitrary"),
        name="in_odd",
    )(*args, w_in, ccs, wg, ps)


def _attn_kernel(sink_ref, q_ref, *rest, window, qb, nblocks, e):
    if window:
        ktp, ktc, ktn, vp, vc, vn, kxt, vx, o_ref = rest
    else:
        kxt, vx, o_ref = rest
    n = pl.program_id(1)
    rows = 2 * BLOCK
    first_pair = lax.broadcasted_iota(jnp.int32, (rows, 1), 0) < BLOCK
    lo = lax.broadcasted_iota(jnp.int32, (1, LANES), 1) < HEAD_DIM
    if window:
        qi = lax.broadcasted_iota(jnp.int32, (rows, BLOCK), 0) % BLOCK
        j = lax.broadcasted_iota(jnp.int32, (rows, BLOCK), 1)
        tri_prev = jnp.where(j >= qi, 0.0, MASK_VALUE)
        tri_next = jnp.where(j <= qi, 0.0, MASK_VALUE)
        bias = []
        for t in range(qb):
            blk = n * qb + t
            bias.append((tri_prev + jnp.where(blk >= 1, 0.0, MASK_VALUE),
                         tri_next + jnp.where(blk <= nblocks - 2, 0.0, MASK_VALUE)))

    def lane_halves(h):
        c0 = 2 * LANES * h
        return slice(c0, c0 + LANES), slice(c0 + LANES, c0 + 2 * LANES)

    def keys_values(h, t):
        kparts, vparts = [], []
        for sl in lane_halves(h):
            if window:
                kall = jnp.concatenate([ktp[sl, :], ktc[sl, :], ktn[sl, :]], axis=1)
                vall = jnp.concatenate([vp[0, :, sl], vc[0, :, sl], vn[0, :, sl]], axis=0)
                kparts.append(kall[:, t * BLOCK:(t + 3) * BLOCK])
                vparts.append(vall[t * BLOCK:(t + 3) * BLOCK])
            kparts.append(kxt[sl, :])
            vparts.append(vx[0, :, sl])
        return jnp.concatenate(kparts, axis=1), jnp.concatenate(vparts, axis=0)

    def logits(h, t):
        rs = slice(t * BLOCK, (t + 1) * BLOCK)
        lo_sl, hi_sl = lane_halves(h)
        qp = jnp.concatenate([q_ref[0, rs, lo_sl], q_ref[0, rs, hi_sl]], axis=0)
        return _dot(qp, keys_values(h, t)[0])

    def finish(h, t, s):
        vst = keys_values(h, t)[1]
        nkeys = vst.shape[0] // 2
        probs, inv = [], []
        for half in range(2):
            tiles = [s[:, half * nkeys + c * LANES:half * nkeys + (c + 1) * LANES] for c in range(nkeys // LANES)]
            if window:
                tiles[0] = tiles[0] + bias[t][0]
                tiles[2] = tiles[2] + bias[t][1]
            head = 4 * h + half
            sink_col = jnp.where(first_pair, sink_ref[e, head], sink_ref[e, head + 2]) * LOG2E
            m = jnp.maximum(jnp.max(functools.reduce(jnp.maximum, tiles), axis=-1, keepdims=True), sink_col)
            es = [jnp.exp2(tl - m) for tl in tiles]
            den = jnp.sum(functools.reduce(jnp.add, es), axis=-1, keepdims=True) + jnp.exp2(sink_col - m)
            probs += [e_.astype(BF16) for e_ in es]
            inv.append(1.0 / den)
        o = _dot(jnp.concatenate(probs, axis=1), vst) * jnp.where(lo, inv[0], inv[1])
        rs = slice(t * BLOCK, (t + 1) * BLOCK)
        lo_sl, hi_sl = lane_halves(h)
        o_ref[0, rs, lo_sl] = o[:BLOCK].astype(o_ref.dtype)
        o_ref[0, rs, hi_sl] = o[BLOCK:].astype(o_ref.dtype)

    units = [(h, t) for h in range(N_KV_HEADS) for t in range(qb)]
    s_next = logits(*units[0])
    for u, unit in enumerate(units):
        s_cur = s_next
        if u + 1 < len(units):
            s_next = logits(*units[u + 1])
        finish(*unit, s_cur)


def _attention(sinks, e, q, kt, v, kxt, vx, window):
    b, t, _ = q.shape
    l = vx.shape[1]
    nb = t // BLOCK
    qb = ATTN_QBLOCKS if window else nb
    w = vx.shape[2]
    steps = nb // qb
    cur = lambda i, n: (i, n, 0)
    in_specs = [pl.BlockSpec(memory_space=pltpu.SMEM), pl.BlockSpec((1, qb * BLOCK, Q_DIM), cur)]
    args = [sinks, q]
    if window:
        pb = lambda n: jnp.maximum(n * qb - 1, 0)
        nx = lambda n: jnp.minimum(n * qb + qb, nb - 1)
        in_specs += [
            pl.BlockSpec((w, BLOCK), lambda i, n: (0, i * nb + pb(n))),
            pl.BlockSpec((w, qb * BLOCK), lambda i, n: (0, i * steps + n)),
            pl.BlockSpec((w, BLOCK), lambda i, n: (0, i * nb + nx(n))),
            pl.BlockSpec((1, BLOCK, w), lambda i, n: (i, pb(n), 0)),
            pl.BlockSpec((1, qb * BLOCK, w), cur),
            pl.BlockSpec((1, BLOCK, w), lambda i, n: (i, nx(n), 0)),
        ]
        args += [kt, kt, kt, v, v, v]
    in_specs += [pl.BlockSpec((w, l), lambda i, n: (0, i)), pl.BlockSpec((1, l, w), lambda i, n: (i, 0, 0))]
    args += [kxt, vx]
    return pl.pallas_call(
        functools.partial(_attn_kernel, window=window, qb=qb, nblocks=nb, e=e),
        grid=(b, steps),
        in_specs=in_specs,
        out_specs=pl.BlockSpec((1, qb * BLOCK, Q_DIM), cur),
        out_shape=jax.ShapeDtypeStruct((b, t, Q_DIM), BF16),
        compiler_params=_params("arbitrary", "arbitrary"),
        name="attention",
    )(*args)


def _seq_dft_kernel(ct_ref, st_ref, flip_ref, ab_ref, *rest, norm, n_cast):
    o_ref = rest[n_cast]
    _run_casts(rest[:n_cast], rest[n_cast + 1:])
    t = ab_ref.shape[1]
    half = t // 2
    a = ab_ref[0, :, 0:FFT_DIM]
    b = ab_ref[0, :, FFT_DIM:2 * FFT_DIM]
    p = _dot(ct_ref[...], a)
    q = _dot(st_ref[...], b)
    o_ref[0, 0:half] = ((p - q) * norm).astype(o_ref.dtype)
    sign = 1.0 - 2.0 * (lax.broadcasted_iota(jnp.int32, (t, 1), 0) % 2).astype(F32)
    nyquist = jnp.sum(a.astype(F32) * sign, axis=0, keepdims=True)
    first = lax.broadcasted_iota(jnp.int32, (half, 1), 0) == 0
    mirrored = (jnp.where(first, nyquist, p + q) * norm).astype(BF16)
    o_ref[0, half:t] = _dot(flip_ref[...], mirrored).astype(o_ref.dtype)


def _seq_dft(ab3, tables, casts=()):
    b, t, _ = ab3.shape
    cast_in, cast_out, cast_shape = _cast_rider(casts, b)
    return pl.pallas_call(
        functools.partial(_seq_dft_kernel, norm=float((t * FFT_GROUP) ** -0.5), n_cast=len(casts)),
        grid=(b,),
        in_specs=([_const_spec(a) for a in tables] + [pl.BlockSpec((1, t, 2 * FFT_DIM), lambda i: (i, 0, 0))]
                  + cast_in),
        out_specs=[pl.BlockSpec((1, t, FFT_DIM), lambda i: (i, 0, 0))] + cast_out,
        out_shape=[jax.ShapeDtypeStruct((b, t, FFT_DIM), BF16)] + cast_shape,
        compiler_params=_params("arbitrary"),
        name="seq_dft",
    )(*tables, ab3, *[a for a, _ in casts])


def _post_kernel(ap_ref, a_ref, an_ref, bp_ref, b_ref, bn_ref, xp_ref, x_ref, xn_ref,
                 wo_ref, gn1_ref, g1_ref, gn2_ref, sc_ref, sh_ref, g2_ref, gn3_ref,
                 wu_hbm, cw_ref, wd_hbm, o_ref, y_scr, hh_scr, xm_scr, act_scr, wu_ref, wd_ref, w_sem, *, seg):
    i = pl.program_id(0)
    tm = x_ref.shape[0]
    main = slice(HALO, HALO + tm)
    rd = i % 2
    wr = 1 - rd
    piece_rows = tm // POST_PIECES
    pieces = ([(0, xp_ref, 0, HALO)]
              + [(HALO + k * piece_rows, x_ref, k * piece_rows, piece_rows) for k in range(POST_PIECES)]
              + [(HALO + tm, xn_ref, 0, HALO)])
    ka = a_ref.shape[1]

    def prepare(piece):
        lo, ref, off, nrows = piece
        xm = ref[off:off + nrows] + g1_ref[0] * _rms(y_scr[lo:lo + nrows], gn1_ref[0])
        hn = _rms(xm, gn2_ref[0]) * (1.0 + sc_ref[0]) + sh_ref[0]
        hh_scr[wr, lo:lo + nrows] = hn.astype(BF16)
        if ref is x_ref:
            xm_scr[wr, off:off + nrows] = xm
        return jnp.concatenate([_zero_anchor(hn)] * (FFN_CHUNK // LANES), axis=1)

    def out_projection():
        y_scr[...] = (_dot(_ext(ap_ref, a_ref, an_ref), wo_ref[0, 0:ka, :])
                      + _dot(_ext(bp_ref, b_ref, bn_ref), wo_ref[0, ka:, :]))

    weight_copies = (pltpu.make_async_copy(wu_hbm, wu_ref, w_sem.at[0]),
                     pltpu.make_async_copy(wd_hbm, wd_ref, w_sem.at[1]))

    @pl.when(i == 0)
    def _():
        for copy in weight_copies:
            copy.start()
        y_scr[...] = jnp.zeros_like(y_scr)

    @pl.when(i == 2)
    def _():
        for copy in weight_copies:
            copy.wait()

    @pl.when(i < 2)
    def _():
        for piece in pieces:
            prepare(piece)
        out_projection()

    @pl.when(i >= 2)
    def _():
        hh = hh_scr[rd]
        h = hh[main]
        cw = cw_ref[0]
        anchor = None
        for c in range(D_FF // FFN_CHUNK):
            cs = slice(c * FFN_CHUNK, (c + 1) * FFN_CHUNK)
            cwc = cw[:, cs] if anchor is None else cw[:, cs] + anchor
            conv = _conv3(_dot(hh, wu_ref[0, :, cs]), seg, cwc, tile=i - 2)
            val = _dot(h, wu_ref[0, :, D_FF + c * FFN_CHUNK:D_FF + (c + 1) * FFN_CHUNK])
            act_scr[:, cs] = (conv / (1.0 + jnp.exp(-conv)) * val).astype(BF16)
            anchor = prepare(pieces[c]) if c < len(pieces) else None
        y2 = _dot(act_scr[...], wd_ref[0])
        out_projection()
        o_ref[...] = xm_scr[rd] + g2_ref[0] * _rms(y2, gn3_ref[0])


def _post(a2, b2, stream, tabs, layer, w_out, w_up, conv_w, w_down):
    x2 = stream.x
    rows, d = x2.shape
    tm = ROW_TILE
    seg = stream.seg
    assert seg % tm == 0 or tm % seg == 0
    n = rows // tm
    stage_a = lambda i: jnp.minimum(i, n - 1)
    stage_b = lambda i: jnp.clip(i - 1, 0, n - 1)
    stage_c = lambda i: jnp.maximum(i - 2, 0)
    row = stream.mod_row(min(tm, seg))
    mod = lambda chunk, stage: _mod_spec(layer, chunk, row, stage)
    return pl.pallas_call(
        functools.partial(_post_kernel, seg=seg),
        grid=(n + 2,),
        in_specs=(_halo_specs(tm, rows, a2.shape[1], stage_a) + _halo_specs(tm, rows, b2.shape[1], stage_a)
                  + _halo_specs(tm, rows, d, stage_b)
                  + [_layer_spec(w_out, 0), _gain_spec(layer, 1), mod(2, stage_b), _gain_spec(layer, 2),
                     mod(4, stage_b), mod(3, stage_b), mod(5, stage_c), _gain_spec(layer, 3),
                     pl.BlockSpec(memory_space=pl.ANY), _layer_spec(conv_w, layer),
                     pl.BlockSpec(memory_space=pl.ANY)]),
        out_specs=pl.BlockSpec((tm, d), lambda i: (stage_c(i), 0)),
        out_shape=jax.ShapeDtypeStruct((rows, d), F32),
        scratch_shapes=[pltpu.VMEM((tm + 2 * HALO, d), F32), pltpu.VMEM((2, tm + 2 * HALO, d), BF16),
                        pltpu.VMEM((2, tm, d), F32), pltpu.VMEM((tm, D_FF), BF16),
                        pltpu.VMEM(w_up.shape, BF16), pltpu.VMEM(w_down.shape, BF16),
                        pltpu.SemaphoreType.DMA((2,))],
        compiler_params=_params("arbitrary"),
        name="post",
    )(a2, a2, a2, b2, b2, b2, x2, x2, x2, w_out, tabs.gains, tabs.mods, tabs.gains, tabs.mods, tabs.mods,
      tabs.mods, tabs.gains, w_up, conv_w, w_down)


def _rope_tables(seq):
    half = HEAD_DIM // 2
    nf = half // 2
    inv = ROPE_BASE ** (-np.arange(nf, dtype=np.float64) / nf)
    t = np.arange(seq)
    lane = np.arange(LANES)
    in_head = lane % HEAD_DIM
    pos = np.where((in_head // half)[None, :] == 0, (t // GRID_W)[:, None], (t % GRID_W)[:, None])
    ang = pos * inv[lane % nf][None, :]
    first = ((lane % half) < nf)[None, :]
    cos = np.cos(ang)
    sin_a = np.where(first, -np.sin(ang), 0.0)
    sin_b = np.where(first, 0.0, np.sin(ang))
    return tuple(jnp.asarray(a, F32) for a in (cos, sin_a, sin_b))


def _dft_tables(n, rows=None):
    k = np.arange(n if rows is None else rows)
    ang = 2.0 * np.pi * ((k[:, None] * np.arange(n)[None, :]) % n) / n
    return jnp.asarray(np.cos(ang), F32).astype(BF16), jnp.asarray(np.sin(ang), F32).astype(BF16)


def _seq_dft_tables(n):
    half = n // 2
    flip = np.zeros((half, half), np.float32)
    flip[np.arange(half), (half - np.arange(half)) % half] = 1.0
    return _dft_tables(n, half) + (jnp.asarray(flip).astype(BF16),)


def kernel(x, c, ctx, c_ctx, mod_w, mod_b, norm_g, att_w_in, att_sink, sconv_w, att_w_out, mix_w_in,
           pool_w_grp, pool_scale, mix_w_out, ffn_w_up, ffn_conv, ffn_w_down):
    bsz, seq, d = x.shape
    ctx_len = ctx.shape[1]
    assert d == D_MODEL and seq % ROW_TILE == 0 and (bsz * ctx_len) % ROW_TILE == 0 and ROW_TILE % ctx_len == 0
    assert bsz < MOD_ROWS and norm_g.shape[1] == N_GAINS

    cc = jnp.concatenate([c, c_ctx[None, :], jnp.zeros((MOD_ROWS - bsz - 1, d), F32)], axis=0)
    mods = _modulation(cc, mod_w, mod_b)
    tabs = _Tables(mods.reshape(DEPTH * MOD_ROWS, 1, 6 * d), norm_g.reshape(DEPTH * N_GAINS, 1, d))

    rope = _rope_tables(seq)
    chan_cs = jnp.concatenate(_dft_tables(FFT_GROUP), axis=1)
    seq_tabs = {seq: _seq_dft_tables(seq), ctx_len: _seq_dft_tables(ctx_len)}

    w_att_in, w_mix_in = att_w_in, mix_w_in
    g4 = pool_w_grp.astype(BF16)
    zero = jnp.zeros_like(g4[:, 0])
    w_grp = jnp.stack([jnp.concatenate([jnp.concatenate([g4[:, 2 * j], zero], axis=2),
                                        jnp.concatenate([zero, g4[:, 2 * j + 1]], axis=2)], axis=1)
                       for j in range(N_POOL_GROUPS // 2)], axis=1)
    p_scale = pool_scale.reshape(-1, 1, POOL_DIM)

    xs = x.reshape(bsz * seq, d)
    hc = ctx.reshape(bsz * ctx_len, d)

    for i in range(DEPTH):
        even = i % 2 == 0
        ctx_live = any(j % 2 == 0 for j in range(i + 1, DEPTH))
        lat = _Stream(xs, seq, None)
        con = _Stream(hc, ctx_len, bsz)
        if even:
            e = i // 2
            casts = ((att_w_out, e), (ffn_w_up, i), (ffn_w_down, i))
            q, kt, v, sx, w_out, w_up, w_down = _in_even(lat, tabs, i, e, w_att_in, sconv_w, rope, casts)
            post = (w_out, w_up, ffn_conv, w_down)
            if ctx_live:
                qc, kct, vc, sxc = _in_even(con, tabs, i, e, w_att_in, sconv_w, None)
            else:
                kct, vc = _in_kv(con, tabs, i, e, w_att_in)
            vc3 = vc.reshape(bsz, ctx_len, -1)
            ax = _attention(att_sink, e, q.reshape(bsz, seq, -1), kt, v.reshape(bsz, seq, -1), kct, vc3, True)
            xs = _post(ax.reshape(bsz * seq, -1), sx, lat, tabs, i, *post)
            if ctx_live:
                ac = _attention(att_sink, e, qc.reshape(bsz, ctx_len, -1), None, None, kct, vc3, False)
                hc = _post(ac.reshape(bsz * ctx_len, -1), sxc, con, tabs, i, *post)
        else:
            o = i // 2
            casts = ((mix_w_out, o), (ffn_w_up, i), (ffn_w_down, i))
            yp, ab = _in_odd(lat, tabs, i, o, w_mix_in, chan_cs, w_grp, p_scale)
            yf, w_out, w_up, w_down = _seq_dft(ab.reshape(bsz, seq, -1), seq_tabs[seq], casts)
            post = (w_out, w_up, ffn_conv, w_down)
            if ctx_live:
                ypc, abc = _in_odd(con, tabs, i, o, w_mix_in, chan_cs, w_grp, p_scale)
                yfc, = _seq_dft(abc.reshape(bsz, ctx_len, -1), seq_tabs[ctx_len])
            xs = _post(yp, yf.reshape(bsz * seq, -1), lat, tabs, i, *post)
            if ctx_live:
                hc = _post(ypc, yfc.reshape(bsz * ctx_len, -1), con, tabs, i, *post)
    return xs.reshape(bsz, seq, d)
```

```python
import functools
from typing import NamedTuple

import numpy as np
import jax
import jax.numpy as jnp
from jax import lax
from jax.experimental import pallas as pl
from jax.experimental.pallas import tpu as pltpu

F32 = jnp.float32
BF16 = jnp.bfloat16

D_MODEL = 1024
DEPTH = 4
GRID_W = 64
EPS = 1e-6
N_Q_HEADS = 8
N_KV_HEADS = 2
HEAD_DIM = 64
WINDOW = 128
BLOCK = 128
ROPE_BASE = 10000.0
SCONV_DIM = 512
POOL_DIM = 512
N_POOL_GROUPS = 4
POOL_GROUP = POOL_DIM // N_POOL_GROUPS
POOL_WINDOWS = (2, 4, 8, 16)
FFT_DIM = 512
N_FFT_GROUPS = 4
FFT_GROUP = FFT_DIM // N_FFT_GROUPS
D_FF = 2816
Q_DIM = N_Q_HEADS * HEAD_DIM
KV_DIM = N_KV_HEADS * HEAD_DIM
KV_TILES = 2 * KV_DIM * 2

LANES = 128
MASK_VALUE = -1e30
LOG2E = 1.4426950408889634
ATTN_QBLOCKS = 16
VMEM_LIMIT = 56 * 1024 * 1024
ROW_TILE = 512
IN_TILE = 1024
HALO = 16
FFN_CHUNK = 256
POST_PIECES = 8
MOD_ROWS = 16
N_GAINS = 4

assert HALO >= max(POOL_WINDOWS) and all(w & (w - 1) == 0 for w in POOL_WINDOWS)


def _params(*sem):
    return pltpu.CompilerParams(dimension_semantics=sem, vmem_limit_bytes=VMEM_LIMIT)


def _rms(x, g):
    ms = jnp.mean(x * x, axis=-1, keepdims=True)
    return x * lax.rsqrt(ms + EPS) * g


def _dot(a, b):
    return jnp.dot(a, b, preferred_element_type=F32)


class _Stream(NamedTuple):
    x: jax.Array
    seg: int
    ctx_row: int | None

    def tile(self):
        return min(IN_TILE, self.seg)

    def mod_row(self, tm):
        if self.ctx_row is not None:
            row = self.ctx_row
            return lambda tile: row
        per_seg = self.seg // tm
        return lambda tile: tile // per_seg


class _Tables(NamedTuple):
    mods: jax.Array
    gains: jax.Array


def _layer_spec(a, layer):
    nd = a.ndim
    return pl.BlockSpec((1,) + a.shape[1:], lambda *_: (layer,) + (0,) * (nd - 1), pipeline_mode=pl.Buffered(1))


def _const_spec(a):
    return pl.BlockSpec(a.shape, lambda *_: (0,) * a.ndim, pipeline_mode=pl.Buffered(1))


def _gain_spec(layer, j):
    return pl.BlockSpec((1, 1, D_MODEL), lambda *_: (layer * N_GAINS + j, 0, 0))


def _mod_spec(layer, chunk, row, tile=lambda i: i):
    return pl.BlockSpec((1, 1, D_MODEL), lambda i: (layer * MOD_ROWS + row(tile(i)), 0, chunk))


def _halo_specs(tm, rows, width, tile=lambda i: i):
    hb = tm // HALO
    return [
        pl.BlockSpec((HALO, width), lambda i: (jnp.maximum(tile(i) * hb - 1, 0), 0)),
        pl.BlockSpec((tm, width), lambda i: (tile(i), 0)),
        pl.BlockSpec((HALO, width), lambda i: (jnp.minimum((tile(i) + 1) * hb, rows // HALO - 1), 0)),
    ]


def _ext(prev_ref, main_ref, next_ref):
    return jnp.concatenate([prev_ref[...], main_ref[...], next_ref[...]], axis=0)


def _tile_pos(tm, seg):
    return (pl.program_id(0) * tm + lax.broadcasted_iota(jnp.int32, (tm, 1), 0)) % seg


def _seg_ext(x_ext, seg, tile=None):
    tm = x_ext.shape[0] - 2 * HALO
    i = pl.program_id(0) if tile is None else tile
    top_ok = (i * tm) % seg != 0
    bot_ok = ((i + 1) * tm) % seg != 0
    top, bot = x_ext[:HALO], x_ext[HALO + tm:]
    return jnp.concatenate([jnp.where(top_ok, top, jnp.zeros_like(top)), x_ext[HALO:HALO + tm],
                            jnp.where(bot_ok, bot, jnp.zeros_like(bot))], axis=0)


def _roll_rows(x_ext, off):
    rows = x_ext.shape[0]
    return pltpu.roll(x_ext, (-off) % rows, 0)


def _conv3(x_ext, seg, w, tile=None):
    tm = x_ext.shape[0] - 2 * HALO
    main = slice(HALO, HALO + tm)
    xm = _seg_ext(x_ext, seg, tile)
    prev, nxt = _roll_rows(xm, -1)[main], _roll_rows(xm, 1)[main]
    if seg < tm:
        pos = lax.broadcasted_iota(jnp.int32, (tm, 1), 0) % seg
        prev = jnp.where(pos == 0, 0.0, prev)
        nxt = jnp.where(pos == seg - 1, 0.0, nxt)
    return prev * w[0:1] + x_ext[main] * w[1:2] + nxt * w[2:3]


def _zero_anchor(v):
    fold = sum(v[r:r + 8] for r in range(0, v.shape[0], 8))
    fold = sum(fold[:, l:l + LANES] for l in range(0, fold.shape[1], LANES))
    return jnp.minimum(jnp.abs(fold[0:1]), 0.0)


def _modulated(x, gn_ref, sc_ref, sh_ref):
    ms = jnp.mean(x * x, axis=-1, keepdims=True)
    return (x * lax.rsqrt(ms + EPS) * (gn_ref[0] * (1.0 + sc_ref[0])) + sh_ref[0]).astype(BF16)


def _stream_specs(stream, tabs, layer):
    tm = stream.tile()
    rows, d = stream.x.shape
    row = stream.mod_row(tm)
    specs = _halo_specs(tm, rows, d) + [_gain_spec(layer, 0), _mod_spec(layer, 1, row), _mod_spec(layer, 0, row)]
    return specs, [stream.x] * 3 + [tabs.gains, tabs.mods, tabs.mods]


def _cast_rider(params, n_steps):
    in_specs, out_specs, out_shape = [], [], []
    for a, layer in params:
        r, c = a.shape[1:]
        chunk = r // n_steps
        assert chunk * n_steps == r and chunk % 16 == 0
        in_specs.append(pl.BlockSpec((1, chunk, c), lambda i, layer=layer: (layer, i, 0)))
        out_specs.append(pl.BlockSpec((1, chunk, c), lambda i: (0, i, 0)))
        out_shape.append(jax.ShapeDtypeStruct((1, r, c), BF16))
    return in_specs, out_specs, out_shape


def _run_casts(in_refs, out_refs):
    for src, dst in zip(in_refs, out_refs):
        dst[...] = src[...].astype(BF16)


def _cast_once(w_ref, w_scr):
    @pl.when(pl.program_id(0) == 0)
    def _():
        w_scr[...] = w_ref[0].astype(BF16)


def _mod_kernel(c_ref, w_ref, b_ref, o_ref):
    c = c_ref[...]
    s = (c / (1.0 + jnp.exp(-c))).astype(BF16)
    res = _dot(s, w_ref[0].astype(BF16)) + b_ref[0]
    for r in range(MOD_ROWS):
        o_ref[r] = res[r:r + 1]


def _modulation(cc, mod_w, mod_b):
    depth, d, n = mod_w.shape
    tn = 2048
    return pl.pallas_call(
        _mod_kernel,
        grid=(depth, n // tn),
        in_specs=[
            pl.BlockSpec((MOD_ROWS, d), lambda i, j: (0, 0)),
            pl.BlockSpec((1, d, tn), lambda i, j: (i, 0, j)),
            pl.BlockSpec((1, 1, tn), lambda i, j: (i, 0, j)),
        ],
        out_specs=pl.BlockSpec((MOD_ROWS, 1, tn), lambda i, j: (i, 0, j)),
        out_shape=jax.ShapeDtypeStruct((depth * MOD_ROWS, 1, n), F32),
        compiler_params=_params("arbitrary", "arbitrary"),
        name="modulation",
    )(cc, mod_w, mod_b.reshape(depth, 1, n))


def _rope128(x, cos, sin_a, sin_b):
    return x * cos + pltpu.roll(x, LANES - 16, 1) * sin_a + pltpu.roll(x, 16, 1) * sin_b


def _write_kv(k, v, k_ref, v_ref):
    lo = lax.broadcasted_iota(jnp.int32, (1, LANES), 1) < HEAD_DIM
    vr = pltpu.roll(v, HEAD_DIM, 1)
    zero = jnp.zeros_like(v)
    v_ref[...] = jnp.concatenate(
        [jnp.where(lo, v, zero), jnp.where(lo, zero, vr), jnp.where(lo, vr, zero), jnp.where(lo, zero, v)],
        axis=1).astype(BF16)
    kt = k.T
    top = lax.broadcasted_iota(jnp.int32, (LANES, 1), 0) < HEAD_DIM
    ktr = pltpu.roll(kt, HEAD_DIM, 0)
    zero = jnp.zeros_like(kt)
    k_ref[...] = jnp.concatenate(
        [jnp.where(top, kt, zero), jnp.where(top, zero, ktr), jnp.where(top, ktr, zero), jnp.where(top, zero, kt)],
        axis=0).astype(BF16)


def _in_even_kernel(xp_ref, x_ref, xn_ref, gn_ref, sc_ref, sh_ref, w_ref, cw_ref, *rest, rope, seg, n_cast):
    if rope:
        cos_ref, sa_ref, sb_ref = rest[:3]
        rest = rest[3:]
    cast_in, rest = rest[:n_cast], rest[n_cast:]
    q_ref, k_ref, v_ref, s_ref = rest[:4]
    _run_casts(cast_in, rest[4:4 + n_cast])
    w_scr = rest[-1]
    _cast_once(w_ref, w_scr)
    tm = x_ref.shape[0]
    hh = _modulated(_ext(xp_ref, x_ref, xn_ref), gn_ref, sc_ref, sh_ref)
    h = hh[HALO:HALO + tm]
    o = Q_DIM + 2 * KV_DIM
    cz = (_dot(hh, w_scr[:, o + SCONV_DIM:o + 2 * SCONV_DIM])
          * _dot(hh, w_scr[:, o + 2 * SCONV_DIM:o + 3 * SCONV_DIM]))
    q = _dot(h, w_scr[:, 0:Q_DIM])
    kv = _dot(h, w_scr[:, Q_DIM:Q_DIM + 2 * KV_DIM])
    k, v = kv[:, 0:KV_DIM], kv[:, KV_DIM:2 * KV_DIM]
    b = _dot(h, w_scr[:, o:o + SCONV_DIM])
    s_ref[...] = (b * _conv3(cz, seg, cw_ref[0])).astype(BF16)
    if rope:
        cos, sa, sb = cos_ref[...], sa_ref[...], sb_ref[...]
        k = _rope128(k, cos, sa, sb)
        q = jnp.concatenate(
            [_rope128(q[:, LANES * j:LANES * (j + 1)], cos, sa, sb) for j in range(Q_DIM // LANES)], axis=1)
    q_ref[...] = (q * (HEAD_DIM ** -0.5 * LOG2E)).astype(BF16)
    _write_kv(k, v, k_ref, v_ref)


def _in_even(stream, tabs, layer, e, w_in, conv_w, rope_tables, casts=()):
    rows = stream.x.shape[0]
    tm = stream.tile()
    per_seg = stream.seg // tm
    rope = rope_tables is not None
    row = lambda i: (i, 0)
    in_specs, args = _stream_specs(stream, tabs, layer)
    in_specs += [_layer_spec(w_in, e), _layer_spec(conv_w, e)]
    args += [w_in, conv_w]
    if rope:
        in_specs += [pl.BlockSpec((tm, LANES), lambda i: (i % per_seg, 0))] * 3
        args += list(rope_tables)
    cast_in, cast_out, cast_shape = _cast_rider(casts, rows // tm)
    return pl.pallas_call(
        functools.partial(_in_even_kernel, rope=rope, seg=stream.seg, n_cast=len(casts)),
        grid=(rows // tm,),
        in_specs=in_specs + cast_in,
        out_specs=[pl.BlockSpec((tm, Q_DIM), row), pl.BlockSpec((KV_TILES, tm), lambda i: (0, i)),
                   pl.BlockSpec((tm, KV_TILES), row), pl.BlockSpec((tm, SCONV_DIM), row)] + cast_out,
        out_shape=[jax.ShapeDtypeStruct((rows, Q_DIM), BF16), jax.ShapeDtypeStruct((KV_TILES, rows), BF16),
                   jax.ShapeDtypeStruct((rows, KV_TILES), BF16), jax.ShapeDtypeStruct((rows, SCONV_DIM), BF16)]
        + cast_shape,
        scratch_shapes=[pltpu.VMEM(w_in.shape[1:], BF16)],
        compiler_params=_params("arbitrary"),
        name="in_even",
    )(*args, *[a for a, _ in casts])


def _in_kv_kernel(x_ref, gn_ref, sc_ref, sh_ref, w_ref, k_ref, v_ref):
    h = _modulated(x_ref[...], gn_ref, sc_ref, sh_ref)
    kv = _dot(h, w_ref[0].astype(BF16))
    _write_kv(kv[:, 0:KV_DIM], kv[:, KV_DIM:2 * KV_DIM], k_ref, v_ref)


def _in_kv(stream, tabs, layer, e, w_in):
    rows, d = stream.x.shape
    assert stream.ctx_row is not None and rows % IN_TILE == 0
    tm = IN_TILE
    row = lambda i: (i, 0)
    mrow = stream.mod_row(tm)
    return pl.pallas_call(
        _in_kv_kernel,
        grid=(rows // tm,),
        in_specs=[pl.BlockSpec((tm, d), row), _gain_spec(layer, 0), _mod_spec(layer, 1, mrow),
                  _mod_spec(layer, 0, mrow),
                  pl.BlockSpec((1, d, 2 * KV_DIM), lambda i: (e, 0, Q_DIM // (2 * KV_DIM)))],
        out_specs=[pl.BlockSpec((KV_TILES, tm), lambda i: (0, i)), pl.BlockSpec((tm, KV_TILES), row)],
        out_shape=[jax.ShapeDtypeStruct((KV_TILES, rows), BF16), jax.ShapeDtypeStruct((rows, KV_TILES), BF16)],
        compiler_params=_params("arbitrary"),
        name="in_kv",
    )(stream.x, tabs.gains, tabs.mods, tabs.mods, w_in)


def _in_odd_kernel(xp_ref, x_ref, xn_ref, gn_ref, sc_ref, sh_ref, w_ref, ccs_ref, wg_ref, ps_ref,
                   p_ref, ab_ref, w_scr, *, seg):
    _cast_once(w_ref, w_scr)
    tm = x_ref.shape[0]
    hh = _modulated(_ext(xp_ref, x_ref, xn_ref), gn_ref, sc_ref, sh_ref)
    h = hh[HALO:HALO + tm]
    up = _seg_ext(_dot(hh, w_scr[:, 0:POOL_DIM]), seg)
    uf = _dot(h, w_scr[:, POOL_DIM:POOL_DIM + FFT_DIM]).astype(BF16)
    ccs = ccs_ref[...]
    both = [_dot(uf[:, g * FFT_GROUP:(g + 1) * FFT_GROUP], ccs) for g in range(N_FFT_GROUPS)]
    ab_ref[...] = jnp.concatenate([r[:, 0:FFT_GROUP] for r in both] + [r[:, FFT_GROUP:] for r in both],
                                  axis=1).astype(BF16)
    pos = _tile_pos(tm, seg)
    pooled = []
    for gi, w in enumerate(POOL_WINDOWS):
        ug = up[:, gi * POOL_GROUP:(gi + 1) * POOL_GROUP]
        back, fwd = w // 2, w - w // 2
        run, span = ug, 1
        while span < w:
            run = run + _roll_rows(run, span)
            span *= 2
        total = _roll_rows(run, -back)[HALO:HALO + tm]
        count = (jnp.minimum(pos + fwd, seg) - jnp.maximum(pos - back, 0)).astype(F32)
        pooled.append((total / count - ug[HALO:HALO + tm]).astype(BF16))
    outs = [_dot(jnp.concatenate(pooled[2 * j:2 * j + 2], axis=1), wg_ref[0, j]) for j in range(N_POOL_GROUPS // 2)]
    p_ref[...] = (jnp.concatenate(outs, axis=1) * ps_ref[0]).astype(BF16)


def _in_odd(stream, tabs, layer, o, w_in, ccs, wg, ps):
    rows = stream.x.shape[0]
    tm = stream.tile()
    row = lambda i: (i, 0)
    in_specs, args = _stream_specs(stream, tabs, layer)
    return pl.pallas_call(
        functools.partial(_in_odd_kernel, seg=stream.seg),
        grid=(rows // tm,),
        in_specs=in_specs + [_layer_spec(w_in, o), _const_spec(ccs), _layer_spec(wg, o), _layer_spec(ps, o)],
        out_specs=[pl.BlockSpec((tm, POOL_DIM), row), pl.BlockSpec((tm, 2 * FFT_DIM), row)],
        out_shape=[jax.ShapeDtypeStruct((rows, POOL_DIM), BF16), jax.ShapeDtypeStruct((rows, 2 * FFT_DIM), BF16)],
        scratch_shapes=[pltpu.VMEM(w_in.shape[1:], BF16)],
        compiler_params=_params("arbitrary"),
        name="in_odd",
    )(*args, w_in, ccs, wg, ps)


def _attn_kernel(sink_ref, q_ref, *rest, window, qb, nblocks, e):
    if window:
        ktp, ktc, ktn, vp, vc, vn, kxt, vx, o_ref = rest
    else:
        kxt, vx, o_ref = rest
    n = pl.program_id(1)
    rows = 2 * BLOCK
    first_pair = lax.broadcasted_iota(jnp.int32, (rows, 1), 0) < BLOCK
    lo = lax.broadcasted_iota(jnp.int32, (1, LANES), 1) < HEAD_DIM
    if window:
        qi = lax.broadcasted_iota(jnp.int32, (rows, BLOCK), 0) % BLOCK
        j = lax.broadcasted_iota(jnp.int32, (rows, BLOCK), 1)
        tri_prev = jnp.where(j >= qi, 0.0, MASK_VALUE)
        tri_next = jnp.where(j <= qi, 0.0, MASK_VALUE)
        bias = []
        for t in range(qb):
            blk = n * qb + t
            bias.append((tri_prev + jnp.where(blk >= 1, 0.0, MASK_VALUE),
                         tri_next + jnp.where(blk <= nblocks - 2, 0.0, MASK_VALUE)))

    def lane_halves(h):
        c0 = 2 * LANES * h
        return slice(c0, c0 + LANES), slice(c0 + LANES, c0 + 2 * LANES)

    def keys_values(h, t):
        kparts, vparts = [], []
        for sl in lane_halves(h):
            if window:
                kall = jnp.concatenate([ktp[sl, :], ktc[sl, :], ktn[sl, :]], axis=1)
                vall = jnp.concatenate([vp[0, :, sl], vc[0, :, sl], vn[0, :, sl]], axis=0)
                kparts.append(kall[:, t * BLOCK:(t + 3) * BLOCK])
                vparts.append(vall[t * BLOCK:(t + 3) * BLOCK])
            kparts.append(kxt[sl, :])
            vparts.append(vx[0, :, sl])
        return jnp.concatenate(kparts, axis=1), jnp.concatenate(vparts, axis=0)

    def logits(h, t):
        rs = slice(t * BLOCK, (t + 1) * BLOCK)
        lo_sl, hi_sl = lane_halves(h)
        qp = jnp.concatenate([q_ref[0, rs, lo_sl], q_ref[0, rs, hi_sl]], axis=0)
        return _dot(qp, keys_values(h, t)[0])

    def finish(h, t, s):
        vst = keys_values(h, t)[1]
        nkeys = vst.shape[0] // 2
        probs, inv = [], []
        for half in range(2):
            tiles = [s[:, half * nkeys + c * LANES:half * nkeys + (c + 1) * LANES] for c in range(nkeys // LANES)]
            if window:
                tiles[0] = tiles[0] + bias[t][0]
                tiles[2] = tiles[2] + bias[t][1]
            head = 4 * h + half
            sink_col = jnp.where(first_pair, sink_ref[e, head], sink_ref[e, head + 2]) * LOG2E
            m = jnp.maximum(jnp.max(functools.reduce(jnp.maximum, tiles), axis=-1, keepdims=True), sink_col)
            es = [jnp.exp2(tl - m) for tl in tiles]
            den = jnp.sum(functools.reduce(jnp.add, es), axis=-1, keepdims=True) + jnp.exp2(sink_col - m)
            probs += [e_.astype(BF16) for e_ in es]
            inv.append(1.0 / den)
        o = _dot(jnp.concatenate(probs, axis=1), vst) * jnp.where(lo, inv[0], inv[1])
        rs = slice(t * BLOCK, (t + 1) * BLOCK)
        lo_sl, hi_sl = lane_halves(h)
        o_ref[0, rs, lo_sl] = o[:BLOCK].astype(o_ref.dtype)
        o_ref[0, rs, hi_sl] = o[BLOCK:].astype(o_ref.dtype)

    units = [(h, t) for h in range(N_KV_HEADS) for t in range(qb)]
    s_next = logits(*units[0])
    for u, unit in enumerate(units):
        s_cur = s_next
        if u + 1 < len(units):
            s_next = logits(*units[u + 1])
        finish(*unit, s_cur)


def _attention(sinks, e, q, kt, v, kxt, vx, window):
    b, t, _ = q.shape
    l = vx.shape[1]
    nb = t // BLOCK
    qb = ATTN_QBLOCKS if window else nb
    w = vx.shape[2]
    steps = nb // qb
    cur = lambda i, n: (i, n, 0)
    in_specs = [pl.BlockSpec(memory_space=pltpu.SMEM), pl.BlockSpec((1, qb * BLOCK, Q_DIM), cur)]
    args = [sinks, q]
    if window:
        pb = lambda n: jnp.maximum(n * qb - 1, 0)
        nx = lambda n: jnp.minimum(n * qb + qb, nb - 1)
        in_specs += [
            pl.BlockSpec((w, BLOCK), lambda i, n: (0, i * nb + pb(n))),
            pl.BlockSpec((w, qb * BLOCK), lambda i, n: (0, i * steps + n)),
            pl.BlockSpec((w, BLOCK), lambda i, n: (0, i * nb + nx(n))),
            pl.BlockSpec((1, BLOCK, w), lambda i, n: (i, pb(n), 0)),
            pl.BlockSpec((1, qb * BLOCK, w), cur),
            pl.BlockSpec((1, BLOCK, w), lambda i, n: (i, nx(n), 0)),
        ]
        args += [kt, kt, kt, v, v, v]
    in_specs += [pl.BlockSpec((w, l), lambda i, n: (0, i)), pl.BlockSpec((1, l, w), lambda i, n: (i, 0, 0))]
    args += [kxt, vx]
    return pl.pallas_call(
        functools.partial(_attn_kernel, window=window, qb=qb, nblocks=nb, e=e),
        grid=(b, steps),
        in_specs=in_specs,
        out_specs=pl.BlockSpec((1, qb * BLOCK, Q_DIM), cur),
        out_shape=jax.ShapeDtypeStruct((b, t, Q_DIM), BF16),
        compiler_params=_params("arbitrary", "arbitrary"),
        name="attention",
    )(*args)


def _seq_dft_kernel(ct_ref, st_ref, flip_ref, ab_ref, *rest, norm, n_cast):
    o_ref = rest[n_cast]
    _run_casts(rest[:n_cast], rest[n_cast + 1:])
    t = ab_ref.shape[1]
    half = t // 2
    a = ab_ref[0, :, 0:FFT_DIM]
    b = ab_ref[0, :, FFT_DIM:2 * FFT_DIM]
    p = _dot(ct_ref[...], a)
    q = _dot(st_ref[...], b)
    o_ref[0, 0:half] = ((p - q) * norm).astype(o_ref.dtype)
    sign = 1.0 - 2.0 * (lax.broadcasted_iota(jnp.int32, (t, 1), 0) % 2).astype(F32)
    nyquist = jnp.sum(a.astype(F32) * sign, axis=0, keepdims=True)
    first = lax.broadcasted_iota(jnp.int32, (half, 1), 0) == 0
    mirrored = (jnp.where(first, nyquist, p + q) * norm).astype(BF16)
    o_ref[0, half:t] = _dot(flip_ref[...], mirrored).astype(o_ref.dtype)


def _seq_dft(ab3, tables, casts=()):
    b, t, _ = ab3.shape
    cast_in, cast_out, cast_shape = _cast_rider(casts, b)
    return pl.pallas_call(
        functools.partial(_seq_dft_kernel, norm=float((t * FFT_GROUP) ** -0.5), n_cast=len(casts)),
        grid=(b,),
        in_specs=([_const_spec(a) for a in tables] + [pl.BlockSpec((1, t, 2 * FFT_DIM), lambda i: (i, 0, 0))]
                  + cast_in),
        out_specs=[pl.BlockSpec((1, t, FFT_DIM), lambda i: (i, 0, 0))] + cast_out,
        out_shape=[jax.ShapeDtypeStruct((b, t, FFT_DIM), BF16)] + cast_shape,
        compiler_params=_params("arbitrary"),
        name="seq_dft",
    )(*tables, ab3, *[a for a, _ in casts])


def _post_kernel(ap_ref, a_ref, an_ref, bp_ref, b_ref, bn_ref, xp_ref, x_ref, xn_ref,
                 wo_ref, gn1_ref, g1_ref, gn2_ref, sc_ref, sh_ref, g2_ref, gn3_ref,
                 wu_hbm, cw_ref, wd_hbm, o_ref, y_scr, hh_scr, xm_scr, act_scr, wu_ref, wd_ref, w_sem, *, seg):
    i = pl.program_id(0)
    tm = x_ref.shape[0]
    main = slice(HALO, HALO + tm)
    rd = i % 2
    wr = 1 - rd
    piece_rows = tm // POST_PIECES
    pieces = ([(0, xp_ref, 0, HALO)]
              + [(HALO + k * piece_rows, x_ref, k * piece_rows, piece_rows) for k in range(POST_PIECES)]
              + [(HALO + tm, xn_ref, 0, HALO)])
    ka = a_ref.shape[1]

    def prepare(piece):
        lo, ref, off, nrows = piece
        xm = ref[off:off + nrows] + g1_ref[0] * _rms(y_scr[lo:lo + nrows], gn1_ref[0])
        hn = _rms(xm, gn2_ref[0]) * (1.0 + sc_ref[0]) + sh_ref[0]
        hh_scr[wr, lo:lo + nrows] = hn.astype(BF16)
        if ref is x_ref:
            xm_scr[wr, off:off + nrows] = xm
        return jnp.concatenate([_zero_anchor(hn)] * (FFN_CHUNK // LANES), axis=1)

    def out_projection():
        y_scr[...] = (_dot(_ext(ap_ref, a_ref, an_ref), wo_ref[0, 0:ka, :])
                      + _dot(_ext(bp_ref, b_ref, bn_ref), wo_ref[0, ka:, :]))

    weight_copies = (pltpu.make_async_copy(wu_hbm, wu_ref, w_sem.at[0]),
                     pltpu.make_async_copy(wd_hbm, wd_ref, w_sem.at[1]))

    @pl.when(i == 0)
    def _():
        for copy in weight_copies:
            copy.start()
        y_scr[...] = jnp.zeros_like(y_scr)

    @pl.when(i == 2)
    def _():
        for copy in weight_copies:
            copy.wait()

    @pl.when(i < 2)
    def _():
        for piece in pieces:
            prepare(piece)
        out_projection()

    @pl.when(i >= 2)
    def _():
        hh = hh_scr[rd]
        h = hh[main]
        cw = cw_ref[0]
        anchor = None
        for c in range(D_FF // FFN_CHUNK):
            cs = slice(c * FFN_CHUNK, (c + 1) * FFN_CHUNK)
            cwc = cw[:, cs] if anchor is None else cw[:, cs] + anchor
            conv = _conv3(_dot(hh, wu_ref[0, :, cs]), seg, cwc, tile=i - 2)
            val = _dot(h, wu_ref[0, :, D_FF + c * FFN_CHUNK:D_FF + (c + 1) * FFN_CHUNK])
            act_scr[:, cs] = (conv / (1.0 + jnp.exp(-conv)) * val).astype(BF16)
            anchor = prepare(pieces[c]) if c < len(pieces) else None
        y2 = _dot(act_scr[...], wd_ref[0])
        out_projection()
        o_ref[...] = xm_scr[rd] + g2_ref[0] * _rms(y2, gn3_ref[0])


def _post(a2, b2, stream, tabs, layer, w_out, w_up, conv_w, w_down):
    x2 = stream.x
    rows, d = x2.shape
    tm = ROW_TILE
    seg = stream.seg
    assert seg % tm == 0 or tm % seg == 0
    n = rows // tm
    stage_a = lambda i: jnp.minimum(i, n - 1)
    stage_b = lambda i: jnp.clip(i - 1, 0, n - 1)
    stage_c = lambda i: jnp.maximum(i - 2, 0)
    row = stream.mod_row(min(tm, seg))
    mod = lambda chunk, stage: _mod_spec(layer, chunk, row, stage)
    return pl.pallas_call(
        functools.partial(_post_kernel, seg=seg),
        grid=(n + 2,),
        in_specs=(_halo_specs(tm, rows, a2.shape[1], stage_a) + _halo_specs(tm, rows, b2.shape[1], stage_a)
                  + _halo_specs(tm, rows, d, stage_b)
                  + [_layer_spec(w_out, 0), _gain_spec(layer, 1), mod(2, stage_b), _gain_spec(layer, 2),
                     mod(4, stage_b), mod(3, stage_b), mod(5, stage_c), _gain_spec(layer, 3),
                     pl.BlockSpec(memory_space=pl.ANY), _layer_spec(conv_w, layer),
                     pl.BlockSpec(memory_space=pl.ANY)]),
        out_specs=pl.BlockSpec((tm, d), lambda i: (stage_c(i), 0)),
        out_shape=jax.ShapeDtypeStruct((rows, d), F32),
        scratch_shapes=[pltpu.VMEM((tm + 2 * HALO, d), F32), pltpu.VMEM((2, tm + 2 * HALO, d), BF16),
                        pltpu.VMEM((2, tm, d), F32), pltpu.VMEM((tm, D_FF), BF16),
                        pltpu.VMEM(w_up.shape, BF16), pltpu.VMEM(w_down.shape, BF16),
                        pltpu.SemaphoreType.DMA((2,))],
        compiler_params=_params("arbitrary"),
        name="post",
    )(a2, a2, a2, b2, b2, b2, x2, x2, x2, w_out, tabs.gains, tabs.mods, tabs.gains, tabs.mods, tabs.mods,
      tabs.mods, tabs.gains, w_up, conv_w, w_down)


def _rope_tables(seq):
    half = HEAD_DIM // 2
    nf = half // 2
    inv = ROPE_BASE ** (-np.arange(nf, dtype=np.float64) / nf)
    t = np.arange(seq)
    lane = np.arange(LANES)
    in_head = lane % HEAD_DIM
    pos = np.where((in_head // half)[None, :] == 0, (t // GRID_W)[:, None], (t % GRID_W)[:, None])
    ang = pos * inv[lane % nf][None, :]
    first = ((lane % half) < nf)[None, :]
    cos = np.cos(ang)
    sin_a = np.where(first, -np.sin(ang), 0.0)
    sin_b = np.where(first, 0.0, np.sin(ang))
    return tuple(jnp.asarray(a, F32) for a in (cos, sin_a, sin_b))


def _dft_tables(n, rows=None):
    k = np.arange(n if rows is None else rows)
    ang = 2.0 * np.pi * ((k[:, None] * np.arange(n)[None, :]) % n) / n
    return jnp.asarray(np.cos(ang), F32).astype(BF16), jnp.asarray(np.sin(ang), F32).astype(BF16)


def _seq_dft_tables(n):
    half = n // 2
    flip = np.zeros((half, half), np.float32)
    flip[np.arange(half), (half - np.arange(half)) % half] = 1.0
    return _dft_tables(n, half) + (jnp.asarray(flip).astype(BF16),)


def kernel(x, c, ctx, c_ctx, mod_w, mod_b, norm_g, att_w_in, att_sink, sconv_w, att_w_out, mix_w_in,
           pool_w_grp, pool_scale, mix_w_out, ffn_w_up, ffn_conv, ffn_w_down):
    bsz, seq, d = x.shape
    ctx_len = ctx.shape[1]
    assert d == D_MODEL and seq % ROW_TILE == 0 and (bsz * ctx_len) % ROW_TILE == 0 and ROW_TILE % ctx_len == 0
    assert bsz < MOD_ROWS and norm_g.shape[1] == N_GAINS

    cc = jnp.concatenate([c, c_ctx[None, :], jnp.zeros((MOD_ROWS - bsz - 1, d), F32)], axis=0)
    mods = _modulation(cc, mod_w, mod_b)
    tabs = _Tables(mods, norm_g.reshape(DEPTH * N_GAINS, 1, d))

    rope = _rope_tables(seq)
    chan_cs = jnp.concatenate(_dft_tables(FFT_GROUP), axis=1)
    seq_tabs = {seq: _seq_dft_tables(seq), ctx_len: _seq_dft_tables(ctx_len)}

    w_att_in, w_mix_in = att_w_in, mix_w_in
    g4 = pool_w_grp.astype(BF16)
    zero = jnp.zeros_like(g4[:, 0])
    w_grp = jnp.stack([jnp.concatenate([jnp.concatenate([g4[:, 2 * j], zero], axis=2),
                                        jnp.concatenate([zero, g4[:, 2 * j + 1]], axis=2)], axis=1)
                       for j in range(N_POOL_GROUPS // 2)], axis=1)
    p_scale = pool_scale.reshape(-1, 1, POOL_DIM)

    xs = x.reshape(bsz * seq, d)
    hc = ctx.reshape(bsz * ctx_len, d)

    for i in range(DEPTH):
        even = i % 2 == 0
        ctx_live = any(j % 2 == 0 for j in range(i + 1, DEPTH))
        lat = _Stream(xs, seq, None)
        con = _Stream(hc, ctx_len, bsz)
        if even:
            e = i // 2
            casts = ((att_w_out, e), (ffn_w_up, i), (ffn_w_down, i))
            q, kt, v, sx, w_out, w_up, w_down = _in_even(lat, tabs, i, e, w_att_in, sconv_w, rope, casts)
            post = (w_out, w_up, ffn_conv, w_down)
            if ctx_live:
                qc, kct, vc, sxc = _in_even(con, tabs, i, e, w_att_in, sconv_w, None)
            else:
                kct, vc = _in_kv(con, tabs, i, e, w_att_in)
            vc3 = vc.reshape(bsz, ctx_len, -1)
            ax = _attention(att_sink, e, q.reshape(bsz, seq, -1), kt, v.reshape(bsz, seq, -1), kct, vc3, True)
            xs = _post(ax.reshape(bsz * seq, -1), sx, lat, tabs, i, *post)
            if ctx_live:
                ac = _attention(att_sink, e, qc.reshape(bsz, ctx_len, -1), None, None, kct, vc3, False)
                hc = _post(ac.reshape(bsz * ctx_len, -1), sxc, con, tabs, i, *post)
        else:
            o = i // 2
            casts = ((mix_w_out, o), (ffn_w_up, i), (ffn_w_down, i))
            yp, ab = _in_odd(lat, tabs, i, o, w_mix_in, chan_cs, w_grp, p_scale)
            yf, w_out, w_up, w_down = _seq_dft(ab.reshape(bsz, seq, -1), seq_tabs[seq], casts)
            post = (w_out, w_up, ffn_conv, w_down)
            if ctx_live:
                ypc, abc = _in_odd(con, tabs, i, o, w_mix_in, chan_cs, w_grp, p_scale)
                yfc, = _seq_dft(abc.reshape(bsz, ctx_len, -1), seq_tabs[ctx_len])
            xs = _post(yp, yf.reshape(bsz * seq, -1), lat, tabs, i, *post)
            if ctx_live:
                hc = _post(ypc, yfc.reshape(bsz * ctx_len, -1), con, tabs, i, *post)
    return xs.reshape(bsz, seq, d)
```

```python
import functools
from typing import NamedTuple

import numpy as np
import jax
import jax.numpy as jnp
from jax import lax
from jax.experimental import pallas as pl
from jax.experimental.pallas import tpu as pltpu

F32 = jnp.float32
BF16 = jnp.bfloat16

D_MODEL = 1024
DEPTH = 4
GRID_W = 64
EPS = 1e-6
N_Q_HEADS = 8
N_KV_HEADS = 2
HEAD_DIM = 64
WINDOW = 128
BLOCK = 128
ROPE_BASE = 10000.0
SCONV_DIM = 512
POOL_DIM = 512
N_POOL_GROUPS = 4
POOL_GROUP = POOL_DIM // N_POOL_GROUPS
POOL_WINDOWS = (2, 4, 8, 16)
FFT_DIM = 512
N_FFT_GROUPS = 4
FFT_GROUP = FFT_DIM // N_FFT_GROUPS
D_FF = 2816
Q_DIM = N_Q_HEADS * HEAD_DIM
KV_DIM = N_KV_HEADS * HEAD_DIM
KV_TILES = 2 * KV_DIM * 2

LANES = 128
MASK_VALUE = -1e30
LOG2E = 1.4426950408889634
ATTN_QBLOCKS = 16
VMEM_LIMIT = 56 * 1024 * 1024
ROW_TILE = 512
IN_TILE = 1024
HALO = 16
FFN_CHUNK = 256
POST_PIECES = 8
MOD_ROWS = 16
N_GAINS = 4

assert HALO >= max(POOL_WINDOWS) and all(w & (w - 1) == 0 for w in POOL_WINDOWS)


def _params(*sem):
    return pltpu.CompilerParams(dimension_semantics=sem, vmem_limit_bytes=VMEM_LIMIT)


def _rms(x, g):
    ms = jnp.mean(x * x, axis=-1, keepdims=True)
    return x * lax.rsqrt(ms + EPS) * g


def _dot(a, b):
    return jnp.dot(a, b, preferred_element_type=F32)


class _Stream(NamedTuple):
    x: jax.Array
    seg: int
    ctx_row: int | None

    def tile(self, span_segments=False):
        if span_segments and self.seg < IN_TILE and self.x.shape[0] % IN_TILE == 0 and IN_TILE % self.seg == 0:
            return IN_TILE
        return min(IN_TILE, self.seg)

    def mod_row(self, tm):
        if self.ctx_row is not None:
            row = self.ctx_row
            return lambda tile: row
        per_seg = self.seg // tm
        return lambda tile: tile // per_seg


class _Tables(NamedTuple):
    mods: jax.Array
    gains: jax.Array


def _layer_spec(a, layer):
    nd = a.ndim
    return pl.BlockSpec((1,) + a.shape[1:], lambda *_: (layer,) + (0,) * (nd - 1), pipeline_mode=pl.Buffered(1))


def _const_spec(a):
    return pl.BlockSpec(a.shape, lambda *_: (0,) * a.ndim, pipeline_mode=pl.Buffered(1))


def _gain_spec(layer, j):
    return pl.BlockSpec((1, 1, D_MODEL), lambda *_: (layer * N_GAINS + j, 0, 0))


def _mod_spec(layer, chunk, row, tile=lambda i: i):
    return pl.BlockSpec((1, 1, D_MODEL), lambda i: (layer * MOD_ROWS + row(tile(i)), 0, chunk))


def _halo_specs(tm, rows, width, tile=lambda i: i):
    hb = tm // HALO
    return [
        pl.BlockSpec((HALO, width), lambda i: (jnp.maximum(tile(i) * hb - 1, 0), 0)),
        pl.BlockSpec((tm, width), lambda i: (tile(i), 0)),
        pl.BlockSpec((HALO, width), lambda i: (jnp.minimum((tile(i) + 1) * hb, rows // HALO - 1), 0)),
    ]


def _ext(prev_ref, main_ref, next_ref):
    return jnp.concatenate([prev_ref[...], main_ref[...], next_ref[...]], axis=0)


def _tile_pos(tm, seg):
    return (pl.program_id(0) * tm + lax.broadcasted_iota(jnp.int32, (tm, 1), 0)) % seg


def _seg_ext(x_ext, seg, tile=None):
    tm = x_ext.shape[0] - 2 * HALO
    i = pl.program_id(0) if tile is None else tile
    top_ok = (i * tm) % seg != 0
    bot_ok = ((i + 1) * tm) % seg != 0
    top, bot = x_ext[:HALO], x_ext[HALO + tm:]
    return jnp.concatenate([jnp.where(top_ok, top, jnp.zeros_like(top)), x_ext[HALO:HALO + tm],
                            jnp.where(bot_ok, bot, jnp.zeros_like(bot))], axis=0)


def _roll_rows(x_ext, off):
    rows = x_ext.shape[0]
    return pltpu.roll(x_ext, (-off) % rows, 0)


def _conv3(x_ext, seg, w, tile=None):
    tm = x_ext.shape[0] - 2 * HALO
    main = slice(HALO, HALO + tm)
    xm = _seg_ext(x_ext, seg, tile)
    prev, nxt = _roll_rows(xm, -1)[main], _roll_rows(xm, 1)[main]
    if seg < tm:
        pos = lax.broadcasted_iota(jnp.int32, (tm, 1), 0) % seg
        prev = jnp.where(pos == 0, 0.0, prev)
        nxt = jnp.where(pos == seg - 1, 0.0, nxt)
    return prev * w[0:1] + x_ext[main] * w[1:2] + nxt * w[2:3]


def _zero_anchor(v):
    fold = sum(v[r:r + 8] for r in range(0, v.shape[0], 8))
    fold = sum(fold[:, l:l + LANES] for l in range(0, fold.shape[1], LANES))
    return jnp.minimum(jnp.abs(fold[0:1]), 0.0)


def _modulated(x, gn_ref, sc_ref, sh_ref):
    ms = jnp.mean(x * x, axis=-1, keepdims=True)
    return (x * lax.rsqrt(ms + EPS) * (gn_ref[0] * (1.0 + sc_ref[0])) + sh_ref[0]).astype(BF16)


def _stream_specs(stream, tabs, layer, tm):
    rows, d = stream.x.shape
    row = stream.mod_row(tm)
    specs = _halo_specs(tm, rows, d) + [_gain_spec(layer, 0), _mod_spec(layer, 1, row), _mod_spec(layer, 0, row)]
    return specs, [stream.x] * 3 + [tabs.gains, tabs.mods, tabs.mods]


def _cast_rider(params, n_steps):
    in_specs, out_specs, out_shape = [], [], []
    for a, layer in params:
        r, c = a.shape[1:]
        chunk = r // n_steps
        assert chunk * n_steps == r and chunk % 16 == 0
        in_specs.append(pl.BlockSpec((1, chunk, c), lambda i, layer=layer: (layer, i, 0)))
        out_specs.append(pl.BlockSpec((1, chunk, c), lambda i: (0, i, 0)))
        out_shape.append(jax.ShapeDtypeStruct((1, r, c), BF16))
    return in_specs, out_specs, out_shape


def _run_casts(in_refs, out_refs):
    for src, dst in zip(in_refs, out_refs):
        dst[...] = src[...].astype(BF16)


def _cast_once(w_ref, w_scr):
    @pl.when(pl.program_id(0) == 0)
    def _():
        w_scr[...] = w_ref[0].astype(BF16)


def _mod_kernel(c_ref, w_ref, b_ref, o_ref):
    c = c_ref[...]
    s = (c / (1.0 + jnp.exp(-c))).astype(BF16)
    res = _dot(s, w_ref[0].astype(BF16)) + b_ref[0]
    for r in range(MOD_ROWS):
        o_ref[r] = res[r:r + 1]


def _modulation(cc, mod_w, mod_b):
    depth, d, n = mod_w.shape
    tn = 2048
    return pl.pallas_call(
        _mod_kernel,
        grid=(depth, n // tn),
        in_specs=[
            pl.BlockSpec((MOD_ROWS, d), lambda i, j: (0, 0)),
            pl.BlockSpec((1, d, tn), lambda i, j: (i, 0, j)),
            pl.BlockSpec((1, 1, tn), lambda i, j: (i, 0, j)),
        ],
        out_specs=pl.BlockSpec((MOD_ROWS, 1, tn), lambda i, j: (i, 0, j)),
        out_shape=jax.ShapeDtypeStruct((depth * MOD_ROWS, 1, n), F32),
        compiler_params=_params("arbitrary", "arbitrary"),
        name="modulation",
    )(cc, mod_w, mod_b.reshape(depth, 1, n))


def _rope128(x, cos, sin_a, sin_b):
    return x * cos + pltpu.roll(x, LANES - 16, 1) * sin_a + pltpu.roll(x, 16, 1) * sin_b


def _write_kv(k, v, k_ref, v_ref):
    lo = lax.broadcasted_iota(jnp.int32, (1, LANES), 1) < HEAD_DIM
    vr = pltpu.roll(v, HEAD_DIM, 1)
    zero = jnp.zeros_like(v)
    v_ref[...] = jnp.concatenate(
        [jnp.where(lo, v, zero), jnp.where(lo, zero, vr), jnp.where(lo, vr, zero), jnp.where(lo, zero, v)],
        axis=1).astype(BF16)
    kt = k.T
    top = lax.broadcasted_iota(jnp.int32, (LANES, 1), 0) < HEAD_DIM
    ktr = pltpu.roll(kt, HEAD_DIM, 0)
    zero = jnp.zeros_like(kt)
    k_ref[...] = jnp.concatenate(
        [jnp.where(top, kt, zero), jnp.where(top, zero, ktr), jnp.where(top, ktr, zero), jnp.where(top, zero, kt)],
        axis=0).astype(BF16)


def _in_even_kernel(xp_ref, x_ref, xn_ref, gn_ref, sc_ref, sh_ref, w_ref, cw_ref, *rest, rope, seg, n_cast):
    if rope:
        cos_ref, sa_ref, sb_ref = rest[:3]
        rest = rest[3:]
    cast_in, rest = rest[:n_cast], rest[n_cast:]
    q_ref, k_ref, v_ref, s_ref = rest[:4]
    _run_casts(cast_in, rest[4:4 + n_cast])
    w_scr = rest[-1]
    _cast_once(w_ref, w_scr)
    tm = x_ref.shape[0]
    hh = _modulated(_ext(xp_ref, x_ref, xn_ref), gn_ref, sc_ref, sh_ref)
    h = hh[HALO:HALO + tm]
    o = Q_DIM + 2 * KV_DIM
    cz = (_dot(hh, w_scr[:, o + SCONV_DIM:o + 2 * SCONV_DIM])
          * _dot(hh, w_scr[:, o + 2 * SCONV_DIM:o + 3 * SCONV_DIM]))
    q = _dot(h, w_scr[:, 0:Q_DIM])
    kv = _dot(h, w_scr[:, Q_DIM:Q_DIM + 2 * KV_DIM])
    k, v = kv[:, 0:KV_DIM], kv[:, KV_DIM:2 * KV_DIM]
    b = _dot(h, w_scr[:, o:o + SCONV_DIM])
    s_ref[...] = (b * _conv3(cz, seg, cw_ref[0])).astype(BF16)
    if rope:
        cos, sa, sb = cos_ref[...], sa_ref[...], sb_ref[...]
        k = _rope128(k, cos, sa, sb)
        q = jnp.concatenate(
            [_rope128(q[:, LANES * j:LANES * (j + 1)], cos, sa, sb) for j in range(Q_DIM // LANES)], axis=1)
    q_ref[...] = (q * (HEAD_DIM ** -0.5 * LOG2E)).astype(BF16)
    _write_kv(k, v, k_ref, v_ref)


def _in_even(stream, tabs, layer, e, w_in, conv_w, rope_tables, casts=()):
    rows = stream.x.shape[0]
    tm = stream.tile(span_segments=True)
    per_seg = max(stream.seg // tm, 1)
    rope = rope_tables is not None
    assert not rope or tm <= stream.seg
    row = lambda i: (i, 0)
    in_specs, args = _stream_specs(stream, tabs, layer, tm)
    in_specs += [_layer_spec(w_in, e), _layer_spec(conv_w, e)]
    args += [w_in, conv_w]
    if rope:
        in_specs += [pl.BlockSpec((tm, LANES), lambda i: (i % per_seg, 0))] * 3
        args += list(rope_tables)
    cast_in, cast_out, cast_shape = _cast_rider(casts, rows // tm)
    return pl.pallas_call(
        functools.partial(_in_even_kernel, rope=rope, seg=stream.seg, n_cast=len(casts)),
        grid=(rows // tm,),
        in_specs=in_specs + cast_in,
        out_specs=[pl.BlockSpec((tm, Q_DIM), row), pl.BlockSpec((KV_TILES, tm), lambda i: (0, i)),
                   pl.BlockSpec((tm, KV_TILES), row), pl.BlockSpec((tm, SCONV_DIM), row)] + cast_out,
        out_shape=[jax.ShapeDtypeStruct((rows, Q_DIM), BF16), jax.ShapeDtypeStruct((KV_TILES, rows), BF16),
                   jax.ShapeDtypeStruct((rows, KV_TILES), BF16), jax.ShapeDtypeStruct((rows, SCONV_DIM), BF16)]
        + cast_shape,
        scratch_shapes=[pltpu.VMEM(w_in.shape[1:], BF16)],
        compiler_params=_params("arbitrary"),
        name="in_even",
    )(*args, *[a for a, _ in casts])


def _in_kv_kernel(x_ref, gn_ref, sc_ref, sh_ref, w_ref, k_ref, v_ref):
    h = _modulated(x_ref[...], gn_ref, sc_ref, sh_ref)
    kv = _dot(h, w_ref[0].astype(BF16))
    _write_kv(kv[:, 0:KV_DIM], kv[:, KV_DIM:2 * KV_DIM], k_ref, v_ref)


def _in_kv(stream, tabs, layer, e, w_in):
    rows, d = stream.x.shape
    assert stream.ctx_row is not None and rows % IN_TILE == 0
    tm = IN_TILE
    row = lambda i: (i, 0)
    mrow = stream.mod_row(tm)
    return pl.pallas_call(
        _in_kv_kernel,
        grid=(rows // tm,),
        in_specs=[pl.BlockSpec((tm, d), row), _gain_spec(layer, 0), _mod_spec(layer, 1, mrow),
                  _mod_spec(layer, 0, mrow),
                  pl.BlockSpec((1, d, 2 * KV_DIM), lambda i: (e, 0, Q_DIM // (2 * KV_DIM)))],
        out_specs=[pl.BlockSpec((KV_TILES, tm), lambda i: (0, i)), pl.BlockSpec((tm, KV_TILES), row)],
        out_shape=[jax.ShapeDtypeStruct((KV_TILES, rows), BF16), jax.ShapeDtypeStruct((rows, KV_TILES), BF16)],
        compiler_params=_params("arbitrary"),
        name="in_kv",
    )(stream.x, tabs.gains, tabs.mods, tabs.mods, w_in)


def _in_odd_kernel(xp_ref, x_ref, xn_ref, gn_ref, sc_ref, sh_ref, w_ref, ccs_ref, wg_ref, ps_ref,
                   p_ref, ab_ref, w_scr, *, seg):
    _cast_once(w_ref, w_scr)
    tm = x_ref.shape[0]
    hh = _modulated(_ext(xp_ref, x_ref, xn_ref), gn_ref, sc_ref, sh_ref)
    h = hh[HALO:HALO + tm]
    up = _seg_ext(_dot(hh, w_scr[:, 0:POOL_DIM]), seg)
    uf = _dot(h, w_scr[:, POOL_DIM:POOL_DIM + FFT_DIM]).astype(BF16)
    ccs = ccs_ref[...]
    both = [_dot(uf[:, g * FFT_GROUP:(g + 1) * FFT_GROUP], ccs) for g in range(N_FFT_GROUPS)]
    ab_ref[...] = jnp.concatenate([r[:, 0:FFT_GROUP] for r in both] + [r[:, FFT_GROUP:] for r in both],
                                  axis=1).astype(BF16)
    pos = _tile_pos(tm, seg)
    pooled = []
    for gi, w in enumerate(POOL_WINDOWS):
        ug = up[:, gi * POOL_GROUP:(gi + 1) * POOL_GROUP]
        back, fwd = w // 2, w - w // 2
        run, span = ug, 1
        while span < w:
            run = run + _roll_rows(run, span)
            span *= 2
        total = _roll_rows(run, -back)[HALO:HALO + tm]
        count = (jnp.minimum(pos + fwd, seg) - jnp.maximum(pos - back, 0)).astype(F32)
        pooled.append((total / count - ug[HALO:HALO + tm]).astype(BF16))
    outs = [_dot(jnp.concatenate(pooled[2 * j:2 * j + 2], axis=1), wg_ref[0, j]) for j in range(N_POOL_GROUPS // 2)]
    p_ref[...] = (jnp.concatenate(outs, axis=1) * ps_ref[0]).astype(BF16)


def _in_odd(stream, tabs, layer, o, w_in, ccs, wg, ps):
    rows = stream.x.shape[0]
    tm = stream.tile()
    row = lambda i: (i, 0)
    in_specs, args = _stream_specs(stream, tabs, layer, tm)
    return pl.pallas_call(
        functools.partial(_in_odd_kernel, seg=stream.seg),
        grid=(rows // tm,),
        in_specs=in_specs + [_layer_spec(w_in, o), _const_spec(ccs), _layer_spec(wg, o), _layer_spec(ps, o)],
        out_specs=[pl.BlockSpec((tm, POOL_DIM), row), pl.BlockSpec((tm, 2 * FFT_DIM), row)],
        out_shape=[jax.ShapeDtypeStruct((rows, POOL_DIM), BF16), jax.ShapeDtypeStruct((rows, 2 * FFT_DIM), BF16)],
        scratch_shapes=[pltpu.VMEM(w_in.shape[1:], BF16)],
        compiler_params=_params("arbitrary"),
        name="in_odd",
    )(*args, w_in, ccs, wg, ps)


def _attn_kernel(sink_ref, q_ref, *rest, window, qb, nblocks, e):
    if window:
        ktp, ktc, ktn, vp, vc, vn, kxt, vx, o_ref = rest
    else:
        kxt, vx, o_ref = rest
    n = pl.program_id(1)
    rows = 2 * BLOCK
    first_pair = lax.broadcasted_iota(jnp.int32, (rows, 1), 0) < BLOCK
    lo = lax.broadcasted_iota(jnp.int32, (1, LANES), 1) < HEAD_DIM
    if window:
        qi = lax.broadcasted_iota(jnp.int32, (rows, BLOCK), 0) % BLOCK
        j = lax.broadcasted_iota(jnp.int32, (rows, BLOCK), 1)
        tri_prev = jnp.where(j >= qi, 0.0, MASK_VALUE)
        tri_next = jnp.where(j <= qi, 0.0, MASK_VALUE)
        bias = []
        for t in range(qb):
            blk = n * qb + t
            bias.append((tri_prev + jnp.where(blk >= 1, 0.0, MASK_VALUE),
                         tri_next + jnp.where(blk <= nblocks - 2, 0.0, MASK_VALUE)))

    def lane_halves(h):
        c0 = 2 * LANES * h
        return slice(c0, c0 + LANES), slice(c0 + LANES, c0 + 2 * LANES)

    def keys_values(h, t):
        kparts, vparts = [], []
        for sl in lane_halves(h):
            if window:
                kall = jnp.concatenate([ktp[sl, :], ktc[sl, :], ktn[sl, :]], axis=1)
                vall = jnp.concatenate([vp[0, :, sl], vc[0, :, sl], vn[0, :, sl]], axis=0)
                kparts.append(kall[:, t * BLOCK:(t + 3) * BLOCK])
                vparts.append(vall[t * BLOCK:(t + 3) * BLOCK])
            kparts.append(kxt[sl, :])
            vparts.append(vx[0, :, sl])
        return jnp.concatenate(kparts, axis=1), jnp.concatenate(vparts, axis=0)

    def logits(h, t):
        rs = slice(t * BLOCK, (t + 1) * BLOCK)
        lo_sl, hi_sl = lane_halves(h)
        qp = jnp.concatenate([q_ref[0, rs, lo_sl], q_ref[0, rs, hi_sl]], axis=0)
        return _dot(qp, keys_values(h, t)[0])

    def finish(h, t, s):
        vst = keys_values(h, t)[1]
        nkeys = vst.shape[0] // 2
        probs, inv = [], []
        for half in range(2):
            tiles = [s[:, half * nkeys + c * LANES:half * nkeys + (c + 1) * LANES] for c in range(nkeys // LANES)]
            if window:
                tiles[0] = tiles[0] + bias[t][0]
                tiles[2] = tiles[2] + bias[t][1]
            head = 4 * h + half
            sink_col = jnp.where(first_pair, sink_ref[e, head], sink_ref[e, head + 2]) * LOG2E
            m = jnp.maximum(jnp.max(functools.reduce(jnp.maximum, tiles), axis=-1, keepdims=True), sink_col)
            es = [jnp.exp2(tl - m) for tl in tiles]
            den = jnp.sum(functools.reduce(jnp.add, es), axis=-1, keepdims=True) + jnp.exp2(sink_col - m)
            probs += [e_.astype(BF16) for e_ in es]
            inv.append(1.0 / den)
        o = _dot(jnp.concatenate(probs, axis=1), vst) * jnp.where(lo, inv[0], inv[1])
        rs = slice(t * BLOCK, (t + 1) * BLOCK)
        lo_sl, hi_sl = lane_halves(h)
        o_ref[0, rs, lo_sl] = o[:BLOCK].astype(o_ref.dtype)
        o_ref[0, rs, hi_sl] = o[BLOCK:].astype(o_ref.dtype)

    units = [(h, t) for h in range(N_KV_HEADS) for t in range(qb)]
    s_next = logits(*units[0])
    for u, unit in enumerate(units):
        s_cur = s_next
        if u + 1 < len(units):
            s_next = logits(*units[u + 1])
        finish(*unit, s_cur)


def _attention(sinks, e, q, kt, v, kxt, vx, window):
    b, t, _ = q.shape
    l = vx.shape[1]
    nb = t // BLOCK
    qb = ATTN_QBLOCKS if window else nb
    w = vx.shape[2]
    steps = nb // qb
    cur = lambda i, n: (i, n, 0)
    in_specs = [pl.BlockSpec(memory_space=pltpu.SMEM), pl.BlockSpec((1, qb * BLOCK, Q_DIM), cur)]
    args = [sinks, q]
    if window:
        pb = lambda n: jnp.maximum(n * qb - 1, 0)
        nx = lambda n: jnp.minimum(n * qb + qb, nb - 1)
        in_specs += [
            pl.BlockSpec((w, BLOCK), lambda i, n: (0, i * nb + pb(n))),
            pl.BlockSpec((w, qb * BLOCK), lambda i, n: (0, i * steps + n)),
            pl.BlockSpec((w, BLOCK), lambda i, n: (0, i * nb + nx(n))),
            pl.BlockSpec((1, BLOCK, w), lambda i, n: (i, pb(n), 0)),
            pl.BlockSpec((1, qb * BLOCK, w), cur),
            pl.BlockSpec((1, BLOCK, w), lambda i, n: (i, nx(n), 0)),
        ]
        args += [kt, kt, kt, v, v, v]
    in_specs += [pl.BlockSpec((w, l), lambda i, n: (0, i)), pl.BlockSpec((1, l, w), lambda i, n: (i, 0, 0))]
    args += [kxt, vx]
    return pl.pallas_call(
        functools.partial(_attn_kernel, window=window, qb=qb, nblocks=nb, e=e),
        grid=(b, steps),
        in_specs=in_specs,
        out_specs=pl.BlockSpec((1, qb * BLOCK, Q_DIM), cur),
        out_shape=jax.ShapeDtypeStruct((b, t, Q_DIM), BF16),
        compiler_params=_params("arbitrary", "arbitrary"),
        name="attention",
    )(*args)


def _seq_dft_kernel(ct_ref, st_ref, flip_ref, ab_ref, *rest, norm, n_cast):
    o_ref = rest[n_cast]
    _run_casts(rest[:n_cast], rest[n_cast + 1:])
    t = ab_ref.shape[1]
    half = t // 2
    a = ab_ref[0, :, 0:FFT_DIM]
    b = ab_ref[0, :, FFT_DIM:2 * FFT_DIM]
    p = _dot(ct_ref[...], a)
    q = _dot(st_ref[...], b)
    o_ref[0, 0:half] = ((p - q) * norm).astype(o_ref.dtype)
    sign = 1.0 - 2.0 * (lax.broadcasted_iota(jnp.int32, (t, 1), 0) % 2).astype(F32)
    nyquist = jnp.sum(a.astype(F32) * sign, axis=0, keepdims=True)
    first = lax.broadcasted_iota(jnp.int32, (half, 1), 0) == 0
    mirrored = (jnp.where(first, nyquist, p + q) * norm).astype(BF16)
    o_ref[0, half:t] = _dot(flip_ref[...], mirrored).astype(o_ref.dtype)


def _seq_dft(ab3, tables, casts=()):
    b, t, _ = ab3.shape
    cast_in, cast_out, cast_shape = _cast_rider(casts, b)
    return pl.pallas_call(
        functools.partial(_seq_dft_kernel, norm=float((t * FFT_GROUP) ** -0.5), n_cast=len(casts)),
        grid=(b,),
        in_specs=([_const_spec(a) for a in tables] + [pl.BlockSpec((1, t, 2 * FFT_DIM), lambda i: (i, 0, 0))]
                  + cast_in),
        out_specs=[pl.BlockSpec((1, t, FFT_DIM), lambda i: (i, 0, 0))] + cast_out,
        out_shape=[jax.ShapeDtypeStruct((b, t, FFT_DIM), BF16)] + cast_shape,
        compiler_params=_params("arbitrary"),
        name="seq_dft",
    )(*tables, ab3, *[a for a, _ in casts])


def _post_kernel(ap_ref, a_ref, an_ref, bp_ref, b_ref, bn_ref, xp_ref, x_ref, xn_ref,
                 wo_ref, gn1_ref, g1_ref, gn2_ref, sc_ref, sh_ref, g2_ref, gn3_ref,
                 wu_hbm, cw_ref, wd_hbm, o_ref, y_scr, hh_scr, xm_scr, act_scr, wu_ref, wd_ref, w_sem, *, seg):
    i = pl.program_id(0)
    tm = x_ref.shape[0]
    main = slice(HALO, HALO + tm)
    rd = i % 2
    wr = 1 - rd
    piece_rows = tm // POST_PIECES
    pieces = ([(0, xp_ref, 0, HALO)]
              + [(HALO + k * piece_rows, x_ref, k * piece_rows, piece_rows) for k in range(POST_PIECES)]
              + [(HALO + tm, xn_ref, 0, HALO)])
    ka = a_ref.shape[1]

    def prepare(piece):
        lo, ref, off, nrows = piece
        xm = ref[off:off + nrows] + g1_ref[0] * _rms(y_scr[lo:lo + nrows], gn1_ref[0])
        hn = _rms(xm, gn2_ref[0]) * (1.0 + sc_ref[0]) + sh_ref[0]
        hh_scr[wr, lo:lo + nrows] = hn.astype(BF16)
        if ref is x_ref:
            xm_scr[wr, off:off + nrows] = xm
        return jnp.concatenate([_zero_anchor(hn)] * (FFN_CHUNK // LANES), axis=1)

    def out_projection():
        y_scr[...] = (_dot(_ext(ap_ref, a_ref, an_ref), wo_ref[0, 0:ka, :])
                      + _dot(_ext(bp_ref, b_ref, bn_ref), wo_ref[0, ka:, :]))

    weight_copies = (pltpu.make_async_copy(wu_hbm, wu_ref, w_sem.at[0]),
                     pltpu.make_async_copy(wd_hbm, wd_ref, w_sem.at[1]))

    @pl.when(i == 0)
    def _():
        for copy in weight_copies:
            copy.start()
        y_scr[...] = jnp.zeros_like(y_scr)

    @pl.when(i == 2)
    def _():
        for copy in weight_copies:
            copy.wait()

    @pl.when(i < 2)
    def _():
        for piece in pieces:
            prepare(piece)
        out_projection()

    @pl.when(i >= 2)
    def _():
        hh = hh_scr[rd]
        h = hh[main]
        cw = cw_ref[0]
        anchor = None
        for c in range(D_FF // FFN_CHUNK):
            cs = slice(c * FFN_CHUNK, (c + 1) * FFN_CHUNK)
            cwc = cw[:, cs] if anchor is None else cw[:, cs] + anchor
            conv = _conv3(_dot(hh, wu_ref[0, :, cs]), seg, cwc, tile=i - 2)
            val = _dot(h, wu_ref[0, :, D_FF + c * FFN_CHUNK:D_FF + (c + 1) * FFN_CHUNK])
            act_scr[:, cs] = (conv / (1.0 + jnp.exp(-conv)) * val).astype(BF16)
            anchor = prepare(pieces[c]) if c < len(pieces) else None
        y2 = _dot(act_scr[...], wd_ref[0])
        out_projection()
        o_ref[...] = xm_scr[rd] + g2_ref[0] * _rms(y2, gn3_ref[0])


def _post(a2, b2, stream, tabs, layer, w_out, w_up, conv_w, w_down):
    x2 = stream.x
    rows, d = x2.shape
    tm = ROW_TILE
    seg = stream.seg
    assert seg % tm == 0 or tm % seg == 0
    n = rows // tm
    stage_a = lambda i: jnp.minimum(i, n - 1)
    stage_b = lambda i: jnp.clip(i - 1, 0, n - 1)
    stage_c = lambda i: jnp.maximum(i - 2, 0)
    row = stream.mod_row(min(tm, seg))
    mod = lambda chunk, stage: _mod_spec(layer, chunk, row, stage)
    return pl.pallas_call(
        functools.partial(_post_kernel, seg=seg),
        grid=(n + 2,),
        in_specs=(_halo_specs(tm, rows, a2.shape[1], stage_a) + _halo_specs(tm, rows, b2.shape[1], stage_a)
                  + _halo_specs(tm, rows, d, stage_b)
                  + [_layer_spec(w_out, 0), _gain_spec(layer, 1), mod(2, stage_b), _gain_spec(layer, 2),
                     mod(4, stage_b), mod(3, stage_b), mod(5, stage_c), _gain_spec(layer, 3),
                     pl.BlockSpec(memory_space=pl.ANY), _layer_spec(conv_w, layer),
                     pl.BlockSpec(memory_space=pl.ANY)]),
        out_specs=pl.BlockSpec((tm, d), lambda i: (stage_c(i), 0)),
        out_shape=jax.ShapeDtypeStruct((rows, d), F32),
        scratch_shapes=[pltpu.VMEM((tm + 2 * HALO, d), F32), pltpu.VMEM((2, tm + 2 * HALO, d), BF16),
                        pltpu.VMEM((2, tm, d), F32), pltpu.VMEM((tm, D_FF), BF16),
                        pltpu.VMEM(w_up.shape, BF16), pltpu.VMEM(w_down.shape, BF16),
                        pltpu.SemaphoreType.DMA((2,))],
        compiler_params=_params("arbitrary"),
        name="post",
    )(a2, a2, a2, b2, b2, b2, x2, x2, x2, w_out, tabs.gains, tabs.mods, tabs.gains, tabs.mods, tabs.mods,
      tabs.mods, tabs.gains, w_up, conv_w, w_down)


def _rope_tables(seq):
    half = HEAD_DIM // 2
    nf = half // 2
    inv = ROPE_BASE ** (-np.arange(nf, dtype=np.float64) / nf)
    t = np.arange(seq)
    lane = np.arange(LANES)
    in_head = lane % HEAD_DIM
    pos = np.where((in_head // half)[None, :] == 0, (t // GRID_W)[:, None], (t % GRID_W)[:, None])
    ang = pos * inv[lane % nf][None, :]
    first = ((lane % half) < nf)[None, :]
    cos = np.cos(ang)
    sin_a = np.where(first, -np.sin(ang), 0.0)
    sin_b = np.where(first, 0.0, np.sin(ang))
    return tuple(jnp.asarray(a, F32) for a in (cos, sin_a, sin_b))


def _dft_tables(n, rows=None):
    k = np.arange(n if rows is None else rows)
    ang = 2.0 * np.pi * ((k[:, None] * np.arange(n)[None, :]) % n) / n
    return jnp.asarray(np.cos(ang), F32).astype(BF16), jnp.asarray(np.sin(ang), F32).astype(BF16)


def _seq_dft_tables(n):
    half = n // 2
    flip = np.zeros((half, half), np.float32)
    flip[np.arange(half), (half - np.arange(half)) % half] = 1.0
    return _dft_tables(n, half) + (jnp.asarray(flip).astype(BF16),)


def kernel(x, c, ctx, c_ctx, mod_w, mod_b, norm_g, att_w_in, att_sink, sconv_w, att_w_out, mix_w_in,
           pool_w_grp, pool_scale, mix_w_out, ffn_w_up, ffn_conv, ffn_w_down):
    bsz, seq, d = x.shape
    ctx_len = ctx.shape[1]
    assert d == D_MODEL and seq % ROW_TILE == 0 and (bsz * ctx_len) % ROW_TILE == 0 and ROW_TILE % ctx_len == 0
    assert bsz < MOD_ROWS and norm_g.shape[1] == N_GAINS

    cc = jnp.concatenate([c, c_ctx[None, :], jnp.zeros((MOD_ROWS - bsz - 1, d), F32)], axis=0)
    mods = _modulation(cc, mod_w, mod_b)
    tabs = _Tables(mods, norm_g.reshape(DEPTH * N_GAINS, 1, d))

    rope = _rope_tables(seq)
    chan_cs = jnp.concatenate(_dft_tables(FFT_GROUP), axis=1)
    seq_tabs = {seq: _seq_dft_tables(seq), ctx_len: _seq_dft_tables(ctx_len)}

    g4 = pool_w_grp.astype(BF16)
    zero = jnp.zeros_like(g4[:, 0])
    w_grp = jnp.stack([jnp.concatenate([jnp.concatenate([g4[:, 2 * j], zero], axis=2),
                                        jnp.concatenate([zero, g4[:, 2 * j + 1]], axis=2)], axis=1)
                       for j in range(N_POOL_GROUPS // 2)], axis=1)
    p_scale = pool_scale.reshape(-1, 1, POOL_DIM)

    xs = x.reshape(bsz * seq, d)
    hc = ctx.reshape(bsz * ctx_len, d)

    for i in range(DEPTH):
        even = i % 2 == 0
        ctx_live = any(j % 2 == 0 for j in range(i + 1, DEPTH))
        lat = _Stream(xs, seq, None)
        con = _Stream(hc, ctx_len, bsz)
        if even:
            e = i // 2
            casts = ((att_w_out, e), (ffn_w_up, i), (ffn_w_down, i))
            q, kt, v, sx, w_out, w_up, w_down = _in_even(lat, tabs, i, e, att_w_in, sconv_w, rope, casts)
            post = (w_out, w_up, ffn_conv, w_down)
            if ctx_live:
                qc, kct, vc, sxc = _in_even(con, tabs, i, e, att_w_in, sconv_w, None)
            else:
                kct, vc = _in_kv(con, tabs, i, e, att_w_in)
            vc3 = vc.reshape(bsz, ctx_len, -1)
            ax = _attention(att_sink, e, q.reshape(bsz, seq, -1), kt, v.reshape(bsz, seq, -1), kct, vc3, True)
            xs = _post(ax.reshape(bsz * seq, -1), sx, lat, tabs, i, *post)
            if ctx_live:
                ac = _attention(att_sink, e, qc.reshape(bsz, ctx_len, -1), None, None, kct, vc3, False)
                hc = _post(ac.reshape(bsz * ctx_len, -1), sxc, con, tabs, i, *post)
        else:
            o = i // 2
            casts = ((mix_w_out, o), (ffn_w_up, i), (ffn_w_down, i))
            yp, ab = _in_odd(lat, tabs, i, o, mix_w_in, chan_cs, w_grp, p_scale)
            yf, w_out, w_up, w_down = _seq_dft(ab.reshape(bsz, seq, -1), seq_tabs[seq], casts)
            post = (w_out, w_up, ffn_conv, w_down)
            if ctx_live:
                ypc, abc = _in_odd(con, tabs, i, o, mix_w_in, chan_cs, w_grp, p_scale)
                yfc, = _seq_dft(abc.reshape(bsz, ctx_len, -1), seq_tabs[ctx_len])
            xs = _post(yp, yf.reshape(bsz * seq, -1), lat, tabs, i, *post)
            if ctx_live:
                hc = _post(ypc, yfc.reshape(bsz * ctx_len, -1), con, tabs, i, *post)
    return xs.reshape(bsz, seq, d)
```

```python
import functools
from typing import NamedTuple

import numpy as np
import jax
import jax.numpy as jnp
from jax import lax
from jax.experimental import pallas as pl
from jax.experimental.pallas import tpu as pltpu

F32 = jnp.float32
BF16 = jnp.bfloat16

D_MODEL = 1024
DEPTH = 4
GRID_W = 64
EPS = 1e-6
N_Q_HEADS = 8
N_KV_HEADS = 2
HEAD_DIM = 64
WINDOW = 128
BLOCK = 128
ROPE_BASE = 10000.0
SCONV_DIM = 512
POOL_DIM = 512
N_POOL_GROUPS = 4
POOL_GROUP = POOL_DIM // N_POOL_GROUPS
POOL_WINDOWS = (2, 4, 8, 16)
FFT_DIM = 512
N_FFT_GROUPS = 4
FFT_GROUP = FFT_DIM // N_FFT_GROUPS
D_FF = 2816
Q_DIM = N_Q_HEADS * HEAD_DIM
KV_DIM = N_KV_HEADS * HEAD_DIM
KV_TILES = 2 * KV_DIM * 2

LANES = 128
MASK_VALUE = -1e30
LOG2E = 1.4426950408889634
ATTN_QBLOCKS = 16
VMEM_LIMIT = 56 * 1024 * 1024
ROW_TILE = 512
IN_TILE = 1024
HALO = 16
FFN_CHUNK = 256
POST_PIECES = 8
MOD_ROWS = 16
N_GAINS = 4

assert HALO >= max(POOL_WINDOWS) and all(w & (w - 1) == 0 for w in POOL_WINDOWS)


def _params(*sem):
    return pltpu.CompilerParams(dimension_semantics=sem, vmem_limit_bytes=VMEM_LIMIT)


def _rms(x, g):
    ms = jnp.mean(x * x, axis=-1, keepdims=True)
    return x * lax.rsqrt(ms + EPS) * g


def _dot(a, b):
    return jnp.dot(a, b, preferred_element_type=F32)


class _Stream(NamedTuple):
    x: jax.Array
    seg: int
    ctx_row: int | None

    def tile(self):
        return min(IN_TILE, self.seg)

    def mod_row(self, tm):
        if self.ctx_row is not None:
            row = self.ctx_row
            return lambda tile: row
        per_seg = self.seg // tm
        return lambda tile: tile // per_seg


class _Tables(NamedTuple):
    mods: jax.Array
    gains: jax.Array


def _layer_spec(a, layer):
    nd = a.ndim
    return pl.BlockSpec((1,) + a.shape[1:], lambda *_: (layer,) + (0,) * (nd - 1), pipeline_mode=pl.Buffered(1))


def _const_spec(a):
    return pl.BlockSpec(a.shape, lambda *_: (0,) * a.ndim, pipeline_mode=pl.Buffered(1))


def _gain_spec(layer, j):
    return pl.BlockSpec((1, 1, D_MODEL), lambda *_: (layer * N_GAINS + j, 0, 0))


def _mod_spec(layer, chunk, row, tile=lambda i: i):
    return pl.BlockSpec((1, 1, D_MODEL), lambda i: (layer * MOD_ROWS + row(tile(i)), 0, chunk))


def _halo_specs(tm, rows, width, tile=lambda i: i):
    hb = tm // HALO
    return [
        pl.BlockSpec((HALO, width), lambda i: (jnp.maximum(tile(i) * hb - 1, 0), 0)),
        pl.BlockSpec((tm, width), lambda i: (tile(i), 0)),
        pl.BlockSpec((HALO, width), lambda i: (jnp.minimum((tile(i) + 1) * hb, rows // HALO - 1), 0)),
    ]


def _ext(prev_ref, main_ref, next_ref):
    return jnp.concatenate([prev_ref[...], main_ref[...], next_ref[...]], axis=0)


def _tile_pos(tm, seg):
    return (pl.program_id(0) * tm + lax.broadcasted_iota(jnp.int32, (tm, 1), 0)) % seg


def _seg_ext(x_ext, seg, tile=None):
    tm = x_ext.shape[0] - 2 * HALO
    i = pl.program_id(0) if tile is None else tile
    top_ok = (i * tm) % seg != 0
    bot_ok = ((i + 1) * tm) % seg != 0
    top, bot = x_ext[:HALO], x_ext[HALO + tm:]
    return jnp.concatenate([jnp.where(top_ok, top, jnp.zeros_like(top)), x_ext[HALO:HALO + tm],
                            jnp.where(bot_ok, bot, jnp.zeros_like(bot))], axis=0)


def _roll_rows(x_ext, off):
    rows = x_ext.shape[0]
    return pltpu.roll(x_ext, (-off) % rows, 0)


def _conv3(x_ext, seg, w, tile=None):
    tm = x_ext.shape[0] - 2 * HALO
    main = slice(HALO, HALO + tm)
    xm = _seg_ext(x_ext, seg, tile)
    prev, nxt = _roll_rows(xm, -1)[main], _roll_rows(xm, 1)[main]
    if seg < tm:
        pos = lax.broadcasted_iota(jnp.int32, (tm, 1), 0) % seg
        prev = jnp.where(pos == 0, 0.0, prev)
        nxt = jnp.where(pos == seg - 1, 0.0, nxt)
    return prev * w[0:1] + x_ext[main] * w[1:2] + nxt * w[2:3]


def _zero_anchor(v):
    fold = sum(v[r:r + 8] for r in range(0, v.shape[0], 8))
    fold = sum(fold[:, l:l + LANES] for l in range(0, fold.shape[1], LANES))
    return jnp.minimum(jnp.abs(fold[0:1]), 0.0)


def _modulated(x, gn_ref, sc_ref, sh_ref):
    ms = jnp.mean(x * x, axis=-1, keepdims=True)
    return (x * lax.rsqrt(ms + EPS) * (gn_ref[0] * (1.0 + sc_ref[0])) + sh_ref[0]).astype(BF16)


def _stream_specs(stream, tabs, layer):
    tm = stream.tile()
    rows, d = stream.x.shape
    row = stream.mod_row(tm)
    specs = _halo_specs(tm, rows, d) + [_gain_spec(layer, 0), _mod_spec(layer, 1, row), _mod_spec(layer, 0, row)]
    return specs, [stream.x] * 3 + [tabs.gains, tabs.mods, tabs.mods]


def _cast_rider(params, n_steps):
    in_specs, out_specs, out_shape = [], [], []
    for a, layer in params:
        r, c = a.shape[1:]
        chunk = r // n_steps
        assert chunk * n_steps == r and chunk % 16 == 0
        in_specs.append(pl.BlockSpec((1, chunk, c), lambda i, layer=layer: (layer, i, 0)))
        out_specs.append(pl.BlockSpec((1, chunk, c), lambda i: (0, i, 0)))
        out_shape.append(jax.ShapeDtypeStruct((1, r, c), BF16))
    return in_specs, out_specs, out_shape


def _run_casts(in_refs, out_refs):
    for src, dst in zip(in_refs, out_refs):
        dst[...] = src[...].astype(BF16)


def _cast_once(w_ref, w_scr):
    @pl.when(pl.program_id(0) == 0)
    def _():
        w_scr[...] = w_ref[0].astype(BF16)


def _mod_kernel(c_ref, w_ref, b_ref, o_ref):
    c = c_ref[...]
    s = (c / (1.0 + jnp.exp(-c))).astype(BF16)
    res = _dot(s, w_ref[0].astype(BF16)) + b_ref[0]
    for r in range(MOD_ROWS):
        o_ref[r] = res[r:r + 1]


def _modulation(cc, mod_w, mod_b):
    depth, d, n = mod_w.shape
    tn = 2048
    return pl.pallas_call(
        _mod_kernel,
        grid=(depth, n // tn),
        in_specs=[
            pl.BlockSpec((MOD_ROWS, d), lambda i, j: (0, 0)),
            pl.BlockSpec((1, d, tn), lambda i, j: (i, 0, j)),
            pl.BlockSpec((1, 1, tn), lambda i, j: (i, 0, j)),
        ],
        out_specs=pl.BlockSpec((MOD_ROWS, 1, tn), lambda i, j: (i, 0, j)),
        out_shape=jax.ShapeDtypeStruct((depth * MOD_ROWS, 1, n), F32),
        compiler_params=_params("arbitrary", "arbitrary"),
        name="modulation",
    )(cc, mod_w, mod_b.reshape(depth, 1, n))


def _rope128(x, cos, sin_a, sin_b):
    return x * cos + pltpu.roll(x, LANES - 16, 1) * sin_a + pltpu.roll(x, 16, 1) * sin_b


def _write_kv(k, v, k_ref, v_ref):
    lo = lax.broadcasted_iota(jnp.int32, (1, LANES), 1) < HEAD_DIM
    vr = pltpu.roll(v, HEAD_DIM, 1)
    zero = jnp.zeros_like(v)
    v_ref[...] = jnp.concatenate(
        [jnp.where(lo, v, zero), jnp.where(lo, zero, vr), jnp.where(lo, vr, zero), jnp.where(lo, zero, v)],
        axis=1).astype(BF16)
    kt = k.T
    top = lax.broadcasted_iota(jnp.int32, (LANES, 1), 0) < HEAD_DIM
    ktr = pltpu.roll(kt, HEAD_DIM, 0)
    zero = jnp.zeros_like(kt)
    k_ref[...] = jnp.concatenate(
        [jnp.where(top, kt, zero), jnp.where(top, zero, ktr), jnp.where(top, ktr, zero), jnp.where(top, zero, kt)],
        axis=0).astype(BF16)


def _in_even_kernel(xp_ref, x_ref, xn_ref, gn_ref, sc_ref, sh_ref, w_ref, cw_ref, *rest, rope, seg, n_cast):
    if rope:
        cos_ref, sa_ref, sb_ref = rest[:3]
        rest = rest[3:]
    cast_in, rest = rest[:n_cast], rest[n_cast:]
    q_ref, k_ref, v_ref, s_ref = rest[:4]
    _run_casts(cast_in, rest[4:4 + n_cast])
    w_scr = rest[-1]
    _cast_once(w_ref, w_scr)
    tm = x_ref.shape[0]
    hh = _modulated(_ext(xp_ref, x_ref, xn_ref), gn_ref, sc_ref, sh_ref)
    h = hh[HALO:HALO + tm]
    o = Q_DIM + 2 * KV_DIM
    cz = (_dot(hh, w_scr[:, o + SCONV_DIM:o + 2 * SCONV_DIM])
          * _dot(hh, w_scr[:, o + 2 * SCONV_DIM:o + 3 * SCONV_DIM]))
    q = _dot(h, w_scr[:, 0:Q_DIM])
    kv = _dot(h, w_scr[:, Q_DIM:Q_DIM + 2 * KV_DIM])
    k, v = kv[:, 0:KV_DIM], kv[:, KV_DIM:2 * KV_DIM]
    b = _dot(h, w_scr[:, o:o + SCONV_DIM])
    s_ref[...] = (b * _conv3(cz, seg, cw_ref[0])).astype(BF16)
    if rope:
        cos, sa, sb = cos_ref[...], sa_ref[...], sb_ref[...]
        k = _rope128(k, cos, sa, sb)
        q = jnp.concatenate(
            [_rope128(q[:, LANES * j:LANES * (j + 1)], cos, sa, sb) for j in range(Q_DIM // LANES)], axis=1)
    q_ref[...] = (q * (HEAD_DIM ** -0.5 * LOG2E)).astype(BF16)
    _write_kv(k, v, k_ref, v_ref)


def _in_even(stream, tabs, layer, e, w_in, conv_w, rope_tables, casts=()):
    rows = stream.x.shape[0]
    tm = stream.tile()
    per_seg = stream.seg // tm
    rope = rope_tables is not None
    row = lambda i: (i, 0)
    in_specs, args = _stream_specs(stream, tabs, layer)
    in_specs += [_layer_spec(w_in, e), _layer_spec(conv_w, e)]
    args += [w_in, conv_w]
    if rope:
        in_specs += [pl.BlockSpec((tm, LANES), lambda i: (i % per_seg, 0))] * 3
        args += list(rope_tables)
    cast_in, cast_out, cast_shape = _cast_rider(casts, rows // tm)
    return pl.pallas_call(
        functools.partial(_in_even_kernel, rope=rope, seg=stream.seg, n_cast=len(casts)),
        grid=(rows // tm,),
        in_specs=in_specs + cast_in,
        out_specs=[pl.BlockSpec((tm, Q_DIM), row), pl.BlockSpec((KV_TILES, tm), lambda i: (0, i)),
                   pl.BlockSpec((tm, KV_TILES), row), pl.BlockSpec((tm, SCONV_DIM), row)] + cast_out,
        out_shape=[jax.ShapeDtypeStruct((rows, Q_DIM), BF16), jax.ShapeDtypeStruct((KV_TILES, rows), BF16),
                   jax.ShapeDtypeStruct((rows, KV_TILES), BF16), jax.ShapeDtypeStruct((rows, SCONV_DIM), BF16)]
        + cast_shape,
        scratch_shapes=[pltpu.VMEM(w_in.shape[1:], BF16)],
        compiler_params=_params("arbitrary"),
        name="in_even",
    )(*args, *[a for a, _ in casts])


def _in_kv_kernel(x_ref, gn_ref, sc_ref, sh_ref, w_ref, k_ref, v_ref):
    h = _modulated(x_ref[...], gn_ref, sc_ref, sh_ref)
    kv = _dot(h, w_ref[0].astype(BF16))
    _write_kv(kv[:, 0:KV_DIM], kv[:, KV_DIM:2 * KV_DIM], k_ref, v_ref)


def _in_kv(stream, tabs, layer, e, w_in):
    rows, d = stream.x.shape
    assert stream.ctx_row is not None and rows % IN_TILE == 0
    tm = IN_TILE
    row = lambda i: (i, 0)
    mrow = stream.mod_row(tm)
    return pl.pallas_call(
        _in_kv_kernel,
        grid=(rows // tm,),
        in_specs=[pl.BlockSpec((tm, d), row), _gain_spec(layer, 0), _mod_spec(layer, 1, mrow),
                  _mod_spec(layer, 0, mrow),
                  pl.BlockSpec((1, d, 2 * KV_DIM), lambda i: (e, 0, Q_DIM // (2 * KV_DIM)))],
        out_specs=[pl.BlockSpec((KV_TILES, tm), lambda i: (0, i)), pl.BlockSpec((tm, KV_TILES), row)],
        out_shape=[jax.ShapeDtypeStruct((KV_TILES, rows), BF16), jax.ShapeDtypeStruct((rows, KV_TILES), BF16)],
        compiler_params=_params("arbitrary"),
        name="in_kv",
    )(stream.x, tabs.gains, tabs.mods, tabs.mods, w_in)


def _in_odd_kernel(xp_ref, x_ref, xn_ref, gn_ref, sc_ref, sh_ref, w_ref, ccs_ref, wg_ref, ps_ref,
                   p_ref, ab_ref, w_scr, *, seg):
    _cast_once(w_ref, w_scr)
    tm = x_ref.shape[0]
    hh = _modulated(_ext(xp_ref, x_ref, xn_ref), gn_ref, sc_ref, sh_ref)
    h = hh[HALO:HALO + tm]
    up = _seg_ext(_dot(hh, w_scr[:, 0:POOL_DIM]), seg)
    uf = _dot(h, w_scr[:, POOL_DIM:POOL_DIM + FFT_DIM]).astype(BF16)
    ccs = ccs_ref[...]
    both = [_dot(uf[:, g * FFT_GROUP:(g + 1) * FFT_GROUP], ccs) for g in range(N_FFT_GROUPS)]
    ab_ref[...] = jnp.concatenate([r[:, 0:FFT_GROUP] for r in both] + [r[:, FFT_GROUP:] for r in both],
                                  axis=1).astype(BF16)
    pos = _tile_pos(tm, seg)
    pooled = []
    for gi, w in enumerate(POOL_WINDOWS):
        ug = up[:, gi * POOL_GROUP:(gi + 1) * POOL_GROUP]
        back, fwd = w // 2, w - w // 2
        run, span = ug, 1
        while span < w:
            run = run + _roll_rows(run, span)
            span *= 2
        total = _roll_rows(run, -back)[HALO:HALO + tm]
        count = (jnp.minimum(pos + fwd, seg) - jnp.maximum(pos - back, 0)).astype(F32)
        pooled.append((total / count - ug[HALO:HALO + tm]).astype(BF16))
    outs = [_dot(jnp.concatenate(pooled[2 * j:2 * j + 2], axis=1), wg_ref[0, j]) for j in range(N_POOL_GROUPS // 2)]
    p_ref[...] = (jnp.concatenate(outs, axis=1) * ps_ref[0]).astype(BF16)


def _in_odd(stream, tabs, layer, o, w_in, ccs, wg, ps):
    rows = stream.x.shape[0]
    tm = stream.tile()
    row = lambda i: (i, 0)
    in_specs, args = _stream_specs(stream, tabs, layer)
    return pl.pallas_call(
        functools.partial(_in_odd_kernel, seg=stream.seg),
        grid=(rows // tm,),
        in_specs=in_specs + [_layer_spec(w_in, o), _const_spec(ccs), _layer_spec(wg, o), _layer_spec(ps, o)],
        out_specs=[pl.BlockSpec((tm, POOL_DIM), row), pl.BlockSpec((tm, 2 * FFT_DIM), row)],
        out_shape=[jax.ShapeDtypeStruct((rows, POOL_DIM), BF16), jax.ShapeDtypeStruct((rows, 2 * FFT_DIM), BF16)],
        scratch_shapes=[pltpu.VMEM(w_in.shape[1:], BF16)],
        compiler_params=_params("arbitrary"),
        name="in_odd",
    )(*args, w_in, ccs, wg, ps)


def _attn_kernel(sink_ref, q_ref, *rest, window, qb, nblocks, e):
    if window:
        ktp, ktc, ktn, vp, vc, vn, kxt, vx, o_ref = rest
    else:
        kxt, vx, o_ref = rest
    n = pl.program_id(1)
    rows = 2 * BLOCK
    first_pair = lax.broadcasted_iota(jnp.int32, (rows, 1), 0) < BLOCK
    lo = lax.broadcasted_iota(jnp.int32, (1, LANES), 1) < HEAD_DIM
    if window:
        qi = lax.broadcasted_iota(jnp.int32, (rows, BLOCK), 0) % BLOCK
        j = lax.broadcasted_iota(jnp.int32, (rows, BLOCK), 1)
        tri_prev = jnp.where(j >= qi, 0.0, MASK_VALUE)
        tri_next = jnp.where(j <= qi, 0.0, MASK_VALUE)
        bias = []
        for t in range(qb):
            blk = n * qb + t
            bias.append((tri_prev + jnp.where(blk >= 1, 0.0, MASK_VALUE),
                         tri_next + jnp.where(blk <= nblocks - 2, 0.0, MASK_VALUE)))

    def lane_halves(h):
        c0 = 2 * LANES * h
        return slice(c0, c0 + LANES), slice(c0 + LANES, c0 + 2 * LANES)

    def keys_values(h, t):
        kparts, vparts = [], []
        for sl in lane_halves(h):
            if window:
                kall = jnp.concatenate([ktp[sl, :], ktc[sl, :], ktn[sl, :]], axis=1)
                vall = jnp.concatenate([vp[0, :, sl], vc[0, :, sl], vn[0, :, sl]], axis=0)
                kparts.append(kall[:, t * BLOCK:(t + 3) * BLOCK])
                vparts.append(vall[t * BLOCK:(t + 3) * BLOCK])
            kparts.append(kxt[sl, :])
            vparts.append(vx[0, :, sl])
        return jnp.concatenate(kparts, axis=1), jnp.concatenate(vparts, axis=0)

    def logits(h, t):
        rs = slice(t * BLOCK, (t + 1) * BLOCK)
        lo_sl, hi_sl = lane_halves(h)
        qp = jnp.concatenate([q_ref[0, rs, lo_sl], q_ref[0, rs, hi_sl]], axis=0)
        return _dot(qp, keys_values(h, t)[0])

    def finish(h, t, s):
        vst = keys_values(h, t)[1]
        nkeys = vst.shape[0] // 2
        probs, inv = [], []
        for half in range(2):
            tiles = [s[:, half * nkeys + c * LANES:half * nkeys + (c + 1) * LANES] for c in range(nkeys // LANES)]
            if window:
                tiles[0] = tiles[0] + bias[t][0]
                tiles[2] = tiles[2] + bias[t][1]
            head = 4 * h + half
            sink_col = jnp.where(first_pair, sink_ref[e, head], sink_ref[e, head + 2]) * LOG2E
            m = jnp.maximum(jnp.max(functools.reduce(jnp.maximum, tiles), axis=-1, keepdims=True), sink_col)
            es = [jnp.exp2(tl - m) for tl in tiles]
            den = jnp.sum(functools.reduce(jnp.add, es), axis=-1, keepdims=True) + jnp.exp2(sink_col - m)
            probs += [e_.astype(BF16) for e_ in es]
            inv.append(1.0 / den)
        o = _dot(jnp.concatenate(probs, axis=1), vst) * jnp.where(lo, inv[0], inv[1])
        rs = slice(t * BLOCK, (t + 1) * BLOCK)
        lo_sl, hi_sl = lane_halves(h)
        o_ref[0, rs, lo_sl] = o[:BLOCK].astype(o_ref.dtype)
        o_ref[0, rs, hi_sl] = o[BLOCK:].astype(o_ref.dtype)

    units = [(h, t) for h in range(N_KV_HEADS) for t in range(qb)]
    s_next = logits(*units[0])
    for u, unit in enumerate(units):
        s_cur = s_next
        if u + 1 < len(units):
            s_next = logits(*units[u + 1])
        finish(*unit, s_cur)


def _attention(sinks, e, q, kt, v, kxt, vx, window):
    b, t, _ = q.shape
    l = vx.shape[1]
    nb = t // BLOCK
    qb = ATTN_QBLOCKS if window else nb
    w = vx.shape[2]
    steps = nb // qb
    cur = lambda i, n: (i, n, 0)
    in_specs = [pl.BlockSpec(memory_space=pltpu.SMEM), pl.BlockSpec((1, qb * BLOCK, Q_DIM), cur)]
    args = [sinks, q]
    if window:
        pb = lambda n: jnp.maximum(n * qb - 1, 0)
        nx = lambda n: jnp.minimum(n * qb + qb, nb - 1)
        in_specs += [
            pl.BlockSpec((w, BLOCK), lambda i, n: (0, i * nb + pb(n))),
            pl.BlockSpec((w, qb * BLOCK), lambda i, n: (0, i * steps + n)),
            pl.BlockSpec((w, BLOCK), lambda i, n: (0, i * nb + nx(n))),
            pl.BlockSpec((1, BLOCK, w), lambda i, n: (i, pb(n), 0)),
            pl.BlockSpec((1, qb * BLOCK, w), cur),
            pl.BlockSpec((1, BLOCK, w), lambda i, n: (i, nx(n), 0)),
        ]
        args += [kt, kt, kt, v, v, v]
    in_specs += [pl.BlockSpec((w, l), lambda i, n: (0, i)), pl.BlockSpec((1, l, w), lambda i, n: (i, 0, 0))]
    args += [kxt, vx]
    return pl.pallas_call(
        functools.partial(_attn_kernel, window=window, qb=qb, nblocks=nb, e=e),
        grid=(b, steps),
        in_specs=in_specs,
        out_specs=pl.BlockSpec((1, qb * BLOCK, Q_DIM), cur),
        out_shape=jax.ShapeDtypeStruct((b, t, Q_DIM), BF16),
        compiler_params=_params("arbitrary", "arbitrary"),
        name="attention",
    )(*args)


def _seq_dft_kernel(ct_ref, st_ref, flip_ref, ab_ref, *rest, norm, n_cast):
    o_ref = rest[n_cast]
    _run_casts(rest[:n_cast], rest[n_cast + 1:])
    t = ab_ref.shape[1]
    half = t // 2
    a = ab_ref[0, :, 0:FFT_DIM]
    b = ab_ref[0, :, FFT_DIM:2 * FFT_DIM]
    p = _dot(ct_ref[...], a)
    q = _dot(st_ref[...], b)
    o_ref[0, 0:half] = ((p - q) * norm).astype(o_ref.dtype)
    sign = 1.0 - 2.0 * (lax.broadcasted_iota(jnp.int32, (t, 1), 0) % 2).astype(F32)
    nyquist = jnp.sum(a.astype(F32) * sign, axis=0, keepdims=True)
    first = lax.broadcasted_iota(jnp.int32, (half, 1), 0) == 0
    mirrored = (jnp.where(first, nyquist, p + q) * norm).astype(BF16)
    nblk = half // LANES
    rev = jnp.concatenate([_dot(flip_ref[...], mirrored[(nblk - 1 - j) * LANES:(nblk - j) * LANES])
                           for j in range(nblk)], axis=0)
    o_ref[0, half:t] = pltpu.roll(rev, 1, 0).astype(o_ref.dtype)


def _seq_dft(ab3, tables, casts=()):
    b, t, _ = ab3.shape
    cast_in, cast_out, cast_shape = _cast_rider(casts, b)
    return pl.pallas_call(
        functools.partial(_seq_dft_kernel, norm=float((t * FFT_GROUP) ** -0.5), n_cast=len(casts)),
        grid=(b,),
        in_specs=([_const_spec(a) for a in tables] + [pl.BlockSpec((1, t, 2 * FFT_DIM), lambda i: (i, 0, 0))]
                  + cast_in),
        out_specs=[pl.BlockSpec((1, t, FFT_DIM), lambda i: (i, 0, 0))] + cast_out,
        out_shape=[jax.ShapeDtypeStruct((b, t, FFT_DIM), BF16)] + cast_shape,
        compiler_params=_params("arbitrary"),
        name="seq_dft",
    )(*tables, ab3, *[a for a, _ in casts])


def _post_kernel(ap_ref, a_ref, an_ref, bp_ref, b_ref, bn_ref, xp_ref, x_ref, xn_ref,
                 wo_ref, gn1_ref, g1_ref, gn2_ref, sc_ref, sh_ref, g2_ref, gn3_ref,
                 wu_hbm, cw_ref, wd_hbm, o_ref, y_scr, hh_scr, xm_scr, act_scr, wu_ref, wd_ref, w_sem, *, seg):
    i = pl.program_id(0)
    tm = x_ref.shape[0]
    main = slice(HALO, HALO + tm)
    rd = i % 2
    wr = 1 - rd
    piece_rows = tm // POST_PIECES
    pieces = ([(0, xp_ref, 0, HALO)]
              + [(HALO + k * piece_rows, x_ref, k * piece_rows, piece_rows) for k in range(POST_PIECES)]
              + [(HALO + tm, xn_ref, 0, HALO)])
    ka = a_ref.shape[1]

    def prepare(piece):
        lo, ref, off, nrows = piece
        xm = ref[off:off + nrows] + g1_ref[0] * _rms(y_scr[lo:lo + nrows], gn1_ref[0])
        hn = _rms(xm, gn2_ref[0]) * (1.0 + sc_ref[0]) + sh_ref[0]
        hh_scr[wr, lo:lo + nrows] = hn.astype(BF16)
        if ref is x_ref:
            xm_scr[wr, off:off + nrows] = xm
        return jnp.concatenate([_zero_anchor(hn)] * (FFN_CHUNK // LANES), axis=1)

    def out_projection():
        y_scr[...] = (_dot(_ext(ap_ref, a_ref, an_ref), wo_ref[0, 0:ka, :])
                      + _dot(_ext(bp_ref, b_ref, bn_ref), wo_ref[0, ka:, :]))

    weight_copies = (pltpu.make_async_copy(wu_hbm, wu_ref, w_sem.at[0]),
                     pltpu.make_async_copy(wd_hbm, wd_ref, w_sem.at[1]))

    @pl.when(i == 0)
    def _():
        for copy in weight_copies:
            copy.start()
        y_scr[...] = jnp.zeros_like(y_scr)

    @pl.when(i == 2)
    def _():
        for copy in weight_copies:
            copy.wait()

    @pl.when(i < 2)
    def _():
        for piece in pieces:
            prepare(piece)
        out_projection()

    @pl.when(i >= 2)
    def _():
        hh = hh_scr[rd]
        h = hh[main]
        cw = cw_ref[0]
        anchor = None
        for c in range(D_FF // FFN_CHUNK):
            cs = slice(c * FFN_CHUNK, (c + 1) * FFN_CHUNK)
            cwc = cw[:, cs] if anchor is None else cw[:, cs] + anchor
            conv = _conv3(_dot(hh, wu_ref[0, :, cs]), seg, cwc, tile=i - 2)
            val = _dot(h, wu_ref[0, :, D_FF + c * FFN_CHUNK:D_FF + (c + 1) * FFN_CHUNK])
            act_scr[:, cs] = (conv / (1.0 + jnp.exp(-conv)) * val).astype(BF16)
            anchor = prepare(pieces[c]) if c < len(pieces) else None
        y2 = _dot(act_scr[...], wd_ref[0])
        out_projection()
        o_ref[...] = xm_scr[rd] + g2_ref[0] * _rms(y2, gn3_ref[0])


def _post(a2, b2, stream, tabs, layer, w_out, w_up, conv_w, w_down):
    x2 = stream.x
    rows, d = x2.shape
    tm = ROW_TILE
    seg = stream.seg
    assert seg % tm == 0 or tm % seg == 0
    n = rows // tm
    stage_a = lambda i: jnp.minimum(i, n - 1)
    stage_b = lambda i: jnp.clip(i - 1, 0, n - 1)
    stage_c = lambda i: jnp.maximum(i - 2, 0)
    row = stream.mod_row(min(tm, seg))
    mod = lambda chunk, stage: _mod_spec(layer, chunk, row, stage)
    return pl.pallas_call(
        functools.partial(_post_kernel, seg=seg),
        grid=(n + 2,),
        in_specs=(_halo_specs(tm, rows, a2.shape[1], stage_a) + _halo_specs(tm, rows, b2.shape[1], stage_a)
                  + _halo_specs(tm, rows, d, stage_b)
                  + [_layer_spec(w_out, 0), _gain_spec(layer, 1), mod(2, stage_b), _gain_spec(layer, 2),
                     mod(4, stage_b), mod(3, stage_b), mod(5, stage_c), _gain_spec(layer, 3),
                     pl.BlockSpec(memory_space=pl.ANY), _layer_spec(conv_w, layer),
                     pl.BlockSpec(memory_space=pl.ANY)]),
        out_specs=pl.BlockSpec((tm, d), lambda i: (stage_c(i), 0)),
        out_shape=jax.ShapeDtypeStruct((rows, d), F32),
        scratch_shapes=[pltpu.VMEM((tm + 2 * HALO, d), F32), pltpu.VMEM((2, tm + 2 * HALO, d), BF16),
                        pltpu.VMEM((2, tm, d), F32), pltpu.VMEM((tm, D_FF), BF16),
                        pltpu.VMEM(w_up.shape, BF16), pltpu.VMEM(w_down.shape, BF16),
                        pltpu.SemaphoreType.DMA((2,))],
        compiler_params=_params("arbitrary"),
        name="post",
    )(a2, a2, a2, b2, b2, b2, x2, x2, x2, w_out, tabs.gains, tabs.mods, tabs.gains, tabs.mods, tabs.mods,
      tabs.mods, tabs.gains, w_up, conv_w, w_down)


def _rope_tables(seq):
    half = HEAD_DIM // 2
    nf = half // 2
    inv = ROPE_BASE ** (-np.arange(nf, dtype=np.float64) / nf)
    t = np.arange(seq)
    lane = np.arange(LANES)
    in_head = lane % HEAD_DIM
    pos = np.where((in_head // half)[None, :] == 0, (t // GRID_W)[:, None], (t % GRID_W)[:, None])
    ang = pos * inv[lane % nf][None, :]
    first = ((lane % half) < nf)[None, :]
    cos = np.cos(ang)
    sin_a = np.where(first, -np.sin(ang), 0.0)
    sin_b = np.where(first, 0.0, np.sin(ang))
    return tuple(jnp.asarray(a, F32) for a in (cos, sin_a, sin_b))


def _dft_tables(n, rows=None):
    k = np.arange(n if rows is None else rows)
    ang = 2.0 * np.pi * ((k[:, None] * np.arange(n)[None, :]) % n) / n
    return jnp.asarray(np.cos(ang), F32).astype(BF16), jnp.asarray(np.sin(ang), F32).astype(BF16)


def _seq_dft_tables(n):
    assert (n // 2) % LANES == 0
    flip = np.fliplr(np.eye(LANES, dtype=np.float32))
    return _dft_tables(n, n // 2) + (jnp.asarray(flip).astype(BF16),)


def kernel(x, c, ctx, c_ctx, mod_w, mod_b, norm_g, att_w_in, att_sink, sconv_w, att_w_out, mix_w_in,
           pool_w_grp, pool_scale, mix_w_out, ffn_w_up, ffn_conv, ffn_w_down):
    bsz, seq, d = x.shape
    ctx_len = ctx.shape[1]
    assert d == D_MODEL and seq % ROW_TILE == 0 and (bsz * ctx_len) % ROW_TILE == 0 and ROW_TILE % ctx_len == 0
    assert bsz < MOD_ROWS and norm_g.shape[1] == N_GAINS

    cc = jnp.concatenate([c, c_ctx[None, :], jnp.zeros((MOD_ROWS - bsz - 1, d), F32)], axis=0)
    mods = _modulation(cc, mod_w, mod_b)
    tabs = _Tables(mods, norm_g.reshape(DEPTH * N_GAINS, 1, d))

    rope = _rope_tables(seq)
    chan_cs = jnp.concatenate(_dft_tables(FFT_GROUP), axis=1)
    seq_tabs = {seq: _seq_dft_tables(seq), ctx_len: _seq_dft_tables(ctx_len)}

    w_att_in, w_mix_in = att_w_in, mix_w_in
    g4 = pool_w_grp.astype(BF16)
    zero = jnp.zeros_like(g4[:, 0])
    w_grp = jnp.stack([jnp.concatenate([jnp.concatenate([g4[:, 2 * j], zero], axis=2),
                                        jnp.concatenate([zero, g4[:, 2 * j + 1]], axis=2)], axis=1)
                       for j in range(N_POOL_GROUPS // 2)], axis=1)
    p_scale = pool_scale.reshape(-1, 1, POOL_DIM)

    xs = x.reshape(bsz * seq, d)
    hc = ctx.reshape(bsz * ctx_len, d)

    for i in range(DEPTH):
        even = i % 2 == 0
        ctx_live = any(j % 2 == 0 for j in range(i + 1, DEPTH))
        lat = _Stream(xs, seq, None)
        con = _Stream(hc, ctx_len, bsz)
        if even:
            e = i // 2
            casts = ((att_w_out, e), (ffn_w_up, i), (ffn_w_down, i))
            q, kt, v, sx, w_out, w_up, w_down = _in_even(lat, tabs, i, e, w_att_in, sconv_w, rope, casts)
            post = (w_out, w_up, ffn_conv, w_down)
            if ctx_live:
                qc, kct, vc, sxc = _in_even(con, tabs, i, e, w_att_in, sconv_w, None)
            else:
                kct, vc = _in_kv(con, tabs, i, e, w_att_in)
            vc3 = vc.reshape(bsz, ctx_len, -1)
            ax = _attention(att_sink, e, q.reshape(bsz, seq, -1), kt, v.reshape(bsz, seq, -1), kct, vc3, True)
            xs = _post(ax.reshape(bsz * seq, -1), sx, lat, tabs, i, *post)
            if ctx_live:
                ac = _attention(att_sink, e, qc.reshape(bsz, ctx_len, -1), None, None, kct, vc3, False)
                hc = _post(ac.reshape(bsz * ctx_len, -1), sxc, con, tabs, i, *post)
        else:
            o = i // 2
            casts = ((mix_w_out, o), (ffn_w_up, i), (ffn_w_down, i))
            yp, ab = _in_odd(lat, tabs, i, o, w_mix_in, chan_cs, w_grp, p_scale)
            yf, w_out, w_up, w_down = _seq_dft(ab.reshape(bsz, seq, -1), seq_tabs[seq], casts)
            post = (w_out, w_up, ffn_conv, w_down)
            if ctx_live:
                ypc, abc = _in_odd(con, tabs, i, o, w_mix_in, chan_cs, w_grp, p_scale)
                yfc, = _seq_dft(abc.reshape(bsz, ctx_len, -1), seq_tabs[ctx_len])
            xs = _post(yp, yf.reshape(bsz * seq, -1), lat, tabs, i, *post)
            if ctx_live:
                hc = _post(ypc, yfc.reshape(bsz * ctx_len, -1), con, tabs, i, *post)
    return xs.reshape(bsz, seq, d)
```
